```python
import jax, jax.numpy as jnp
from jax import lax
import numpy as np

D_MODEL = 4096
BATCH = 4
SEQ = 4096
DEPTH = 1
DEC_BATCH = 32
DEC_SEQ = 16
PAST_LEN = 4096

CHUNK = 64
Q_BLOCK = 128
MIX_WIDTH = D_MODEL
MLA_HEADS = 16
QK_NOPE = 128
QK_ROPE = 64
V_DIM = 128
Q_LORA = 1024
KV_LORA = 512
MLA_WIDTH = MLA_HEADS * V_DIM
ATTN_SCALE = float((QK_NOPE + QK_ROPE) ** -0.5)
ROPE_THETA = 10000.0
HG_WIDTH = MIX_WIDTH - MLA_WIDTH
HG_HEAD_DIM = 128
HG_HEADS = HG_WIDTH // HG_HEAD_DIM
D_FF = 11008
CONV_W = 3
PLE_DIM = 256
EPS = 1e-6
IN_WIDTH = Q_LORA + KV_LORA + QK_ROPE + 4 * HG_WIDTH

kernel_name = "hybrid_mla_hgrn2_streaming_step"

F32 = jnp.float32


def rmsnorm(x, g):
    xf = x.astype(F32)
    y = xf * lax.rsqrt(jnp.mean(xf * xf, axis=-1, keepdims=True) + EPS)
    return (y * g.astype(F32)).astype(x.dtype)


def rope(x, pos):
    half = QK_ROPE // 2
    inv = 1.0 / (ROPE_THETA ** (jnp.arange(half, dtype=F32) / half))
    ang = pos.astype(F32)[:, None] * inv[None, :]
    cos = jnp.cos(ang)[None, :, None, :]
    sin = jnp.sin(ang)[None, :, None, :]
    xf = x.astype(F32)
    x1, x2 = xf[..., :half], xf[..., half:]
    return jnp.concatenate([x1 * cos - x2 * sin, x2 * cos + x1 * sin], axis=-1).astype(x.dtype)


def mixer_inputs(h, pos, lb, w_in, g_q_a, w_uq, g_kv_a, g_q_nope, g_q_rope, g_k_rope):
    B, T, _ = h.shape
    z = h @ w_in
    o1 = Q_LORA
    o2 = o1 + KV_LORA
    o3 = o2 + QK_ROPE
    cq, ckv, kr, zh = z[..., :o1], z[..., o1:o2], z[..., o2:o3], z[..., o3:]
    q = (rmsnorm(cq, g_q_a) @ w_uq).reshape(B, T, MLA_HEADS, QK_NOPE + QK_ROPE)
    q_nope = rmsnorm(q[..., :QK_NOPE], g_q_nope)
    q_rope = rope(rmsnorm(q[..., QK_NOPE:], g_q_rope), pos)
    ckv = rmsnorm(ckv, g_kv_a)
    kr = rope(rmsnorm(kr, g_k_rope)[:, :, None, :], pos)[:, :, 0, :]
    hq, hf, hi, hg = jnp.split(zh, 4, axis=-1)
    lbf = lb.astype(F32)
    hff = hf.astype(F32)
    log_f = jnp.log(lbf + (1.0 - lbf) * jax.nn.sigmoid(hff))
    k = (1.0 - lbf) * jax.nn.sigmoid(-hff)
    heads = lambda a: a.reshape(B, T, HG_HEADS, HG_HEAD_DIM)
    hgrn = (heads(hq.astype(F32)), heads(k), heads(hi.astype(F32)), heads(log_f))
    return (q_nope, q_rope, ckv, kr), hgrn, hg


def mla_keys(ckv, w_uk, g_k_nope):
    return rmsnorm(jnp.einsum('bkc,chd->bkhd', ckv, w_uk), g_k_nope)


def mla_scores(q_nope, q_rope, k_nope, k_rope):
    s = (jnp.einsum('bqhd,bkhd->bhqk', q_nope, k_nope).astype(F32)
         + jnp.einsum('bqhr,bkr->bhqk', q_rope, k_rope).astype(F32))
    return s * ATTN_SCALE


def mla_prompt(q_nope, q_rope, ckv, kr, w_uk, w_uv, g_k_nope):
    B, S = ckv.shape[0], ckv.shape[1]
    k_nope = mla_keys(ckv, w_uk, g_k_nope)
    v = jnp.einsum('bkc,chd->bkhd', ckv, w_uv)
    k_chunk = jnp.arange(S) // CHUNK

    def block(i):
        start = i * Q_BLOCK
        qn = lax.dynamic_slice_in_dim(q_nope, start, Q_BLOCK, axis=1)
        qr = lax.dynamic_slice_in_dim(q_rope, start, Q_BLOCK, axis=1)
        s = mla_scores(qn, qr, k_nope, kr)
        q_chunk = (start + jnp.arange(Q_BLOCK)) // CHUNK
        mask = k_chunk[None, :] <= q_chunk[:, None]
        p = jax.nn.softmax(jnp.where(mask[None, None], s, -jnp.inf), axis=-1).astype(v.dtype)
        return jnp.einsum('bhqk,bkhd->bqhd', p, v)

    o = lax.map(block, jnp.arange(S // Q_BLOCK))
    return jnp.moveaxis(o, 0, 1).reshape(B, S, MLA_WIDTH)


def mla_sample(q_nope, q_rope, ckv_all, kr_all, w_uk, w_uv, g_k_nope):
    B, T = q_nope.shape[0], q_nope.shape[1]
    k_nope = mla_keys(ckv_all, w_uk, g_k_nope)
    p = jax.nn.softmax(mla_scores(q_nope, q_rope, k_nope, kr_all), axis=-1).astype(ckv_all.dtype)
    o_lat = jnp.einsum('bhqk,bkc->bqhc', p, ckv_all)
    o = jnp.einsum('bqhc,chd->bqhd', o_lat, w_uv)
    return o.reshape(B, T, MLA_WIDTH)


def hgrn2_chunk(S, q, k, v, log_f):
    L = q.shape[1]
    b = jnp.cumsum(log_f, axis=1)
    inter = jnp.einsum('bthk,bhkv->bthv', q * jnp.exp(b), S)
    causal = jnp.tril(jnp.ones((L, L), dtype=bool))
    decay = jnp.where(causal[None, :, :, None, None], b[:, :, None] - b[:, None, :], -jnp.inf)
    qd = q[:, :, None] * jnp.exp(decay)
    A = jnp.einsum('btshk,bshk->bhts', qd, k)
    intra = jnp.einsum('bhts,bshv->bthv', A, v)
    b_last = b[:, -1]
    S_new = (jnp.exp(b_last)[..., None] * S
             + jnp.einsum('bshk,bshv->bhkv', k * jnp.exp(b_last[:, None] - b), v))
    return S_new, inter + intra


def hgrn2_prompt(q, k, v, log_f):
    B, S = q.shape[0], q.shape[1]
    n = S // CHUNK
    to_chunks = lambda a: jnp.moveaxis(a.reshape(B, n, CHUNK, *a.shape[2:]), 1, 0)
    S0 = jnp.zeros((B, HG_HEADS, HG_HEAD_DIM, HG_HEAD_DIM), F32)
    S_fin, o = lax.scan(lambda c, xs: hgrn2_chunk(c, *xs), S0,
                        (to_chunks(q), to_chunks(k), to_chunks(v), to_chunks(log_f)))
    return jnp.moveaxis(o, 0, 1).reshape(B, S, HG_HEADS, HG_HEAD_DIM), S_fin


def hgrn2_readout(o, hg, g_hg_out, dtype):
    B, T = o.shape[0], o.shape[1]
    gate = jax.nn.silu(hg.astype(F32)).reshape(B, T, HG_HEADS, HG_HEAD_DIM)
    return (rmsnorm(o, g_hg_out) * gate).reshape(B, T, HG_WIDTH).astype(dtype)


def conv_ffn_ple(x, hist, p, g_ffn_norm, w_up, conv_w, conv_b, w_down, g_ple_norm, w_ple_gate, w_ple):
    T = x.shape[1]
    u = rmsnorm(x, g_ffn_norm) @ w_up
    u_all = jnp.concatenate([hist.astype(u.dtype), u], axis=1)
    c = conv_b + sum(conv_w[j] * u_all[:, j:j + T] for j in range(CONV_W))
    c_gate, c_up = jnp.split(c, 2, axis=-1)
    x = x + (jax.nn.silu(c_gate) * c_up) @ w_down
    gate = jax.nn.sigmoid(rmsnorm(x, g_ple_norm) @ w_ple_gate)
    x = x + (p.astype(x.dtype) @ w_ple) * gate
    return x, u_all[:, -(CONV_W - 1):]


def setup_inputs(seed: int = 0) -> dict:
    key = jax.random.key(seed)
    ks = jax.random.split(key, 32)
    nrm = lambda k, shape, s: jax.random.normal(k, shape, F32) * s
    gain = lambda k, shape: 1.0 + 0.05 * jax.random.normal(k, shape, F32)
    return {
        "x_prompt": nrm(ks[0], (BATCH, SEQ, D_MODEL), 1.0),
        "x_sample": nrm(ks[1], (DEC_BATCH, DEC_SEQ, D_MODEL), 1.0),
        "cache_ckv": nrm(ks[2], (DEPTH, DEC_BATCH, PAST_LEN, KV_LORA), 1.0),
        "cache_k_rope": nrm(ks[3], (DEPTH, DEC_BATCH, PAST_LEN, QK_ROPE), 1.0),
        "state_hgrn": nrm(ks[4], (DEPTH, DEC_BATCH, HG_HEADS, HG_HEAD_DIM, HG_HEAD_DIM), 0.3),
        "state_ffn_conv": nrm(ks[5], (DEPTH, DEC_BATCH, CONV_W - 1, 2 * D_FF), 1.0),
        "p_prompt": nrm(ks[6], (DEPTH, BATCH, SEQ, PLE_DIM), 1.0),
        "p_sample": nrm(ks[7], (DEPTH, DEC_BATCH, DEC_SEQ, PLE_DIM), 1.0),
        "g_attn_norm": gain(ks[8], (DEPTH, D_MODEL)),
        "w_in": nrm(ks[9], (DEPTH, D_MODEL, IN_WIDTH), D_MODEL ** -0.5),
        "g_q_a": gain(ks[10], (DEPTH, Q_LORA)),
        "w_uq": nrm(ks[11], (DEPTH, Q_LORA, MLA_HEADS * (QK_NOPE + QK_ROPE)), Q_LORA ** -0.5),
        "g_kv_a": gain(ks[12], (DEPTH, KV_LORA)),
        "w_uk": nrm(ks[13], (DEPTH, KV_LORA, MLA_HEADS, QK_NOPE), KV_LORA ** -0.5),
        "w_uv": nrm(ks[14], (DEPTH, KV_LORA, MLA_HEADS, V_DIM), KV_LORA ** -0.5),
        "g_q_nope": gain(ks[15], (DEPTH, QK_NOPE)),
        "g_q_rope": gain(ks[16], (DEPTH, QK_ROPE)),
        "g_k_nope": gain(ks[17], (DEPTH, QK_NOPE)),
        "g_k_rope": gain(ks[18], (DEPTH, QK_ROPE)),
        "lb_logits": nrm(ks[19], (DEPTH + 1, HG_WIDTH), 0.5),
        "g_hg_out": gain(ks[20], (DEPTH, HG_HEAD_DIM)),
        "w_o": nrm(ks[21], (DEPTH, MIX_WIDTH, D_MODEL), MIX_WIDTH ** -0.5),
        "g_ffn_norm": gain(ks[22], (DEPTH, D_MODEL)),
        "w_up": nrm(ks[23], (DEPTH, D_MODEL, 2 * D_FF), D_MODEL ** -0.5),
        "conv_w": nrm(ks[24], (DEPTH, CONV_W, 2 * D_FF), CONV_W ** -0.5),
        "conv_b": nrm(ks[25], (DEPTH, 2 * D_FF), 0.01),
        "w_down": nrm(ks[26], (DEPTH, D_FF, D_MODEL), D_FF ** -0.5),
        "g_ple_norm": gain(ks[27], (DEPTH, D_MODEL)),
        "w_ple_gate": nrm(ks[28], (DEPTH, D_MODEL, D_MODEL), D_MODEL ** -0.5),
        "w_ple": nrm(ks[29], (DEPTH, PLE_DIM, D_MODEL), PLE_DIM ** -0.5),
    }


def reference(x_prompt, x_sample, cache_ckv, cache_k_rope, state_hgrn, state_ffn_conv, p_prompt, p_sample,
              g_attn_norm, w_in, g_q_a, w_uq, g_kv_a, w_uk, w_uv, g_q_nope, g_q_rope, g_k_nope, g_k_rope,
              lb_logits, g_hg_out, w_o, g_ffn_norm, w_up, conv_w, conv_b, w_down, g_ple_norm, w_ple_gate, w_ple):
    S = x_prompt.shape[1]
    T = x_sample.shape[1]
    pos_p = jnp.arange(S)
    pos_s = PAST_LEN + jnp.arange(T)
    lb_all = jnp.cumsum(jax.nn.softmax(lb_logits.astype(F32), axis=0), axis=0)
    xp, xs = x_prompt, x_sample
    ckv_p_l, kr_p_l, hg_p_l, cv_p_l = [], [], [], []
    ckv_s_l, kr_s_l, hg_s_l, cv_s_l = [], [], [], []
    for i in range(DEPTH):
        proj = (lb_all[i], w_in[i], g_q_a[i], w_uq[i], g_kv_a[i], g_q_nope[i], g_q_rope[i], g_k_rope[i])
        ffn = (g_ffn_norm[i], w_up[i], conv_w[i], conv_b[i], w_down[i], g_ple_norm[i], w_ple_gate[i], w_ple[i])
        (qn, qr, ckv, kr), (hq, hk, hv, hlf), hg = mixer_inputs(rmsnorm(xp, g_attn_norm[i]), pos_p, *proj)
        o_mla = mla_prompt(qn, qr, ckv, kr, w_uk[i], w_uv[i], g_k_nope[i])
        o_h, s_fin = hgrn2_prompt(hq, hk, hv, hlf)
        o_hg = hgrn2_readout(o_h, hg, g_hg_out[i], xp.dtype)
        xp = xp + jnp.concatenate([o_mla, o_hg], axis=-1) @ w_o[i]
        hist0 = jnp.zeros((xp.shape[0], CONV_W - 1, 2 * D_FF), xp.dtype)
        xp, cv_p = conv_ffn_ple(xp, hist0, p_prompt[i], *ffn)
        ckv_p_l.append(ckv)
        kr_p_l.append(kr)
        hg_p_l.append(s_fin.astype(x_prompt.dtype))
        cv_p_l.append(cv_p)
        (qn, qr, ckv, kr), (hq, hk, hv, hlf), hg = mixer_inputs(rmsnorm(xs, g_attn_norm[i]), pos_s, *proj)
        ckv_all = jnp.concatenate([cache_ckv[i].astype(ckv.dtype), ckv], axis=1)
        kr_all = jnp.concatenate([cache_k_rope[i].astype(kr.dtype), kr], axis=1)
        o_mla = mla_sample(qn, qr, ckv_all, kr_all, w_uk[i], w_uv[i], g_k_nope[i])
        s_new, o_h = hgrn2_chunk(state_hgrn[i].astype(F32), hq, hk, hv, hlf)
        o_hg = hgrn2_readout(o_h, hg, g_hg_out[i], xs.dtype)
        xs = xs + jnp.concatenate([o_mla, o_hg], axis=-1) @ w_o[i]
        xs, cv_s = conv_ffn_ple(xs, state_ffn_conv[i], p_sample[i], *ffn)
        ckv_s_l.append(ckv)
        kr_s_l.append(kr)
        hg_s_l.append(s_new.astype(state_hgrn.dtype))
        cv_s_l.append(cv_s)
    return (xp, xs,
            jnp.stack(ckv_p_l), jnp.stack(kr_p_l), jnp.stack(hg_p_l), jnp.stack(cv_p_l),
            jnp.stack(ckv_s_l), jnp.stack(kr_s_l), jnp.stack(hg_s_l), jnp.stack(cv_s_l))
```

```python
import functools

import numpy as np
import jax
import jax.numpy as jnp
from jax import lax
from jax.experimental import pallas as pl
from jax.experimental.pallas import tpu as pltpu

F32 = jnp.float32
BF16 = jnp.bfloat16

EPS = 1e-6
CHUNK = 64
ROPE_THETA = 10000.0
LANES = 128
NEG_BIG = -1e30
VMEM_LIMIT_BYTES = 56 * 1024 * 1024


def _params(*sem):
    return pltpu.CompilerParams(dimension_semantics=sem, vmem_limit_bytes=VMEM_LIMIT_BYTES)


def _dot(a, b):
    return jnp.dot(a, b, preferred_element_type=F32)


def _dot_nt(a, b):
    return lax.dot_general(a, b, (((1,), (1,)), ((), ())), preferred_element_type=F32)


def _dot_tn(a, b):
    return lax.dot_general(a, b, (((0,), (0,)), ((), ())), preferred_element_type=F32)


def _rms(x, g):
    ms = jnp.mean(x * x, axis=-1, keepdims=True)
    return x * lax.rsqrt(ms + EPS) * g


def _slab_rms(x, g, valid):
    outs = []
    for a in range(x.shape[1] // LANES):
        blk = x[:, a * LANES:(a + 1) * LANES]
        ms = jnp.sum(blk * blk, axis=-1, keepdims=True) * (1.0 / valid)
        outs.append(blk * lax.rsqrt(ms + EPS) * g[:, a * LANES:(a + 1) * LANES])
    return outs


def _rope_slab(y, c, sa, sb):
    return y * c + pltpu.roll(y, 96, 1) * sa + pltpu.roll(y, 32, 1) * sb


def _rmsnorm_kernel(x_ref, g_ref, o_ref):
    o_ref[...] = _rms(x_ref[...], g_ref[...]).astype(o_ref.dtype)


def _rmsnorm(x, g, tm):
    m, d = x.shape
    tm = min(tm, m)
    assert m % tm == 0
    return pl.pallas_call(
        _rmsnorm_kernel,
        grid=(m // tm,),
        in_specs=[pl.BlockSpec((tm, d), lambda i: (i, 0)), pl.BlockSpec((1, d), lambda i: (0, 0))],
        out_specs=pl.BlockSpec((tm, d), lambda i: (i, 0)),
        out_shape=jax.ShapeDtypeStruct((m, d), BF16),
        compiler_params=_params("parallel"),
        name="rmsnorm",
    )(x, g.reshape(1, d))


def _mm_kernel(*refs, body, n_lhs, group_sizes, n_in):
    ins, outs = refs[:n_in], refs[n_in:]
    lhs, rhs, rest = ins[:n_lhs], ins[n_lhs:2 * n_lhs], ins[2 * n_lhs:]
    accs, k = [], 0
    for gs in group_sizes:
        acc = None
        for _ in range(gs):
            d = _dot(lhs[k][...], rhs[k][...])
            acc = d if acc is None else acc + d
            k += 1
        accs.append(acc)
    res = body(*accs, *[r[...] for r in rest])
    if not isinstance(res, (tuple, list)):
        res = (res,)
    for o, r in zip(outs, res):
        o[...] = r.astype(o.dtype)


def _mm(name, body, lhs, rhs, *, tm, tn, out_dtypes, group_sizes=None, cols=(), rows=(), extras=()):
    m, n = lhs[0].shape[0], rhs[0].shape[1]
    tm, tn = min(tm, m), min(tn, n)
    assert m % tm == 0 and n % tn == 0, (name, m, n, tm, tn)
    group_sizes = tuple(group_sizes or (len(lhs),))
    in_specs = [pl.BlockSpec((tm, a.shape[1]), lambda i, j: (i, 0)) for a in lhs]
    in_specs += [pl.BlockSpec((b.shape[0], tn), lambda i, j: (0, j)) for b in rhs]
    in_specs += [pl.BlockSpec((c.shape[0], tn), lambda i, j: (0, j)) for c in cols]
    in_specs += [pl.BlockSpec((tm, tn), lambda i, j: (i, j)) for _ in rows]
    in_specs += [spec for _, spec in extras]
    args = list(lhs) + list(rhs) + list(cols) + list(rows) + [a for a, _ in extras]
    kern = functools.partial(_mm_kernel, body=body, n_lhs=len(lhs), group_sizes=group_sizes, n_in=len(args))
    outs = pl.pallas_call(
        kern,
        grid=(m // tm, n // tn),
        in_specs=in_specs,
        out_specs=[pl.BlockSpec((tm, tn), lambda i, j: (i, j)) for _ in out_dtypes],
        out_shape=[jax.ShapeDtypeStruct((m, n), dt) for dt in out_dtypes],
        compiler_params=_params("parallel", "arbitrary"),
        name=name,
    )(*args)
    return outs if len(outs) > 1 else outs[0]


def _latent_kernel(h_ref, w_ref, gq_ref, gkv_ref, gkr_ref, c_ref, sa_ref, sb_ref,
                   cq_ref, ckv_ref, ckvb_ref, kr_ref, krb_ref, *, q_lora, kv_lora, rope):
    z = _dot(h_ref[...], w_ref[...])
    cq_ref[...] = _rms(z[:, :q_lora], gq_ref[...]).astype(cq_ref.dtype)
    ckv = _rms(z[:, q_lora:q_lora + kv_lora], gkv_ref[...])
    ckv_ref[...] = ckv
    ckvb_ref[...] = ckv.astype(ckvb_ref.dtype)
    (kr,) = _slab_rms(z[:, q_lora + kv_lora:], gkr_ref[...], rope)
    kr = _rope_slab(kr, c_ref[...], sa_ref[...], sb_ref[...])
    kr_ref[...] = kr[:, :rope]
    krb_ref[...] = kr.astype(krb_ref.dtype)


def _latent(h, w_lat, g_q_a, g_kv_a, g_k_rope_pad, tabs, *, q_lora, kv_lora, rope, tm):
    m, d = h.shape
    tm = min(tm, m)
    n = w_lat.shape[1]
    n_tab = tabs[0].shape[0] // tm
    row = lambda w: pl.BlockSpec((tm, w), lambda i: (i, 0))
    const = lambda r, w: pl.BlockSpec((r, w), lambda i: (0, 0))
    tab = pl.BlockSpec((tm, LANES), lambda i: (i % n_tab, 0))
    kern = functools.partial(_latent_kernel, q_lora=q_lora, kv_lora=kv_lora, rope=rope)
    return pl.pallas_call(
        kern,
        grid=(m // tm,),
        in_specs=[row(d), const(d, n), const(1, q_lora), const(1, kv_lora), const(1, LANES), tab, tab, tab],
        out_specs=[row(q_lora), row(kv_lora), row(kv_lora), row(rope), row(LANES)],
        out_shape=[jax.ShapeDtypeStruct((m, q_lora), BF16), jax.ShapeDtypeStruct((m, kv_lora), F32),
                   jax.ShapeDtypeStruct((m, kv_lora), BF16), jax.ShapeDtypeStruct((m, rope), F32),
                   jax.ShapeDtypeStruct((m, LANES), BF16)],
        compiler_params=_params("parallel"),
        name="mla_latent",
    )(h, w_lat, g_q_a.reshape(1, -1), g_kv_a.reshape(1, -1), g_k_rope_pad, *tabs)


def _attn_kernel(qn_ref, qr_ref, kn_ref, kr_ref, v_ref, o_ref, *, tq, chunk):
    qi = pl.program_id(2)
    q = jnp.concatenate([qn_ref[...], qr_ref[...]], axis=-1)

    def block(kb, carry, masked):
        m_i, l_i, acc = carry
        off = pl.multiple_of(kb * tq, tq)
        k = jnp.concatenate([kn_ref[pl.ds(off, tq), :], kr_ref[pl.ds(off, tq), :]], axis=-1)
        s = _dot_nt(q, k)
        if masked:
            r = lax.broadcasted_iota(jnp.int32, (tq, tq), 0) // chunk
            c = lax.broadcasted_iota(jnp.int32, (tq, tq), 1) // chunk
            s = jnp.where(c <= r, s, NEG_BIG)
        m_new = jnp.maximum(m_i, jnp.max(s, axis=-1, keepdims=True))
        alpha = jnp.exp(m_i - m_new)
        p = jnp.exp(s - m_new)
        l_new = alpha * l_i + jnp.sum(p, axis=-1, keepdims=True)
        acc = alpha * acc + _dot(p.astype(BF16), v_ref[pl.ds(off, tq), :])
        return m_new, l_new, acc

    init = (jnp.full((tq, 1), NEG_BIG, F32), jnp.zeros((tq, 1), F32), jnp.zeros((tq, LANES), F32))
    carry = lax.fori_loop(0, qi, lambda kb, c: block(kb, c, False), init)
    _, l_i, acc = block(qi, carry, True)
    o_ref[...] = (acc / l_i).astype(o_ref.dtype)


def _attn_prompt(qn, qr, kn, krb, v, *, batch, seq, heads, tq):
    tq = min(tq, seq)
    nq = seq // tq
    qspec = pl.BlockSpec((tq, LANES), lambda b, h, i: (b * nq + i, h))
    kspec = pl.BlockSpec((seq, LANES), lambda b, h, i: (b, h))
    kern = functools.partial(_attn_kernel, tq=tq, chunk=CHUNK)
    return pl.pallas_call(
        kern,
        grid=(batch, heads, nq),
        in_specs=[qspec, qspec, kspec, pl.BlockSpec((seq, LANES), lambda b, h, i: (b, 0)), kspec],
        out_specs=qspec,
        out_shape=jax.ShapeDtypeStruct(qn.shape, BF16),
        compiler_params=_params("parallel", "parallel", "arbitrary"),
        name="mla_attn_prompt",
    )(qn, qr, kn, krb, v)


def _absorb_kernel(x_ref, w_ref, o_ref):
    o_ref[0] = _dot(x_ref[...], w_ref[0]).astype(o_ref.dtype)


def _per_head_mm(name, x, w):
    m = x.shape[0]
    heads, _, n = w.shape
    return pl.pallas_call(
        _absorb_kernel,
        grid=(heads,),
        in_specs=[pl.BlockSpec((m, LANES), lambda h: (0, h)), pl.BlockSpec((1, LANES, n), lambda h: (h, 0, 0))],
        out_specs=pl.BlockSpec((1, m, n), lambda h: (h, 0, 0)),
        out_shape=jax.ShapeDtypeStruct((heads, m, n), BF16),
        compiler_params=_params("parallel"),
        name=name,
    )(x, w)


def _vup_kernel(x_ref, w_ref, o_ref):
    o_ref[...] = _dot(x_ref[0], w_ref[0]).astype(o_ref.dtype)


def _value_up(o_lat, w):
    heads, m, c = o_lat.shape
    return pl.pallas_call(
        _vup_kernel,
        grid=(heads,),
        in_specs=[pl.BlockSpec((1, m, c), lambda h: (h, 0, 0)), pl.BlockSpec((1, c, LANES), lambda h: (h, 0, 0))],
        out_specs=pl.BlockSpec((m, LANES), lambda h: (0, h)),
        out_shape=jax.ShapeDtypeStruct((m, heads * LANES), BF16),
        compiler_params=_params("parallel"),
        name="mla_value_up",
    )(o_lat, w)


def _attn_sample_kernel(qt_ref, qr_ref, cache_ref, kcache_ref, cnew_ref, knew_ref, wkt_ref, o_ref,
                        s_scr, *, heads, t_new, past, tk, nope, rope):
    hq = heads * t_new
    c_lat = cache_ref.shape[-1]
    qt = qt_ref[...].reshape(hq, c_lat)
    qr = qr_ref[...].reshape(hq, LANES)[:, :rope]
    wkt = wkt_ref[...]

    def scores(c_b, kr_b):
        n = c_b.shape[0]
        kn_t = _dot_nt(wkt, c_b)
        ssq = jnp.sum((kn_t * kn_t).reshape(heads, nope, n), axis=1)
        rinv = lax.rsqrt(ssq * (1.0 / nope) + EPS)
        rinv = jnp.broadcast_to(rinv[:, None, :], (heads, t_new, n)).reshape(hq, n)
        return _dot_nt(qt, c_b) * rinv + _dot_nt(qr, kr_b)

    for kt in range(past // tk):
        c_b = cache_ref[0, kt * tk:(kt + 1) * tk, :].astype(BF16)
        kr_b = kcache_ref[0, kt * tk:(kt + 1) * tk, :].astype(BF16)
        s_scr[:, kt * tk:(kt + 1) * tk] = scores(c_b, kr_b)
    c_new = cnew_ref[...]
    s_new = scores(c_new, knew_ref[...][:, :rope])

    s_old = s_scr[...]
    m_i = jnp.maximum(jnp.max(s_old, axis=-1, keepdims=True), jnp.max(s_new, axis=-1, keepdims=True))
    p_new = jnp.exp(s_new - m_i)
    l_i = jnp.sum(p_new, axis=-1, keepdims=True)
    acc = _dot(p_new.astype(BF16), c_new)
    for kt in range(past // tk):
        p = jnp.exp(s_scr[:, kt * tk:(kt + 1) * tk] - m_i)
        l_i = l_i + jnp.sum(p, axis=-1, keepdims=True)
        acc = acc + _dot(p.astype(BF16), cache_ref[0, kt * tk:(kt + 1) * tk, :].astype(BF16))
    o_ref[...] = (acc / l_i).reshape(heads, t_new, c_lat).astype(o_ref.dtype)


def _attn_sample(qt, qr_h, cache, kcache, ckv_new, kr_new, wkt, *, batch, t_new, heads, nope, rope, tk):
    past, c_lat = cache.shape[1], cache.shape[2]
    tk = min(tk, past)
    kern = functools.partial(_attn_sample_kernel, heads=heads, t_new=t_new, past=past, tk=tk, nope=nope, rope=rope)
    return pl.pallas_call(
        kern,
        grid=(batch,),
        in_specs=[pl.BlockSpec((heads, t_new, c_lat), lambda b: (0, b, 0)),
                  pl.BlockSpec((heads, t_new, LANES), lambda b: (0, b, 0)),
                  pl.BlockSpec((1, past, c_lat), lambda b: (b, 0, 0)),
                  pl.BlockSpec((1, past, rope), lambda b: (b, 0, 0)),
                  pl.BlockSpec((t_new, c_lat), lambda b: (b, 0)),
                  pl.BlockSpec((t_new, LANES), lambda b: (b, 0)),
                  pl.BlockSpec(wkt.shape, lambda b: (0, 0))],
        out_specs=pl.BlockSpec((heads, t_new, c_lat), lambda b: (0, b, 0)),
        out_shape=jax.ShapeDtypeStruct((heads, batch * t_new, c_lat), BF16),
        scratch_shapes=[pltpu.VMEM((heads * t_new, past), F32)],
        compiler_params=_params("parallel"),
        name="mla_attn_sample",
    )(qt, qr_h, cache, kcache, ckv_new, kr_new, wkt)


def _hgrn_tables(blk):
    t = np.arange(blk)
    levels = [blk >> (i + 1) for i in range(int(np.log2(blk)))]
    lvl = np.full((blk, blk), -1, np.int32)
    lvl[t, t] = 0
    for li, c in enumerate(levels, start=1):
        same_pair = (t[:, None] // (2 * c)) == (t[None, :] // (2 * c))
        split = (t[:, None] // c) != (t[None, :] // c)
        lvl[same_pair & split & (t[:, None] > t[None, :])] = li
    prefix = (t[None, :] <= t[:, None]).astype(np.float32)
    rows = [prefix]
    for c in (2, 1):
        mid = (t // (2 * c)) * (2 * c) + c - 1
        rows.append((t[None, :] <= mid[:, None]).astype(np.float32))
    return levels, jnp.asarray(lvl), jnp.asarray(np.concatenate(rows, axis=0), dtype=BF16)


def _hgrn_unit(q, k, v, lf, gate, s_t, sel, lvl, g_out, *, blk, levels):
    hi = lf.astype(BF16)
    lo = (lf - hi.astype(F32)).astype(BF16)
    sums = _dot(sel, jnp.concatenate([hi, lo], axis=-1))
    sums = sums[:, :LANES] + sums[:, LANES:]
    b = sums[:blk]
    qf, kf = q.astype(F32), k.astype(F32)
    a = jnp.where(lvl == 0, _dot_nt(q, k), 0.0)
    for li, c in enumerate(levels, start=1):
        if c == 2:
            ref = sums[blk:2 * blk]
        elif c == 1:
            ref = sums[2 * blk:]
        else:
            b3 = b.reshape(blk // (2 * c), 2 * c, LANES)
            ref = jnp.broadcast_to(b3[:, c - 1:c, :], b3.shape).reshape(blk, LANES)
        e = b - ref
        q_c = (qf * jnp.exp(jnp.minimum(e, 0.0))).astype(BF16)
        k_c = (kf * jnp.exp(jnp.minimum(-e, 0.0))).astype(BF16)
        a = jnp.where(lvl == li, _dot_nt(q_c, k_c), a)
    b_last = b[blk - 1:blk, :]
    inter = _dot_nt((qf * jnp.exp(b)).astype(BF16), s_t.astype(BF16))
    o = inter + _dot(a.astype(BF16), v)
    k_t = (kf * jnp.exp(b_last - b)).astype(BF16)
    s_new = s_t * jnp.exp(b_last) + _dot_tn(v, k_t)
    gf = gate.astype(F32)
    y = _rms(o, g_out) * (gf * jax.nn.sigmoid(gf))
    return y, s_new


def _hgrn_kernel(q_ref, k_ref, v_ref, lf_ref, g_ref, s0_ref, sel_ref, lvl_ref, gout_ref,
                 o_ref, s_out_ref, st_scr, *, blk, levels, hb, n_blk):
    tb = pl.program_id(2)

    @pl.when(tb == 0)
    def _():
        for h in range(hb):
            st_scr[h] = s0_ref[0, h].T

    sel, lvl, g_out = sel_ref[...], lvl_ref[...], gout_ref[...]

    def step(c, carry):
        r0 = pl.multiple_of(c * blk, blk)
        rows = pl.ds(r0, blk)
        for h in range(hb):
            cols = slice(h * LANES, (h + 1) * LANES)
            y, s_new = _hgrn_unit(q_ref[rows, cols], k_ref[rows, cols], v_ref[rows, cols], lf_ref[rows, cols],
                                  g_ref[rows, cols], st_scr[h], sel, lvl, g_out, blk=blk, levels=levels)
            o_ref[rows, cols] = y.astype(o_ref.dtype)
            st_scr[h] = s_new
        return carry

    lax.fori_loop(0, n_blk, step, 0)

    @pl.when(tb == pl.num_programs(2) - 1)
    def _():
        for h in range(hb):
            s_out_ref[0, h] = st_scr[h].T


def _hgrn(q, k, v, lf, gate, s0, g_out, *, batch, t_len, blk, tt, hb):
    heads = q.shape[1] // LANES
    tt, hb = min(tt, t_len), min(hb, heads)
    levels, lvl, sel = _hgrn_tables(blk)
    nt = t_len // tt
    xspec = pl.BlockSpec((tt, hb * LANES), lambda b, h, t: (b * nt + t, h))
    sspec = pl.BlockSpec((1, hb, LANES, LANES), lambda b, h, t: (b, h, 0, 0))
    const = lambda a: pl.BlockSpec(a.shape, lambda b, h, t: (0, 0))
    kern = functools.partial(_hgrn_kernel, blk=blk, levels=levels, hb=hb, n_blk=tt // blk)
    g_out = g_out.reshape(1, LANES)
    return pl.pallas_call(
        kern,
        grid=(batch, heads // hb, nt),
        in_specs=[xspec, xspec, xspec, xspec, xspec, sspec, const(sel), const(lvl), const(g_out)],
        out_specs=[xspec, sspec],
        out_shape=[jax.ShapeDtypeStruct(q.shape, BF16), jax.ShapeDtypeStruct(s0.shape, F32)],
        scratch_shapes=[pltpu.VMEM((hb, LANES, LANES), F32)],
        compiler_params=_params("parallel", "parallel", "arbitrary"),
        name="hgrn2",
    )(q, k, v, lf, gate, s0, sel, lvl, g_out)


def _silu(x):
    return x * jax.nn.sigmoid(x)


def _up_conv_kernel(h_ref, wg_ref, wu_ref, cwg_ref, cwu_ref, cbg_ref, cbu_ref,
                    act_ref, tailg_ref, tailu_ref, ug_scr, uu_scr, carry_scr, *, tm, tiles_per_seq):
    i, j = pl.program_id(0), pl.program_id(1)
    seq_start = (i % tiles_per_seq) == 0
    h = h_ref[...]
    conv = []
    for half, (w_ref, cw_ref, cb_ref, tail_ref, scr) in enumerate(
            ((wg_ref, cwg_ref, cbg_ref, tailg_ref, ug_scr), (wu_ref, cwu_ref, cbu_ref, tailu_ref, uu_scr))):
        u = _dot(h, w_ref[...])
        @pl.when(seq_start)
        def _(scr=scr):
            scr[0:8, :] = jnp.zeros((8, scr.shape[1]), F32)

        @pl.when(jnp.logical_not(seq_start))
        def _(scr=scr, half=half):
            scr[0:8, :] = carry_scr[half, j]

        scr[8:, :] = u
        tail = u[tm - 8:, :]
        carry_scr[half, j] = tail
        tail_ref[...] = tail
        cw = cw_ref[...]
        conv.append(cb_ref[...] + cw[2:3] * u + cw[1:2] * scr[7:7 + tm, :] + cw[0:1] * scr[6:6 + tm, :])
    act_ref[...] = (_silu(conv[0]) * conv[1]).astype(act_ref.dtype)


def _up_conv_prompt(h, wg, wu, cwg, cwu, cbg, cbu, *, seq, tm, tn):
    m, d = h.shape
    n = wg.shape[1]
    tm, tn = min(tm, seq), min(tn, n)
    assert seq % tm == 0 and n % tn == 0
    nj = n // tn
    wspec = pl.BlockSpec((d, tn), lambda i, j: (0, j))
    col = lambda r: pl.BlockSpec((r, tn), lambda i, j: (0, j))
    tail = pl.BlockSpec((8, tn), lambda i, j: (i, j))
    kern = functools.partial(_up_conv_kernel, tm=tm, tiles_per_seq=seq // tm)
    return pl.pallas_call(
        kern,
        grid=(m // tm, nj),
        in_specs=[pl.BlockSpec((tm, d), lambda i, j: (i, 0)), wspec, wspec, col(3), col(3), col(1), col(1)],
        out_specs=[pl.BlockSpec((tm, tn), lambda i, j: (i, j)), tail, tail],
        out_shape=[jax.ShapeDtypeStruct((m, n), BF16), jax.ShapeDtypeStruct((m // tm * 8, n), F32),
                   jax.ShapeDtypeStruct((m // tm * 8, n), F32)],
        scratch_shapes=[pltpu.VMEM((tm + 8, tn), F32), pltpu.VMEM((tm + 8, tn), F32),
                        pltpu.VMEM((2, nj, 8, tn), F32)],
        compiler_params=_params("arbitrary", "arbitrary"),
        name="ffn_up_conv",
    )(h, wg, wu, cwg, cwu, cbg, cbu)


def _conv_sample_kernel(ug_ref, uu_ref, hg_ref, hu_ref, cwg_ref, cwu_ref, cbg_ref, cbu_ref, act_ref, *, t_len, taps):
    conv = [[None] * t_len, [None] * t_len]
    for half, (u_ref, hist_ref, cw_ref, cb_ref) in enumerate(
            ((ug_ref, hg_ref, cwg_ref, cbg_ref), (uu_ref, hu_ref, cwu_ref, cbu_ref))):
        cw, cb = cw_ref[...], cb_ref[...]
        rows = [hist_ref[:, r, :] for r in range(taps - 1)] + [u_ref[:, t, :] for t in range(t_len)]
        for t in range(t_len):
            c = cb
            for tap in range(taps):
                c = c + cw[tap:tap + 1] * rows[t + tap]
            conv[half][t] = c
    for t in range(t_len):
        act_ref[:, t, :] = _silu(conv[0][t]) * conv[1][t]


def _conv_sample(u, hist_g, hist_u, cwg, cwu, cbg, cbu, *, tn):
    bsz, t_len, n2 = u.shape
    n = n2 // 2
    tn = min(tn, n)
    nj = n // tn
    taps = cwg.shape[0]
    col = lambda r: pl.BlockSpec((r, tn), lambda j: (0, j))
    hspec = pl.BlockSpec((bsz, taps - 1, tn), lambda j: (0, 0, j))
    kern = functools.partial(_conv_sample_kernel, t_len=t_len, taps=taps)
    return pl.pallas_call(
        kern,
        grid=(nj,),
        in_specs=[pl.BlockSpec((bsz, t_len, tn), lambda j: (0, 0, j)),
                  pl.BlockSpec((bsz, t_len, tn), lambda j: (0, 0, j + nj)),
                  hspec, hspec, col(taps), col(taps), col(1), col(1)],
        out_specs=pl.BlockSpec((bsz, t_len, tn), lambda j: (0, 0, j)),
        out_shape=jax.ShapeDtypeStruct((bsz, t_len, n), F32),
        compiler_params=_params("parallel"),
        name="ffn_conv_sample",
    )(u, u, hist_g, hist_u, cwg, cwu, cbg, cbu)


def _pad_cols(w, n):
    return jnp.pad(w, ((0, 0), (0, n - w.shape[1])))


def _rope_tables(pos, rope):
    half = rope // 2
    inv = 1.0 / (ROPE_THETA ** (jnp.arange(half, dtype=F32) / half))
    ang = pos.astype(F32)[:, None] * inv[None, :]
    cos, sin = jnp.cos(ang), jnp.sin(ang)
    z = jnp.zeros_like(cos)
    zz = jnp.zeros((pos.shape[0], LANES - rope), F32)
    c = jnp.concatenate([cos, cos, zz], axis=-1)
    sa = jnp.concatenate([-sin, z, zz], axis=-1)
    sb = jnp.concatenate([z, sin, zz], axis=-1)
    return c, sa, sb


def _layer_weights(i, lb, g_attn_norm, w_in, g_q_a, w_uq, g_kv_a, w_uk, w_uv, g_q_nope, g_q_rope, g_k_nope,
                   g_k_rope, g_hg_out, w_o, g_ffn_norm, w_up, conv_w, conv_b, w_down, g_ple_norm, w_ple_gate,
                   w_ple):
    q_lora, kv_lora = g_q_a.shape[1], g_kv_a.shape[1]
    nope, rope = g_q_nope.shape[1], g_q_rope.shape[1]
    heads, v_dim = w_uk.shape[2], w_uv.shape[3]
    hg_width = lb.shape[0]
    d_ff = w_down.shape[1]
    assert nope == LANES and v_dim == LANES and g_hg_out.shape[1] == LANES and 2 * rope == LANES
    scale = float((nope + rope) ** -0.5)
    o3 = q_lora + kv_lora + rope
    w_in_i = w_in[i]
    wd = {"dims": dict(q_lora=q_lora, kv_lora=kv_lora, nope=nope, rope=rope, heads=heads, hg_width=hg_width,
                       d_ff=d_ff)}
    wd["g_attn"], wd["g_ffn"], wd["g_ple"] = g_attn_norm[i], g_ffn_norm[i], g_ple_norm[i]
    wd["w_lat"] = _pad_cols(w_in_i[:, :o3], o3 + LANES - rope).astype(BF16)
    wd["g_q_a"], wd["g_kv_a"] = g_q_a[i], g_kv_a[i]
    wd["g_k_rope_pad"] = jnp.pad(g_k_rope[i], (0, LANES - rope)).reshape(1, LANES)
    zh = w_in_i[:, o3:]
    wd["w_hq"], wd["w_hf"], wd["w_hi"], wd["w_hg"] = [zh[:, a * hg_width:(a + 1) * hg_width].astype(BF16)
                                                      for a in range(4)]
    wd["lb"] = lb.reshape(1, hg_width)
    wq = w_uq[i].reshape(q_lora, heads, nope + rope)
    wd["w_qn"] = wq[:, :, :nope].reshape(q_lora, heads * nope).astype(BF16)
    wd["w_qr"] = jnp.pad(wq[:, :, nope:], ((0, 0), (0, 0), (0, LANES - rope))).reshape(q_lora, heads * LANES).astype(BF16)
    wd["g_qn"] = (jnp.tile(g_q_nope[i], heads) * scale).reshape(1, heads * nope)
    wd["g_qr"] = (jnp.tile(jnp.pad(g_q_rope[i], (0, LANES - rope)), heads) * scale).reshape(1, heads * LANES)
    wd["w_uk"] = w_uk[i].reshape(kv_lora, heads * nope).astype(BF16)
    wd["g_kn"] = jnp.tile(g_k_nope[i], heads).reshape(1, heads * nope)
    wd["w_uv"] = w_uv[i].reshape(kv_lora, heads * v_dim).astype(BF16)
    wd["w_absorb"] = jnp.transpose(w_uk[i] * g_k_nope[i][None, None, :], (1, 2, 0)).astype(BF16)
    wd["w_uk_t"] = jnp.transpose(w_uk[i], (1, 2, 0)).reshape(heads * nope, kv_lora).astype(BF16)
    wd["w_uv_h"] = jnp.transpose(w_uv[i], (1, 0, 2)).astype(BF16)
    wd["g_hg_out"] = g_hg_out[i]
    mla_width = heads * v_dim
    wd["w_o_mla"], wd["w_o_hg"] = w_o[i, :mla_width].astype(BF16), w_o[i, mla_width:].astype(BF16)
    ff_pad = -(-d_ff // 512) * 512
    wd["ff_pad"] = ff_pad
    wd["w_gate"] = _pad_cols(w_up[i, :, :d_ff], ff_pad).astype(BF16)
    wd["w_upp"] = _pad_cols(w_up[i, :, d_ff:], ff_pad).astype(BF16)
    wd["cw_g"], wd["cw_u"] = _pad_cols(conv_w[i, :, :d_ff], ff_pad), _pad_cols(conv_w[i, :, d_ff:], ff_pad)
    wd["cb_g"] = _pad_cols(conv_b[i, :d_ff].reshape(1, d_ff), ff_pad)
    wd["cb_u"] = _pad_cols(conv_b[i, d_ff:].reshape(1, d_ff), ff_pad)
    wd["w_down"] = jnp.pad(w_down[i], ((0, ff_pad - d_ff), (0, 0))).astype(BF16)
    wd["w_ple_gate"], wd["w_ple"] = w_ple_gate[i].astype(BF16), w_ple[i].astype(BF16)
    return wd


def _mixer_inputs(x, wd, tabs, *, tm):
    d = wd["dims"]
    h = _rmsnorm(x, wd["g_attn"], 256)
    cq, ckv, ckv_b, kr, kr_b = _latent(h, wd["w_lat"], wd["g_q_a"], wd["g_kv_a"], wd["g_k_rope_pad"], tabs,
                                      q_lora=d["q_lora"], kv_lora=d["kv_lora"], rope=d["rope"], tm=512)
    m = x.shape[0]
    n_tab = tabs[0].shape[0] // min(tm, m)
    tab_spec = pl.BlockSpec((min(tm, m), LANES), lambda i, j: (i % n_tab, 0))

    def qn_body(acc, g):
        return jnp.concatenate(_slab_rms(acc, g, d["nope"]), axis=-1)

    def qr_body(acc, g, c, sa, sb):
        return jnp.concatenate([_rope_slab(y, c, sa, sb) for y in _slab_rms(acc, g, d["rope"])], axis=-1)

    qn = _mm("q_nope", qn_body, [cq], [wd["w_qn"]], tm=tm, tn=512, out_dtypes=[BF16], cols=[wd["g_qn"]])
    qr = _mm("q_rope", qr_body, [cq], [wd["w_qr"]], tm=tm, tn=512, out_dtypes=[BF16], cols=[wd["g_qr"]],
             extras=[(t, tab_spec) for t in tabs])

    def f_body(acc, lb):
        log_f = jnp.log(lb + (1.0 - lb) * jax.nn.sigmoid(acc))
        return log_f, (1.0 - lb) * jax.nn.sigmoid(-acc)

    ident = lambda acc: acc
    hq = _mm("hgrn_q", ident, [h], [wd["w_hq"]], tm=tm, tn=512, out_dtypes=[BF16])
    lf, hk = _mm("hgrn_f", f_body, [h], [wd["w_hf"]], tm=tm, tn=512, out_dtypes=[F32, BF16], cols=[wd["lb"]])
    hv = _mm("hgrn_v", ident, [h], [wd["w_hi"]], tm=tm, tn=512, out_dtypes=[BF16])
    hg = _mm("hgrn_g", ident, [h], [wd["w_hg"]], tm=tm, tn=512, out_dtypes=[BF16])
    return (qn, qr, ckv, ckv_b, kr, kr_b), (hq, hk, hv, lf, hg)


def _out_proj(x, o_mla, o_hg, wd, *, tm):
    return _mm("out_proj", lambda acc, res: res + acc, [o_mla, o_hg], [wd["w_o_mla"], wd["w_o_hg"]],
               tm=tm, tn=512, out_dtypes=[F32], rows=[x])


def _down_ple(x, act, p, wd, *, tm):
    x = _mm("ffn_down", lambda acc, res: res + acc, [act], [wd["w_down"]], tm=min(tm, 512), tn=256,
            out_dtypes=[F32], rows=[x])
    h = _rmsnorm(x, wd["g_ple"], 256)
    return _mm("ple", lambda gate, emb, res: res + emb * jax.nn.sigmoid(gate), [h, p.astype(BF16)],
               [wd["w_ple_gate"], wd["w_ple"]], tm=tm, tn=512, out_dtypes=[F32], group_sizes=(1, 1), rows=[x])


def _prompt_layer(x, p, wd, *, batch, seq):
    d = wd["dims"]
    tm = min(1024, seq)
    assert seq % tm == 0
    tabs = _rope_tables(jnp.arange(seq), d["rope"])
    (qn, qr, ckv, ckv_b, kr, kr_b), (hq, hk, hv, lf, hg) = _mixer_inputs(x, wd, tabs, tm=tm)

    def kn_body(acc, g):
        return jnp.concatenate(_slab_rms(acc, g, d["nope"]), axis=-1)

    kn = _mm("k_nope", kn_body, [ckv_b], [wd["w_uk"]], tm=tm, tn=512, out_dtypes=[BF16], cols=[wd["g_kn"]])
    v = _mm("v_up", lambda acc: acc, [ckv_b], [wd["w_uv"]], tm=tm, tn=512, out_dtypes=[BF16])
    o_mla = _attn_prompt(qn, qr, kn, kr_b, v, batch=batch, seq=seq, heads=d["heads"], tq=512)

    hg_heads = d["hg_width"] // LANES
    s0 = jnp.zeros((batch, hg_heads, LANES, LANES), F32)
    o_hg, s_fin = _hgrn(hq, hk, hv, lf, hg, s0, wd["g_hg_out"], batch=batch, t_len=seq, blk=CHUNK, tt=256, hb=4)

    x = _out_proj(x, o_mla, o_hg, wd, tm=tm)
    h = _rmsnorm(x, wd["g_ffn"], 256)
    tm_up = min(tm, seq)
    act, tail_g, tail_u = _up_conv_prompt(h, wd["w_gate"], wd["w_upp"], wd["cw_g"], wd["cw_u"], wd["cb_g"],
                                          wd["cb_u"], seq=seq, tm=tm_up, tn=512)
    tiles = seq // tm_up
    last = lambda t: t.reshape(batch, tiles, 8, -1)[:, -1, 6:, :d["d_ff"]]
    conv_state = jnp.concatenate([last(tail_g), last(tail_u)], axis=-1)
    x = _down_ple(x, act, p, wd, tm=tm)
    return x, ckv.reshape(batch, seq, -1), kr.reshape(batch, seq, -1), s_fin, conv_state


def _sample_layer(x, p, cache_ckv, cache_kr, state_hgrn, hist, wd, *, batch, t_new):
    d = wd["dims"]
    m = batch * t_new
    past = cache_ckv.shape[1]
    tabs = _rope_tables(jnp.tile(past + jnp.arange(t_new), batch), d["rope"])
    (qn, qr, ckv, ckv_b, kr, kr_b), (hq, hk, hv, lf, hg) = _mixer_inputs(x, wd, tabs, tm=m)

    qt = _per_head_mm("mla_absorb_q", qn, wd["w_absorb"])
    qr_h = jnp.transpose(qr.reshape(m, d["heads"], LANES), (1, 0, 2))
    o_lat = _attn_sample(qt, qr_h, cache_ckv, cache_kr, ckv_b, kr_b, wd["w_uk_t"], batch=batch, t_new=t_new,
                         heads=d["heads"], nope=d["nope"], rope=d["rope"], tk=512)
    o_mla = _value_up(o_lat, wd["w_uv_h"])

    o_hg, s_new = _hgrn(hq, hk, hv, lf, hg, state_hgrn, wd["g_hg_out"], batch=batch, t_len=t_new, blk=t_new,
                        tt=t_new, hb=4)

    x = _out_proj(x, o_mla, o_hg, wd, tm=m)
    h = _rmsnorm(x, wd["g_ffn"], 256)
    ff_pad, d_ff = wd["ff_pad"], d["d_ff"]
    u = _mm("ffn_up_sample", lambda acc: acc, [h], [jnp.concatenate([wd["w_gate"], wd["w_upp"]], axis=1)],
            tm=m, tn=512, out_dtypes=[F32])
    u3 = u.reshape(batch, t_new, 2 * ff_pad)
    pad3 = lambda a: jnp.pad(a, ((0, 0), (0, 0), (0, ff_pad - d_ff)))
    act = _conv_sample(u3, pad3(hist[:, :, :d_ff]), pad3(hist[:, :, d_ff:]), wd["cw_g"], wd["cw_u"], wd["cb_g"],
                       wd["cb_u"], tn=512)
    taps = wd["cw_g"].shape[0]
    u_all = jnp.concatenate([hist, jnp.concatenate([u3[:, :, :d_ff], u3[:, :, ff_pad:ff_pad + d_ff]], axis=-1)],
                            axis=1)
    conv_state = u_all[:, -(taps - 1):]
    x = _down_ple(x, act.reshape(m, ff_pad).astype(BF16), p, wd, tm=m)
    return x, ckv.reshape(batch, t_new, -1), kr.reshape(batch, t_new, -1), s_new, conv_state


def kernel(x_prompt, x_sample, cache_ckv, cache_k_rope, state_hgrn, state_ffn_conv, p_prompt, p_sample,
           g_attn_norm, w_in, g_q_a, w_uq, g_kv_a, w_uk, w_uv, g_q_nope, g_q_rope, g_k_nope, g_k_rope,
           lb_logits, g_hg_out, w_o, g_ffn_norm, w_up, conv_w, conv_b, w_down, g_ple_norm, w_ple_gate, w_ple):
    depth = w_in.shape[0]
    batch, seq, d_model = x_prompt.shape
    dec_batch, t_new, _ = x_sample.shape
    assert t_new >= conv_w.shape[1] - 1
    lb_all = jnp.cumsum(jax.nn.softmax(lb_logits.astype(F32), axis=0), axis=0)
    xp = x_prompt.reshape(batch * seq, d_model)
    xs = x_sample.reshape(dec_batch * t_new, d_model)
    outs = [[] for _ in range(8)]
    for i in range(depth):
        wd = _layer_weights(i, lb_all[i], g_attn_norm, w_in, g_q_a, w_uq, g_kv_a, w_uk, w_uv, g_q_nope, g_q_rope,
                            g_k_nope, g_k_rope, g_hg_out, w_o, g_ffn_norm, w_up, conv_w, conv_b, w_down,
                            g_ple_norm, w_ple_gate, w_ple)
        xp, *new_p = _prompt_layer(xp, p_prompt[i].reshape(batch * seq, -1), wd, batch=batch, seq=seq)
        xs, *new_s = _sample_layer(xs, p_sample[i].reshape(dec_batch * t_new, -1), cache_ckv[i], cache_k_rope[i],
                                   state_hgrn[i], state_ffn_conv[i], wd, batch=dec_batch, t_new=t_new)
        for lst, val in zip(outs, new_p + new_s):
            lst.append(val)
    return (xp.reshape(batch, seq, d_model), xs.reshape(dec_batch, t_new, d_model), *[jnp.stack(o) for o in outs])
```

```python
import functools

import numpy as np
import jax
import jax.numpy as jnp
from jax import lax
from jax.experimental import pallas as pl
from jax.experimental.pallas import tpu as pltpu

F32 = jnp.float32
BF16 = jnp.bfloat16

EPS = 1e-6
CHUNK = 64
ROPE_THETA = 10000.0
LANES = 128
NEG_BIG = -1e30
VMEM_LIMIT_BYTES = 56 * 1024 * 1024


def _params(*sem):
    return pltpu.CompilerParams(dimension_semantics=sem, vmem_limit_bytes=VMEM_LIMIT_BYTES)


def _dot(a, b):
    return jnp.dot(a, b, preferred_element_type=F32)


def _dot_nt(a, b):
    return lax.dot_general(a, b, (((1,), (1,)), ((), ())), preferred_element_type=F32)


def _dot_tn(a, b):
    return lax.dot_general(a, b, (((0,), (0,)), ((), ())), preferred_element_type=F32)


def _rms(x, g):
    ms = jnp.mean(x * x, axis=-1, keepdims=True)
    return x * lax.rsqrt(ms + EPS) * g


def _slab_rms(x, g, valid):
    outs = []
    for a in range(x.shape[1] // LANES):
        blk = x[:, a * LANES:(a + 1) * LANES]
        ms = jnp.sum(blk * blk, axis=-1, keepdims=True) * (1.0 / valid)
        outs.append(blk * lax.rsqrt(ms + EPS) * g[:, a * LANES:(a + 1) * LANES])
    return outs


def _rope_slab(y, c, sa, sb):
    return y * c + pltpu.roll(y, 96, 1) * sa + pltpu.roll(y, 32, 1) * sb


def _rmsnorm_kernel(x_ref, g_ref, o_ref):
    o_ref[...] = _rms(x_ref[...], g_ref[...]).astype(o_ref.dtype)


def _rmsnorm(x, g, tm):
    m, d = x.shape
    tm = min(tm, m)
    assert m % tm == 0
    return pl.pallas_call(
        _rmsnorm_kernel,
        grid=(m // tm,),
        in_specs=[pl.BlockSpec((tm, d), lambda i: (i, 0)), pl.BlockSpec((1, d), lambda i: (0, 0))],
        out_specs=pl.BlockSpec((tm, d), lambda i: (i, 0)),
        out_shape=jax.ShapeDtypeStruct((m, d), BF16),
        compiler_params=_params("parallel"),
        name="rmsnorm",
    )(x, g.reshape(1, d))


def _mm_kernel(*refs, body, n_lhs, group_sizes, n_in):
    ins, outs = refs[:n_in], refs[n_in:]
    lhs, rhs, rest = ins[:n_lhs], ins[n_lhs:2 * n_lhs], ins[2 * n_lhs:]
    accs, k = [], 0
    for gs in group_sizes:
        acc = None
        for _ in range(gs):
            d = _dot(lhs[k][...], rhs[k][...])
            acc = d if acc is None else acc + d
            k += 1
        accs.append(acc)
    res = body(*accs, *[r[...] for r in rest])
    if not isinstance(res, (tuple, list)):
        res = (res,)
    for o, r in zip(outs, res):
        o[...] = r.astype(o.dtype)


def _mm(name, body, lhs, rhs, *, tm, tn, out_dtypes, group_sizes=None, cols=(), rows=(), extras=(),
        single_buffer_lhs=False):
    m, n = lhs[0].shape[0], rhs[0].shape[1]
    tm, tn = min(tm, m), min(tn, n)
    assert m % tm == 0 and n % tn == 0, (name, m, n, tm, tn)
    group_sizes = tuple(group_sizes or (len(lhs),))
    lhs_mode = dict(pipeline_mode=pl.Buffered(1)) if single_buffer_lhs else {}
    in_specs = [pl.BlockSpec((tm, a.shape[1]), lambda i, j: (i, 0), **lhs_mode) for a in lhs]
    in_specs += [pl.BlockSpec((b.shape[0], tn), lambda i, j: (0, j)) for b in rhs]
    in_specs += [pl.BlockSpec((c.shape[0], tn), lambda i, j: (0, j)) for c in cols]
    in_specs += [pl.BlockSpec((tm, tn), lambda i, j: (i, j)) for _ in rows]
    in_specs += [spec for _, spec in extras]
    args = list(lhs) + list(rhs) + list(cols) + list(rows) + [a for a, _ in extras]
    kern = functools.partial(_mm_kernel, body=body, n_lhs=len(lhs), group_sizes=group_sizes, n_in=len(args))
    outs = pl.pallas_call(
        kern,
        grid=(m // tm, n // tn),
        in_specs=in_specs,
        out_specs=[pl.BlockSpec((tm, tn), lambda i, j: (i, j)) for _ in out_dtypes],
        out_shape=[jax.ShapeDtypeStruct((m, n), dt) for dt in out_dtypes],
        compiler_params=_params("parallel", "arbitrary"),
        name=name,
    )(*args)
    return outs if len(outs) > 1 else outs[0]


def _latent_kernel(h_ref, w_ref, gq_ref, gkv_ref, gkr_ref, c_ref, sa_ref, sb_ref,
                   cq_ref, ckv_ref, ckvb_ref, kr_ref, krb_ref, *, q_lora, kv_lora, rope):
    z = _dot(h_ref[...], w_ref[...])
    cq_ref[...] = _rms(z[:, :q_lora], gq_ref[...]).astype(cq_ref.dtype)
    ckv = _rms(z[:, q_lora:q_lora + kv_lora], gkv_ref[...])
    ckv_ref[...] = ckv
    ckvb_ref[...] = ckv.astype(ckvb_ref.dtype)
    (kr,) = _slab_rms(z[:, q_lora + kv_lora:], gkr_ref[...], rope)
    kr = _rope_slab(kr, c_ref[...], sa_ref[...], sb_ref[...])
    kr_ref[...] = kr[:, :rope]
    krb_ref[...] = kr.astype(krb_ref.dtype)


def _latent(h, w_lat, g_q_a, g_kv_a, g_k_rope_pad, tabs, *, q_lora, kv_lora, rope, tm):
    m, d = h.shape
    tm = min(tm, m)
    n = w_lat.shape[1]
    n_tab = tabs[0].shape[0] // tm
    row = lambda w: pl.BlockSpec((tm, w), lambda i: (i, 0))
    const = lambda r, w: pl.BlockSpec((r, w), lambda i: (0, 0))
    tab = pl.BlockSpec((tm, LANES), lambda i: (i % n_tab, 0))
    kern = functools.partial(_latent_kernel, q_lora=q_lora, kv_lora=kv_lora, rope=rope)
    return pl.pallas_call(
        kern,
        grid=(m // tm,),
        in_specs=[row(d), const(d, n), const(1, q_lora), const(1, kv_lora), const(1, LANES), tab, tab, tab],
        out_specs=[row(q_lora), row(kv_lora), row(kv_lora), row(rope), row(LANES)],
        out_shape=[jax.ShapeDtypeStruct((m, q_lora), BF16), jax.ShapeDtypeStruct((m, kv_lora), F32),
                   jax.ShapeDtypeStruct((m, kv_lora), BF16), jax.ShapeDtypeStruct((m, rope), F32),
                   jax.ShapeDtypeStruct((m, LANES), BF16)],
        compiler_params=_params("parallel"),
        name="mla_latent",
    )(h, w_lat, g_q_a.reshape(1, -1), g_kv_a.reshape(1, -1), g_k_rope_pad, *tabs)


def _attn_kernel(qn_ref, qr_ref, kn_ref, kr_ref, vt_ref, o_ref, m_scr, l_scr, acc_scr, *, tq, chunk, hb):
    qi = pl.program_id(2)
    lanes = lambda h: slice(h * LANES, (h + 1) * LANES)
    qs = [jnp.concatenate([qn_ref[:, lanes(h)], qr_ref[:, lanes(h)]], axis=-1) for h in range(hb)]
    m_scr[...] = jnp.full(m_scr.shape, NEG_BIG, F32)
    l_scr[...] = jnp.zeros(l_scr.shape, F32)
    acc_scr[...] = jnp.zeros(acc_scr.shape, F32)

    def block(kb, masked):
        off = pl.multiple_of(kb * tq, tq)
        kr = kr_ref[pl.ds(off, tq), :]
        scores = []
        for h in range(hb):
            k = jnp.concatenate([kn_ref[pl.ds(off, tq), lanes(h)], kr], axis=-1)
            s = _dot_nt(k, qs[h])
            if masked:
                kc = lax.broadcasted_iota(jnp.int32, (tq, tq), 0) // chunk
                qc = lax.broadcasted_iota(jnp.int32, (tq, tq), 1) // chunk
                s = jnp.where(kc <= qc, s, NEG_BIG)
            scores.append(s)
        probs = []
        for h in range(hb):
            m_i = m_scr[h]
            m_new = jnp.maximum(m_i, jnp.max(scores[h], axis=0, keepdims=True))
            alpha = jnp.exp2(m_i - m_new)
            p = jnp.exp2(scores[h] - m_new)
            l_scr[h] = alpha * l_scr[h] + jnp.sum(p, axis=0, keepdims=True)
            m_scr[h] = m_new
            probs.append((alpha, p.astype(BF16)))
        for h in range(hb):
            alpha, p = probs[h]
            acc_scr[h] = alpha * acc_scr[h] + _dot(vt_ref[kb, lanes(h), :], p)

    def body(kb, carry):
        block(kb, False)
        return carry

    lax.fori_loop(0, qi, body, 0)
    block(qi, True)
    for h in range(hb):
        o_ref[:, lanes(h)] = (acc_scr[h] / l_scr[h]).T.astype(o_ref.dtype)


def _attn_prompt(qn, qr, kn, krb, vt, *, batch, seq, heads, tq, hb):
    nq = seq // tq
    hb = min(hb, heads)
    assert heads % hb == 0
    qspec = pl.BlockSpec((tq, hb * LANES), lambda b, h, i: (b * nq + i, h))
    kspec = pl.BlockSpec((seq, hb * LANES), lambda b, h, i: (b, h))
    kern = functools.partial(_attn_kernel, tq=tq, chunk=CHUNK, hb=hb)
    return pl.pallas_call(
        kern,
        grid=(batch, heads // hb, nq),
        in_specs=[qspec, qspec, kspec, pl.BlockSpec((seq, LANES), lambda b, h, i: (b, 0)),
                  pl.BlockSpec((nq, hb * LANES, tq), lambda b, h, i: (b, h, 0))],
        out_specs=qspec,
        out_shape=jax.ShapeDtypeStruct(qn.shape, BF16),
        scratch_shapes=[pltpu.VMEM((hb, 1, tq), F32), pltpu.VMEM((hb, 1, tq), F32),
                        pltpu.VMEM((hb, LANES, tq), F32)],
        compiler_params=_params("parallel", "parallel", "arbitrary"),
        name="mla_attn_prompt",
    )(qn, qr, kn, krb, vt)


def _value_t_kernel(w_ref, x_ref, o_ref):
    o_ref[0] = _dot_nt(w_ref[...], x_ref[...]).astype(o_ref.dtype)


def _value_t(x, w_t, *, tk, tn):
    m, c = x.shape
    n = w_t.shape[0]
    tn = min(tn, n)
    return pl.pallas_call(
        _value_t_kernel,
        grid=(m // tk, n // tn),
        in_specs=[pl.BlockSpec((tn, c), lambda i, j: (j, 0)), pl.BlockSpec((tk, c), lambda i, j: (i, 0))],
        out_specs=pl.BlockSpec((1, tn, tk), lambda i, j: (i, j, 0)),
        out_shape=jax.ShapeDtypeStruct((m // tk, n, tk), BF16),
        compiler_params=_params("parallel", "arbitrary"),
        name="v_up_t",
    )(w_t, x)


def _absorb_kernel(x_ref, w_ref, o_ref):
    o_ref[0] = _dot(x_ref[...], w_ref[0]).astype(o_ref.dtype)


def _per_head_mm(name, x, w):
    m = x.shape[0]
    heads, _, n = w.shape
    return pl.pallas_call(
        _absorb_kernel,
        grid=(heads,),
        in_specs=[pl.BlockSpec((m, LANES), lambda h: (0, h)), pl.BlockSpec((1, LANES, n), lambda h: (h, 0, 0))],
        out_specs=pl.BlockSpec((1, m, n), lambda h: (h, 0, 0)),
        out_shape=jax.ShapeDtypeStruct((heads, m, n), BF16),
        compiler_params=_params("parallel"),
        name=name,
    )(x, w)


def _vup_kernel(x_ref, w_ref, o_ref):
    o_ref[...] = _dot(x_ref[0], w_ref[0]).astype(o_ref.dtype)


def _value_up(o_lat, w):
    heads, m, c = o_lat.shape
    return pl.pallas_call(
        _vup_kernel,
        grid=(heads,),
        in_specs=[pl.BlockSpec((1, m, c), lambda h: (h, 0, 0)), pl.BlockSpec((1, c, LANES), lambda h: (h, 0, 0))],
        out_specs=pl.BlockSpec((m, LANES), lambda h: (0, h)),
        out_shape=jax.ShapeDtypeStruct((m, heads * LANES), BF16),
        compiler_params=_params("parallel"),
        name="mla_value_up",
    )(o_lat, w)


def _attn_sample_kernel(qt_ref, qr_ref, cache_ref, kcache_ref, cnew_ref, knew_ref, wkt_ref, o_ref,
                        s_scr, *, heads, t_new, past, tk, nope, rope):
    hq = heads * t_new
    c_lat = cache_ref.shape[-1]
    qt = qt_ref[...].reshape(hq, c_lat)
    qr = qr_ref[...].reshape(hq, LANES)[:, :rope]
    wkt = wkt_ref[...]

    def scores(c_b, kr_b):
        n = c_b.shape[0]
        kn_t = _dot_nt(wkt, c_b)
        ssq = jnp.sum((kn_t * kn_t).reshape(heads, nope, n), axis=1)
        rinv = lax.rsqrt(ssq * (1.0 / nope) + EPS)
        rinv = jnp.broadcast_to(rinv[:, None, :], (heads, t_new, n)).reshape(hq, n)
        return _dot_nt(qt, c_b) * rinv + _dot_nt(qr, kr_b)

    for kt in range(past // tk):
        c_b = cache_ref[0, kt * tk:(kt + 1) * tk, :].astype(BF16)
        kr_b = kcache_ref[0, kt * tk:(kt + 1) * tk, :].astype(BF16)
        s_scr[:, kt * tk:(kt + 1) * tk] = scores(c_b, kr_b)
    c_new = cnew_ref[...]
    s_new = scores(c_new, knew_ref[...][:, :rope])

    s_old = s_scr[...]
    m_i = jnp.maximum(jnp.max(s_old, axis=-1, keepdims=True), jnp.max(s_new, axis=-1, keepdims=True))
    p_new = jnp.exp2(s_new - m_i)
    l_i = jnp.sum(p_new, axis=-1, keepdims=True)
    acc = _dot(p_new.astype(BF16), c_new)
    for kt in range(past // tk):
        p = jnp.exp2(s_scr[:, kt * tk:(kt + 1) * tk] - m_i)
        l_i = l_i + jnp.sum(p, axis=-1, keepdims=True)
        acc = acc + _dot(p.astype(BF16), cache_ref[0, kt * tk:(kt + 1) * tk, :].astype(BF16))
    o_ref[...] = (acc / l_i).reshape(heads, t_new, c_lat).astype(o_ref.dtype)


def _attn_sample(qt, qr_h, cache, kcache, ckv_new, kr_new, wkt, *, batch, t_new, heads, nope, rope, tk):
    past, c_lat = cache.shape[1], cache.shape[2]
    tk = min(tk, past)
    kern = functools.partial(_attn_sample_kernel, heads=heads, t_new=t_new, past=past, tk=tk, nope=nope, rope=rope)
    return pl.pallas_call(
        kern,
        grid=(batch,),
        in_specs=[pl.BlockSpec((heads, t_new, c_lat), lambda b: (0, b, 0)),
                  pl.BlockSpec((heads, t_new, LANES), lambda b: (0, b, 0)),
                  pl.BlockSpec((1, past, c_lat), lambda b: (b, 0, 0)),
                  pl.BlockSpec((1, past, rope), lambda b: (b, 0, 0)),
                  pl.BlockSpec((t_new, c_lat), lambda b: (b, 0)),
                  pl.BlockSpec((t_new, LANES), lambda b: (b, 0)),
                  pl.BlockSpec(wkt.shape, lambda b: (0, 0))],
        out_specs=pl.BlockSpec((heads, t_new, c_lat), lambda b: (0, b, 0)),
        out_shape=jax.ShapeDtypeStruct((heads, batch * t_new, c_lat), BF16),
        scratch_shapes=[pltpu.VMEM((heads * t_new, past), F32)],
        compiler_params=_params("parallel"),
        name="mla_attn_sample",
    )(qt, qr_h, cache, kcache, ckv_new, kr_new, wkt)


def _hgrn_tables(blk):
    t = np.arange(blk)
    levels = [blk >> (i + 1) for i in range(int(np.log2(blk)))]
    lvl = np.full((blk, blk), -1, np.int32)
    lvl[t, t] = 0
    for li, c in enumerate(levels, start=1):
        same_pair = (t[:, None] // (2 * c)) == (t[None, :] // (2 * c))
        split = (t[:, None] // c) != (t[None, :] // c)
        lvl[same_pair & split & (t[:, None] > t[None, :])] = li
    prefix = (t[None, :] <= t[:, None]).astype(np.float32)
    rows = [prefix]
    for c in (2, 1):
        mid = (t // (2 * c)) * (2 * c) + c - 1
        rows.append((t[None, :] <= mid[:, None]).astype(np.float32))
    return levels, jnp.asarray(lvl), jnp.asarray(np.concatenate(rows, axis=0), dtype=BF16)


def _hgrn_block(q_ref, k_ref, v_ref, lf_ref, g_ref, o_ref, st_scr, rows, sel, pair_masks, upper_masks, signs,
                g_out, *, blk, levels, hb):
    cols = lambda h: slice(h * LANES, (h + 1) * LANES)
    sums = []
    for h in range(hb):
        lf = lf_ref[rows, cols(h)]
        hi = lf.astype(BF16)
        lo = (lf - hi.astype(F32)).astype(BF16)
        both = _dot(sel, jnp.concatenate([hi, lo], axis=-1))
        sums.append(both[:, :LANES] + both[:, LANES:])
    intra = []
    for h in range(hb):
        b = sums[h][:blk]
        q, k = q_ref[rows, cols(h)], k_ref[rows, cols(h)]
        qf, kf = q.astype(F32), k.astype(F32)
        a = jnp.where(pair_masks[0], _dot_nt(q, k), 0.0)
        for li, c in enumerate(levels, start=1):
            if c == 2:
                ref = sums[h][blk:2 * blk]
            elif c == 1:
                ref = sums[h][2 * blk:]
            else:
                b3 = b.reshape(blk // (2 * c), 2 * c, LANES)
                ref = jnp.broadcast_to(b3[:, c - 1:c, :], b3.shape).reshape(blk, LANES)
            decay = jnp.exp2((b - ref) * signs[li - 1])
            w = (jnp.where(upper_masks[li - 1], qf, kf) * decay).astype(BF16)
            a = jnp.where(pair_masks[li], _dot_nt(w, w), a)
        intra.append(a.astype(BF16))
    for h in range(hb):
        b = sums[h][:blk]
        b_last = b[blk - 1:blk, :]
        qf, kf = q_ref[rows, cols(h)].astype(F32), k_ref[rows, cols(h)].astype(F32)
        v = v_ref[rows, cols(h)]
        s_t = st_scr[h]
        inter = _dot_nt((qf * jnp.exp2(b)).astype(BF16), s_t.astype(BF16))
        o = inter + _dot(intra[h], v)
        k_t = (kf * jnp.exp2(b_last - b)).astype(BF16)
        st_scr[h] = s_t * jnp.exp2(b_last) + _dot_tn(v, k_t)
        gf = g_ref[rows, cols(h)].astype(F32)
        o_ref[rows, cols(h)] = (_rms(o, g_out) * (gf * jax.nn.sigmoid(gf))).astype(o_ref.dtype)


def _hgrn_kernel(q_ref, k_ref, v_ref, lf_ref, g_ref, s0_ref, sel_ref, lvl_ref, gout_ref,
                 o_ref, s_out_ref, st_scr, *, blk, levels, hb, n_blk):
    tb = pl.program_id(2)

    @pl.when(tb == 0)
    def _():
        for h in range(hb):
            st_scr[h] = s0_ref[0, h].T

    sel, lvl, g_out = sel_ref[...], lvl_ref[...], gout_ref[...]
    pair_masks = [lvl == li for li in range(len(levels) + 1)]
    row = lax.broadcasted_iota(jnp.int32, (blk, LANES), 0)
    upper_masks = [(row // c) % 2 == 1 for c in levels]
    signs = [jnp.where(u, 1.0, -1.0) for u in upper_masks]

    def step(c, carry):
        rows = pl.ds(pl.multiple_of(c * blk, blk), blk)
        _hgrn_block(q_ref, k_ref, v_ref, lf_ref, g_ref, o_ref, st_scr, rows, sel, pair_masks, upper_masks, signs,
                    g_out, blk=blk, levels=levels, hb=hb)
        return carry

    lax.fori_loop(0, n_blk, step, 0)

    @pl.when(tb == pl.num_programs(2) - 1)
    def _():
        for h in range(hb):
            s_out_ref[0, h] = st_scr[h].T


def _hgrn(q, k, v, lf, gate, s0, g_out, *, batch, t_len, blk, tt, hb):
    heads = q.shape[1] // LANES
    tt, hb = min(tt, t_len), min(hb, heads)
    levels, lvl, sel = _hgrn_tables(blk)
    nt = t_len // tt
    xspec = pl.BlockSpec((tt, hb * LANES), lambda b, h, t: (b * nt + t, h))
    sspec = pl.BlockSpec((1, hb, LANES, LANES), lambda b, h, t: (b, h, 0, 0))
    const = lambda a: pl.BlockSpec(a.shape, lambda b, h, t: (0, 0))
    kern = functools.partial(_hgrn_kernel, blk=blk, levels=levels, hb=hb, n_blk=tt // blk)
    g_out = g_out.reshape(1, LANES)
    return pl.pallas_call(
        kern,
        grid=(batch, heads // hb, nt),
        in_specs=[xspec, xspec, xspec, xspec, xspec, sspec, const(sel), const(lvl), const(g_out)],
        out_specs=[xspec, sspec],
        out_shape=[jax.ShapeDtypeStruct(q.shape, BF16), jax.ShapeDtypeStruct(s0.shape, F32)],
        scratch_shapes=[pltpu.VMEM((hb, LANES, LANES), F32)],
        compiler_params=_params("parallel", "parallel", "arbitrary"),
        name="hgrn2",
    )(q, k, v, lf, gate, s0, sel, lvl, g_out)


def _silu(x):
    return x * jax.nn.sigmoid(x)


def _ff_tile(d_ff):
    assert d_ff % LANES == 0
    return 2 * LANES if d_ff % (2 * LANES) == 0 else LANES


def _up_conv_kernel(h_ref, wg_ref, wu_ref, cwg_ref, cwu_ref, cbg_ref, cbu_ref,
                    act_ref, tailg_ref, tailu_ref, ug_scr, uu_scr, carry_scr, *, tm, cm, tiles_per_seq):
    i, j = pl.program_id(0), pl.program_id(1)
    seq_start = (i % tiles_per_seq) == 0
    halves = ((wg_ref, cwg_ref, cbg_ref, tailg_ref, ug_scr), (wu_ref, cwu_ref, cbu_ref, tailu_ref, uu_scr))
    for half, (_, _, _, _, scr) in enumerate(halves):
        @pl.when(seq_start)
        def _(scr=scr):
            scr[0:8, :] = jnp.zeros((8, scr.shape[1]), F32)

        @pl.when(jnp.logical_not(seq_start))
        def _(scr=scr, half=half):
            scr[0:8, :] = carry_scr[half, j]

    for r in range(tm // cm):
        h = h_ref[r * cm:(r + 1) * cm, :]
        conv = []
        for w_ref, cw_ref, cb_ref, _, scr in halves:
            u = _dot(h, w_ref[...])
            scr[8 + r * cm:8 + (r + 1) * cm, :] = u
            cw = cw_ref[...]
            conv.append(cb_ref[...] + cw[2:3] * u + cw[1:2] * scr[7 + r * cm:7 + (r + 1) * cm, :]
                        + cw[0:1] * scr[6 + r * cm:6 + (r + 1) * cm, :])
        act_ref[r * cm:(r + 1) * cm, :] = (_silu(conv[0]) * conv[1]).astype(act_ref.dtype)

    for half, (_, _, _, tail_ref, scr) in enumerate(halves):
        tail = scr[tm:tm + 8, :]
        carry_scr[half, j] = tail
        tail_ref[...] = tail


def _up_conv_prompt(h, w_up, conv_w, conv_b, *, seq, tm, cm):
    m, d = h.shape
    n = w_up.shape[1] // 2
    tn = _ff_tile(n)
    tm = min(tm, seq)
    cm = min(cm, tm)
    assert seq % tm == 0 and tm % cm == 0 and conv_w.shape[0] == 3
    nj = n // tn
    gate = lambda r: pl.BlockSpec((r, tn), lambda i, j: (0, j))
    up = lambda r: pl.BlockSpec((r, tn), lambda i, j: (0, j + nj))
    tail = pl.BlockSpec((8, tn), lambda i, j: (i, j))
    kern = functools.partial(_up_conv_kernel, tm=tm, cm=cm, tiles_per_seq=seq // tm)
    return pl.pallas_call(
        kern,
        grid=(m // tm, nj),
        in_specs=[pl.BlockSpec((tm, d), lambda i, j: (i, 0)), gate(d), up(d), gate(3), up(3), gate(1), up(1)],
        out_specs=[pl.BlockSpec((tm, tn), lambda i, j: (i, j)), tail, tail],
        out_shape=[jax.ShapeDtypeStruct((m, n), BF16), jax.ShapeDtypeStruct((m // tm * 8, n), F32),
                   jax.ShapeDtypeStruct((m // tm * 8, n), F32)],
        scratch_shapes=[pltpu.VMEM((tm + 8, tn), F32), pltpu.VMEM((tm + 8, tn), F32),
                        pltpu.VMEM((2, nj, 8, tn), F32)],
        compiler_params=_params("arbitrary", "arbitrary"),
        name="ffn_up_conv",
    )(h, w_up, w_up, conv_w, conv_w, conv_b, conv_b)


def _conv_sample_kernel(ug_ref, uu_ref, hg_ref, hu_ref, cwg_ref, cwu_ref, cbg_ref, cbu_ref, act_ref, *, t_len, taps):
    conv = [[None] * t_len, [None] * t_len]
    for half, (u_ref, hist_ref, cw_ref, cb_ref) in enumerate(
            ((ug_ref, hg_ref, cwg_ref, cbg_ref), (uu_ref, hu_ref, cwu_ref, cbu_ref))):
        cw, cb = cw_ref[...], cb_ref[...]
        rows = [hist_ref[:, r, :] for r in range(taps - 1)] + [u_ref[:, t, :] for t in range(t_len)]
        for t in range(t_len):
            c = cb
            for tap in range(taps):
                c = c + cw[tap:tap + 1] * rows[t + tap]
            conv[half][t] = c
    for t in range(t_len):
        act_ref[:, t, :] = _silu(conv[0][t]) * conv[1][t]


def _conv_sample(u, hist, conv_w, conv_b):
    bsz, t_len, n2 = u.shape
    n = n2 // 2
    tn = _ff_tile(n)
    nj = n // tn
    taps = conv_w.shape[0]
    gate = lambda *lead: pl.BlockSpec((*lead, tn), lambda j: (*([0] * len(lead)), j))
    up = lambda *lead: pl.BlockSpec((*lead, tn), lambda j: (*([0] * len(lead)), j + nj))
    kern = functools.partial(_conv_sample_kernel, t_len=t_len, taps=taps)
    return pl.pallas_call(
        kern,
        grid=(nj,),
        in_specs=[gate(bsz, t_len), up(bsz, t_len), gate(bsz, taps - 1), up(bsz, taps - 1),
                  gate(taps), up(taps), gate(1), up(1)],
        out_specs=gate(bsz, t_len),
        out_shape=jax.ShapeDtypeStruct((bsz, t_len, n), F32),
        compiler_params=_params("parallel"),
        name="ffn_conv_sample",
    )(u, u, hist, hist, conv_w, conv_w, conv_b, conv_b)


def _pad_cols(w, n):
    return jnp.pad(w, ((0, 0), (0, n - w.shape[1])))


def _rope_tables(pos, rope):
    half = rope // 2
    inv = 1.0 / (ROPE_THETA ** (jnp.arange(half, dtype=F32) / half))
    ang = pos.astype(F32)[:, None] * inv[None, :]
    cos, sin = jnp.cos(ang), jnp.sin(ang)
    z = jnp.zeros_like(cos)
    zz = jnp.zeros((pos.shape[0], LANES - rope), F32)
    c = jnp.concatenate([cos, cos, zz], axis=-1)
    sa = jnp.concatenate([-sin, z, zz], axis=-1)
    sb = jnp.concatenate([z, sin, zz], axis=-1)
    return c, sa, sb


def _layer_weights(i, lb, g_attn_norm, w_in, g_q_a, w_uq, g_kv_a, w_uk, w_uv, g_q_nope, g_q_rope, g_k_nope,
                   g_k_rope, g_hg_out, w_o, g_ffn_norm, w_up, conv_w, conv_b, w_down, g_ple_norm, w_ple_gate,
                   w_ple):
    q_lora, kv_lora = g_q_a.shape[1], g_kv_a.shape[1]
    nope, rope = g_q_nope.shape[1], g_q_rope.shape[1]
    heads, v_dim = w_uk.shape[2], w_uv.shape[3]
    hg_width = lb.shape[0]
    d_ff = w_down.shape[1]
    assert nope == LANES and v_dim == LANES and g_hg_out.shape[1] == LANES and 2 * rope == LANES
    scale = float((nope + rope) ** -0.5) * float(np.log2(np.e))
    o3 = q_lora + kv_lora + rope
    w_in_i = w_in[i]
    wd = {"dims": dict(q_lora=q_lora, kv_lora=kv_lora, nope=nope, rope=rope, heads=heads, hg_width=hg_width,
                       d_ff=d_ff)}
    wd["g_attn"], wd["g_ffn"], wd["g_ple"] = g_attn_norm[i], g_ffn_norm[i], g_ple_norm[i]
    wd["w_lat"] = _pad_cols(w_in_i[:, :o3], o3 + LANES - rope).astype(BF16)
    wd["g_q_a"], wd["g_kv_a"] = g_q_a[i], g_kv_a[i]
    wd["g_k_rope_pad"] = jnp.pad(g_k_rope[i], (0, LANES - rope)).reshape(1, LANES)
    zh = w_in_i[:, o3:]
    wd["w_hq"], wd["w_hf"], wd["w_hi"], wd["w_hg"] = [zh[:, a * hg_width:(a + 1) * hg_width].astype(BF16)
                                                      for a in range(4)]
    wd["lb"] = lb.reshape(1, hg_width)
    wq = w_uq[i].reshape(q_lora, heads, nope + rope)
    wd["w_qn"] = wq[:, :, :nope].reshape(q_lora, heads * nope).astype(BF16)
    wd["w_qr"] = jnp.pad(wq[:, :, nope:], ((0, 0), (0, 0), (0, LANES - rope))).reshape(q_lora, heads * LANES).astype(BF16)
    wd["g_qn"] = (jnp.tile(g_q_nope[i], heads) * scale).reshape(1, heads * nope)
    wd["g_qr"] = (jnp.tile(jnp.pad(g_q_rope[i], (0, LANES - rope)), heads) * scale).reshape(1, heads * LANES)
    wd["w_uk"] = w_uk[i].reshape(kv_lora, heads * nope).astype(BF16)
    wd["g_kn"] = jnp.tile(g_k_nope[i], heads).reshape(1, heads * nope)
    wd["w_uv_t"] = jnp.transpose(w_uv[i], (1, 2, 0)).reshape(heads * v_dim, kv_lora).astype(BF16)
    wd["w_absorb"] = jnp.transpose(w_uk[i] * g_k_nope[i][None, None, :], (1, 2, 0)).astype(BF16)
    wd["w_uk_t"] = jnp.transpose(w_uk[i], (1, 2, 0)).reshape(heads * nope, kv_lora).astype(BF16)
    wd["w_uv_h"] = jnp.transpose(w_uv[i], (1, 0, 2)).astype(BF16)
    wd["g_hg_out"] = g_hg_out[i]
    mla_width = heads * v_dim
    wd["w_o_mla"], wd["w_o_hg"] = w_o[i, :mla_width].astype(BF16), w_o[i, mla_width:].astype(BF16)
    wd["w_up"], wd["conv_w"], wd["conv_b"] = w_up[i].astype(BF16), conv_w[i], conv_b[i].reshape(1, 2 * d_ff)
    wd["w_down"] = w_down[i].astype(BF16)
    wd["w_ple_gate"], wd["w_ple"] = w_ple_gate[i].astype(BF16), w_ple[i].astype(BF16)
    return wd


def _mixer_inputs(x, wd, tabs, *, tm):
    d = wd["dims"]
    h = _rmsnorm(x, wd["g_attn"], 256)
    cq, ckv, ckv_b, kr, kr_b = _latent(h, wd["w_lat"], wd["g_q_a"], wd["g_kv_a"], wd["g_k_rope_pad"], tabs,
                                      q_lora=d["q_lora"], kv_lora=d["kv_lora"], rope=d["rope"], tm=512)
    m = x.shape[0]
    n_tab = tabs[0].shape[0] // min(tm, m)
    tab_spec = pl.BlockSpec((min(tm, m), LANES), lambda i, j: (i % n_tab, 0))

    def qn_body(acc, g):
        return jnp.concatenate(_slab_rms(acc, g, d["nope"]), axis=-1)

    def qr_body(acc, g, c, sa, sb):
        return jnp.concatenate([_rope_slab(y, c, sa, sb) for y in _slab_rms(acc, g, d["rope"])], axis=-1)

    qn = _mm("q_nope", qn_body, [cq], [wd["w_qn"]], tm=tm, tn=512, out_dtypes=[BF16], cols=[wd["g_qn"]])
    qr = _mm("q_rope", qr_body, [cq], [wd["w_qr"]], tm=tm, tn=512, out_dtypes=[BF16], cols=[wd["g_qr"]],
             extras=[(t, tab_spec) for t in tabs])

    def f_body(acc, lb):
        log2_f = jnp.log2(lb + (1.0 - lb) * jax.nn.sigmoid(acc))
        return log2_f, (1.0 - lb) * jax.nn.sigmoid(-acc)

    ident = lambda acc: acc
    hq = _mm("hgrn_q", ident, [h], [wd["w_hq"]], tm=tm, tn=512, out_dtypes=[BF16])
    lf, hk = _mm("hgrn_f", f_body, [h], [wd["w_hf"]], tm=tm, tn=512, out_dtypes=[F32, BF16], cols=[wd["lb"]])
    hv = _mm("hgrn_v", ident, [h], [wd["w_hi"]], tm=tm, tn=512, out_dtypes=[BF16])
    hg = _mm("hgrn_g", ident, [h], [wd["w_hg"]], tm=tm, tn=512, out_dtypes=[BF16])
    return (qn, qr, ckv, ckv_b, kr, kr_b), (hq, hk, hv, lf, hg)


def _out_proj(x, o_mla, o_hg, wd, *, tm):
    return _mm("out_proj", lambda acc, res: res + acc, [o_mla, o_hg], [wd["w_o_mla"], wd["w_o_hg"]],
               tm=tm, tn=512, out_dtypes=[F32], rows=[x])


def _down_ple(x, act, p, wd, *, tm):
    x = _mm("ffn_down", lambda acc, res: res + acc, [act], [wd["w_down"]], tm=tm, tn=256,
            out_dtypes=[F32], rows=[x], single_buffer_lhs=True)
    h = _rmsnorm(x, wd["g_ple"], 256)
    return _mm("ple", lambda gate, emb, res: res + emb * jax.nn.sigmoid(gate), [h, p.astype(BF16)],
               [wd["w_ple_gate"], wd["w_ple"]], tm=tm, tn=512, out_dtypes=[F32], group_sizes=(1, 1), rows=[x])


def _prompt_layer(x, p, wd, *, batch, seq):
    d = wd["dims"]
    tm = min(1024, seq)
    assert seq % tm == 0
    tabs = _rope_tables(jnp.arange(seq), d["rope"])
    (qn, qr, ckv, ckv_b, kr, kr_b), (hq, hk, hv, lf, hg) = _mixer_inputs(x, wd, tabs, tm=tm)

    def kn_body(acc, g):
        return jnp.concatenate(_slab_rms(acc, g, d["nope"]), axis=-1)

    kn = _mm("k_nope", kn_body, [ckv_b], [wd["w_uk"]], tm=tm, tn=512, out_dtypes=[BF16], cols=[wd["g_kn"]])
    tq = min(512, seq)
    vt = _value_t(ckv_b, wd["w_uv_t"], tk=tq, tn=512)
    o_mla = _attn_prompt(qn, qr, kn, kr_b, vt, batch=batch, seq=seq, heads=d["heads"], tq=tq, hb=4)

    hg_heads = d["hg_width"] // LANES
    s0 = jnp.zeros((batch, hg_heads, LANES, LANES), F32)
    o_hg, s_fin = _hgrn(hq, hk, hv, lf, hg, s0, wd["g_hg_out"], batch=batch, t_len=seq, blk=CHUNK, tt=256, hb=8)

    x = _out_proj(x, o_mla, o_hg, wd, tm=tm)
    h = _rmsnorm(x, wd["g_ffn"], 256)
    tm_up = min(2048, seq)
    act, tail_g, tail_u = _up_conv_prompt(h, wd["w_up"], wd["conv_w"], wd["conv_b"], seq=seq, tm=tm_up, cm=512)
    last = lambda t: t.reshape(batch, seq // tm_up, 8, -1)[:, -1, 6:, :]
    conv_state = jnp.concatenate([last(tail_g), last(tail_u)], axis=-1)
    x = _down_ple(x, act, p, wd, tm=tm)
    return x, ckv.reshape(batch, seq, -1), kr.reshape(batch, seq, -1), s_fin, conv_state


def _sample_layer(x, p, cache_ckv, cache_kr, state_hgrn, hist, wd, *, batch, t_new):
    d = wd["dims"]
    m = batch * t_new
    past = cache_ckv.shape[1]
    tabs = _rope_tables(jnp.tile(past + jnp.arange(t_new), batch), d["rope"])
    (qn, qr, ckv, ckv_b, kr, kr_b), (hq, hk, hv, lf, hg) = _mixer_inputs(x, wd, tabs, tm=m)

    qt = _per_head_mm("mla_absorb_q", qn, wd["w_absorb"])
    qr_h = jnp.transpose(qr.reshape(m, d["heads"], LANES), (1, 0, 2))
    o_lat = _attn_sample(qt, qr_h, cache_ckv, cache_kr, ckv_b, kr_b, wd["w_uk_t"], batch=batch, t_new=t_new,
                         heads=d["heads"], nope=d["nope"], rope=d["rope"], tk=512)
    o_mla = _value_up(o_lat, wd["w_uv_h"])

    o_hg, s_new = _hgrn(hq, hk, hv, lf, hg, state_hgrn, wd["g_hg_out"], batch=batch, t_len=t_new, blk=t_new,
                        tt=t_new, hb=8)

    x = _out_proj(x, o_mla, o_hg, wd, tm=m)
    h = _rmsnorm(x, wd["g_ffn"], 256)
    d_ff = d["d_ff"]
    u = _mm("ffn_up_sample", lambda acc: acc, [h], [wd["w_up"]], tm=m, tn=_ff_tile(d_ff), out_dtypes=[F32])
    u3 = u.reshape(batch, t_new, 2 * d_ff)
    act = _conv_sample(u3, hist, wd["conv_w"], wd["conv_b"])
    taps = wd["conv_w"].shape[0]
    conv_state = jnp.concatenate([hist, u3], axis=1)[:, -(taps - 1):]
    x = _down_ple(x, act.reshape(m, d_ff).astype(BF16), p, wd, tm=m)
    return x, ckv.reshape(batch, t_new, -1), kr.reshape(batch, t_new, -1), s_new, conv_state


def kernel(x_prompt, x_sample, cache_ckv, cache_k_rope, state_hgrn, state_ffn_conv, p_prompt, p_sample,
           g_attn_norm, w_in, g_q_a, w_uq, g_kv_a, w_uk, w_uv, g_q_nope, g_q_rope, g_k_nope, g_k_rope,
           lb_logits, g_hg_out, w_o, g_ffn_norm, w_up, conv_w, conv_b, w_down, g_ple_norm, w_ple_gate, w_ple):
    depth = w_in.shape[0]
    batch, seq, d_model = x_prompt.shape
    dec_batch, t_new, _ = x_sample.shape
    assert t_new >= conv_w.shape[1] - 1
    lb_all = jnp.cumsum(jax.nn.softmax(lb_logits.astype(F32), axis=0), axis=0)
    xp = x_prompt.reshape(batch * seq, d_model)
    xs = x_sample.reshape(dec_batch * t_new, d_model)
    outs = [[] for _ in range(8)]
    for i in range(depth):
        wd = _layer_weights(i, lb_all[i], g_attn_norm, w_in, g_q_a, w_uq, g_kv_a, w_uk, w_uv, g_q_nope, g_q_rope,
                            g_k_nope, g_k_rope, g_hg_out, w_o, g_ffn_norm, w_up, conv_w, conv_b, w_down,
                            g_ple_norm, w_ple_gate, w_ple)
        xp, *new_p = _prompt_layer(xp, p_prompt[i].reshape(batch * seq, -1), wd, batch=batch, seq=seq)
        xs, *new_s = _sample_layer(xs, p_sample[i].reshape(dec_batch * t_new, -1), cache_ckv[i], cache_k_rope[i],
                                   state_hgrn[i], state_ffn_conv[i], wd, batch=dec_batch, t_new=t_new)
        for lst, val in zip(outs, new_p + new_s):
            lst.append(val)
    return (xp.reshape(batch, seq, d_model), xs.reshape(dec_batch, t_new, d_model), *[jnp.stack(o) for o in outs])
```

```python
import functools

import numpy as np
import jax
import jax.numpy as jnp
from jax import lax
from jax.experimental import pallas as pl
from jax.experimental.pallas import tpu as pltpu

F32 = jnp.float32
BF16 = jnp.bfloat16

EPS = 1e-6
CHUNK = 64
ROPE_THETA = 10000.0
LANES = 128
NEG_BIG = -1e30
VMEM_LIMIT_BYTES = 56 * 1024 * 1024


def _params(*sem):
    return pltpu.CompilerParams(dimension_semantics=sem, vmem_limit_bytes=VMEM_LIMIT_BYTES)


def _dot(a, b):
    return jnp.dot(a, b, preferred_element_type=F32)


def _dot_nt(a, b):
    return lax.dot_general(a, b, (((1,), (1,)), ((), ())), preferred_element_type=F32)


def _dot_tn(a, b):
    return lax.dot_general(a, b, (((0,), (0,)), ((), ())), preferred_element_type=F32)


def _rms(x, g):
    ms = jnp.mean(x * x, axis=-1, keepdims=True)
    return x * lax.rsqrt(ms + EPS) * g


def _slab_rms(x, g, valid):
    outs = []
    for a in range(x.shape[1] // LANES):
        blk = x[:, a * LANES:(a + 1) * LANES]
        ms = jnp.sum(blk * blk, axis=-1, keepdims=True) * (1.0 / valid)
        outs.append(blk * lax.rsqrt(ms + EPS) * g[:, a * LANES:(a + 1) * LANES])
    return outs


def _rope_slab(y, c, sa, sb):
    return y * c + pltpu.roll(y, 96, 1) * sa + pltpu.roll(y, 32, 1) * sb


def _rmsnorm_kernel(x_ref, g_ref, o_ref):
    o_ref[...] = _rms(x_ref[...], g_ref[...]).astype(o_ref.dtype)


def _rmsnorm(x, g, tm):
    m, d = x.shape
    tm = min(tm, m)
    assert m % tm == 0
    return pl.pallas_call(
        _rmsnorm_kernel,
        grid=(m // tm,),
        in_specs=[pl.BlockSpec((tm, d), lambda i: (i, 0)), pl.BlockSpec((1, d), lambda i: (0, 0))],
        out_specs=pl.BlockSpec((tm, d), lambda i: (i, 0)),
        out_shape=jax.ShapeDtypeStruct((m, d), BF16),
        compiler_params=_params("parallel"),
        name="rmsnorm",
    )(x, g.reshape(1, d))


def _lane_fold(x):
    out = x[:, :LANES]
    for a in range(1, x.shape[1] // LANES):
        out = out + x[:, a * LANES:(a + 1) * LANES]
    return out


def _row_rinv(ssq, width):
    return lax.rsqrt(jnp.sum(ssq, axis=-1, keepdims=True) * (1.0 / width) + EPS)


def _mm_kernel(*refs, body, n_lhs, group_sizes, n_in, next_norm):
    ins, outs = refs[:n_in], refs[n_in:]
    lhs, rhs, rest = ins[:n_lhs], ins[n_lhs:2 * n_lhs], ins[2 * n_lhs:]
    accs, k = [], 0
    for gs in group_sizes:
        acc = None
        for _ in range(gs):
            d = _dot(lhs[k][...], rhs[k][...])
            acc = d if acc is None else acc + d
            k += 1
        accs.append(acc)
    if next_norm:
        *rest, gain_ref = rest
        *outs, scaled_ref, ssq_ref = outs
    res = body(*accs, *[r[...] for r in rest])
    if not isinstance(res, (tuple, list)):
        res = (res,)
    for o, r in zip(outs, res):
        o[...] = r.astype(o.dtype)
    if next_norm:
        x = res[0]
        scaled_ref[...] = (x * gain_ref[...]).astype(scaled_ref.dtype)
        @pl.when(pl.program_id(1) == 0)
        def _():
            ssq_ref[...] = jnp.zeros(ssq_ref.shape, F32)

        ssq_ref[...] += _lane_fold(x * x)


def _mm(name, body, lhs, rhs, *, tm, tn, out_dtypes, group_sizes=None, cols=(), rows=(), extras=(),
        single_buffer_lhs=False, next_norm_gain=None, n=None, rhs_col0=0):
    m = lhs[0].shape[0]
    n = n or rhs[0].shape[1]
    tm, tn = min(tm, m), min(tn, n)
    assert m % tm == 0 and n % tn == 0 and rhs_col0 % tn == 0, (name, m, n, tm, tn)
    j0 = rhs_col0 // tn
    group_sizes = tuple(group_sizes or (len(lhs),))
    lhs_mode = dict(pipeline_mode=pl.Buffered(1)) if single_buffer_lhs else {}
    tile = pl.BlockSpec((tm, tn), lambda i, j: (i, j))
    in_specs = [pl.BlockSpec((tm, a.shape[1]), lambda i, j: (i, 0), **lhs_mode) for a in lhs]
    in_specs += [pl.BlockSpec((b.shape[0], tn), lambda i, j: (0, j + j0)) for b in rhs]
    in_specs += [pl.BlockSpec((c.shape[0], tn), lambda i, j: (0, j)) for c in cols]
    in_specs += [tile for _ in rows]
    in_specs += [spec for _, spec in extras]
    args = list(lhs) + list(rhs) + list(cols) + list(rows) + [a for a, _ in extras]
    out_specs = [tile for _ in out_dtypes]
    out_shape = [jax.ShapeDtypeStruct((m, n), dt) for dt in out_dtypes]
    if next_norm_gain is not None:
        in_specs.append(pl.BlockSpec((1, tn), lambda i, j: (0, j)))
        args.append(next_norm_gain.reshape(1, n))
        out_specs += [tile, pl.BlockSpec((tm, LANES), lambda i, j: (i, 0))]
        out_shape += [jax.ShapeDtypeStruct((m, n), BF16), jax.ShapeDtypeStruct((m, LANES), F32)]
    kern = functools.partial(_mm_kernel, body=body, n_lhs=len(lhs), group_sizes=group_sizes, n_in=len(args),
                             next_norm=next_norm_gain is not None)
    outs = pl.pallas_call(
        kern,
        grid=(m // tm, n // tn),
        in_specs=in_specs,
        out_specs=out_specs,
        out_shape=out_shape,
        compiler_params=_params("parallel", "arbitrary"),
        name=name,
    )(*args)
    return outs if len(outs) > 1 else outs[0]


def _ssq_spec(tm):
    return pl.BlockSpec((tm, LANES), lambda i, j: (i, 0))


def _latent_kernel(h_ref, w_ref, gq_ref, gkv_ref, gkr_ref, c_ref, sa_ref, sb_ref,
                   cq_ref, ckv_ref, ckvb_ref, kr_ref, krb_ref, *, q_lora, kv_lora, rope):
    z = _dot(h_ref[...], w_ref[...])
    cq_ref[...] = _rms(z[:, :q_lora], gq_ref[...]).astype(cq_ref.dtype)
    ckv = _rms(z[:, q_lora:q_lora + kv_lora], gkv_ref[...])
    ckv_ref[...] = ckv
    ckvb_ref[...] = ckv.astype(ckvb_ref.dtype)
    (kr,) = _slab_rms(z[:, q_lora + kv_lora:], gkr_ref[...], rope)
    kr = _rope_slab(kr, c_ref[...], sa_ref[...], sb_ref[...])
    kr_ref[...] = kr[:, :rope]
    krb_ref[...] = kr.astype(krb_ref.dtype)


def _latent(h, w_lat, g_q_a, g_kv_a, g_k_rope_pad, tabs, *, q_lora, kv_lora, rope, tm):
    m, d = h.shape
    tm = min(tm, m)
    n = w_lat.shape[1]
    n_tab = tabs[0].shape[0] // tm
    row = lambda w: pl.BlockSpec((tm, w), lambda i: (i, 0))
    const = lambda r, w: pl.BlockSpec((r, w), lambda i: (0, 0))
    tab = pl.BlockSpec((tm, LANES), lambda i: (i % n_tab, 0))
    kern = functools.partial(_latent_kernel, q_lora=q_lora, kv_lora=kv_lora, rope=rope)
    return pl.pallas_call(
        kern,
        grid=(m // tm,),
        in_specs=[row(d), const(d, n), const(1, q_lora), const(1, kv_lora), const(1, LANES), tab, tab, tab],
        out_specs=[row(q_lora), row(kv_lora), row(kv_lora), row(rope), row(LANES)],
        out_shape=[jax.ShapeDtypeStruct((m, q_lora), BF16), jax.ShapeDtypeStruct((m, kv_lora), F32),
                   jax.ShapeDtypeStruct((m, kv_lora), BF16), jax.ShapeDtypeStruct((m, rope), F32),
                   jax.ShapeDtypeStruct((m, LANES), BF16)],
        compiler_params=_params("parallel"),
        name="mla_latent",
    )(h, w_lat, g_q_a.reshape(1, -1), g_kv_a.reshape(1, -1), g_k_rope_pad, *tabs)


def _attn_kernel(qn_ref, qr_ref, kn_ref, kr_ref, vt_ref, o_ref, m_scr, l_scr, acc_scr, *, tq, chunk, hb):
    qi = pl.program_id(2)
    lanes = lambda h: slice(h * LANES, (h + 1) * LANES)
    qs = [jnp.concatenate([qn_ref[:, lanes(h)], qr_ref[:, lanes(h)]], axis=-1) for h in range(hb)]
    m_scr[...] = jnp.full(m_scr.shape, NEG_BIG, F32)
    l_scr[...] = jnp.zeros(l_scr.shape, F32)
    acc_scr[...] = jnp.zeros(acc_scr.shape, F32)

    def block(kb, masked):
        off = pl.multiple_of(kb * tq, tq)
        kr = kr_ref[pl.ds(off, tq), :]
        scores = []
        for h in range(hb):
            k = jnp.concatenate([kn_ref[pl.ds(off, tq), lanes(h)], kr], axis=-1)
            s = _dot_nt(k, qs[h])
            if masked:
                kc = lax.broadcasted_iota(jnp.int32, (tq, tq), 0) // chunk
                qc = lax.broadcasted_iota(jnp.int32, (tq, tq), 1) // chunk
                s = jnp.where(kc <= qc, s, NEG_BIG)
            scores.append(s)
        probs = []
        for h in range(hb):
            m_i = m_scr[h]
            m_new = jnp.maximum(m_i, jnp.max(scores[h], axis=0, keepdims=True))
            alpha = jnp.exp2(m_i - m_new)
            p = jnp.exp2(scores[h] - m_new)
            l_scr[h] = alpha * l_scr[h] + jnp.sum(p, axis=0, keepdims=True)
            m_scr[h] = m_new
            probs.append((alpha, p.astype(BF16)))
        for h in range(hb):
            alpha, p = probs[h]
            acc_scr[h] = alpha * acc_scr[h] + _dot(vt_ref[kb, lanes(h), :], p)

    def body(kb, carry):
        block(kb, False)
        return carry

    lax.fori_loop(0, qi, body, 0)
    block(qi, True)
    for h in range(hb):
        o_ref[:, lanes(h)] = (acc_scr[h] / l_scr[h]).T.astype(o_ref.dtype)


def _attn_prompt(qn, qr, kn, krb, vt, *, batch, seq, heads, tq, hb):
    nq = seq // tq
    hb = min(hb, heads)
    assert heads % hb == 0
    qspec = pl.BlockSpec((tq, hb * LANES), lambda b, h, i: (b * nq + i, h))
    kspec = pl.BlockSpec((seq, hb * LANES), lambda b, h, i: (b, h))
    kern = functools.partial(_attn_kernel, tq=tq, chunk=CHUNK, hb=hb)
    return pl.pallas_call(
        kern,
        grid=(batch, heads // hb, nq),
        in_specs=[qspec, qspec, kspec, pl.BlockSpec((seq, LANES), lambda b, h, i: (b, 0)),
                  pl.BlockSpec((nq, hb * LANES, tq), lambda b, h, i: (b, h, 0))],
        out_specs=qspec,
        out_shape=jax.ShapeDtypeStruct(qn.shape, BF16),
        scratch_shapes=[pltpu.VMEM((hb, 1, tq), F32), pltpu.VMEM((hb, 1, tq), F32),
                        pltpu.VMEM((hb, LANES, tq), F32)],
        compiler_params=_params("parallel", "parallel", "arbitrary"),
        name="mla_attn_prompt",
    )(qn, qr, kn, krb, vt)


def _value_t_kernel(w_ref, x_ref, o_ref):
    o_ref[0] = _dot_nt(w_ref[...], x_ref[...]).astype(o_ref.dtype)


def _value_t(x, w_t, *, tk, tn):
    m, c = x.shape
    n = w_t.shape[0]
    tn = min(tn, n)
    return pl.pallas_call(
        _value_t_kernel,
        grid=(m // tk, n // tn),
        in_specs=[pl.BlockSpec((tn, c), lambda i, j: (j, 0)), pl.BlockSpec((tk, c), lambda i, j: (i, 0))],
        out_specs=pl.BlockSpec((1, tn, tk), lambda i, j: (i, j, 0)),
        out_shape=jax.ShapeDtypeStruct((m // tk, n, tk), BF16),
        compiler_params=_params("parallel", "arbitrary"),
        name="v_up_t",
    )(w_t, x)


def _absorb_kernel(x_ref, w_ref, o_ref):
    o_ref[0] = _dot(x_ref[...], w_ref[0]).astype(o_ref.dtype)


def _per_head_mm(name, x, w):
    m = x.shape[0]
    heads, _, n = w.shape
    return pl.pallas_call(
        _absorb_kernel,
        grid=(heads,),
        in_specs=[pl.BlockSpec((m, LANES), lambda h: (0, h)), pl.BlockSpec((1, LANES, n), lambda h: (h, 0, 0))],
        out_specs=pl.BlockSpec((1, m, n), lambda h: (h, 0, 0)),
        out_shape=jax.ShapeDtypeStruct((heads, m, n), BF16),
        compiler_params=_params("parallel"),
        name=name,
    )(x, w)


def _vup_kernel(x_ref, w_ref, o_ref):
    o_ref[...] = _dot(x_ref[0], w_ref[0]).astype(o_ref.dtype)


def _value_up(o_lat, w):
    heads, m, c = o_lat.shape
    return pl.pallas_call(
        _vup_kernel,
        grid=(heads,),
        in_specs=[pl.BlockSpec((1, m, c), lambda h: (h, 0, 0)), pl.BlockSpec((1, c, LANES), lambda h: (h, 0, 0))],
        out_specs=pl.BlockSpec((m, LANES), lambda h: (0, h)),
        out_shape=jax.ShapeDtypeStruct((m, heads * LANES), BF16),
        compiler_params=_params("parallel"),
        name="mla_value_up",
    )(o_lat, w)


def _attn_sample_kernel(qt_ref, qr_ref, cache_ref, kcache_ref, cnew_ref, knew_ref, wkt_ref, o_ref,
                        s_scr, *, heads, t_new, past, tk, nope, rope):
    hq = heads * t_new
    c_lat = cache_ref.shape[-1]
    qt = qt_ref[...].reshape(hq, c_lat)
    qr = qr_ref[...].reshape(hq, LANES)[:, :rope]
    wkt = wkt_ref[...]

    def scores(c_b, kr_b):
        n = c_b.shape[0]
        kn_t = _dot_nt(wkt, c_b)
        ssq = jnp.sum((kn_t * kn_t).reshape(heads, nope, n), axis=1)
        rinv = lax.rsqrt(ssq * (1.0 / nope) + EPS)
        rinv = jnp.broadcast_to(rinv[:, None, :], (heads, t_new, n)).reshape(hq, n)
        return _dot_nt(qt, c_b) * rinv + _dot_nt(qr, kr_b)

    for kt in range(past // tk):
        c_b = cache_ref[0, kt * tk:(kt + 1) * tk, :].astype(BF16)
        kr_b = kcache_ref[0, kt * tk:(kt + 1) * tk, :].astype(BF16)
        s_scr[:, kt * tk:(kt + 1) * tk] = scores(c_b, kr_b)
    c_new = cnew_ref[...]
    s_new = scores(c_new, knew_ref[...][:, :rope])

    s_old = s_scr[...]
    m_i = jnp.maximum(jnp.max(s_old, axis=-1, keepdims=True), jnp.max(s_new, axis=-1, keepdims=True))
    p_new = jnp.exp2(s_new - m_i)
    l_i = jnp.sum(p_new, axis=-1, keepdims=True)
    acc = _dot(p_new.astype(BF16), c_new)
    for kt in range(past // tk):
        p = jnp.exp2(s_scr[:, kt * tk:(kt + 1) * tk] - m_i)
        l_i = l_i + jnp.sum(p, axis=-1, keepdims=True)
        acc = acc + _dot(p.astype(BF16), cache_ref[0, kt * tk:(kt + 1) * tk, :].astype(BF16))
    o_ref[...] = (acc / l_i).reshape(heads, t_new, c_lat).astype(o_ref.dtype)


def _attn_sample(qt, qr_h, cache, kcache, ckv_new, kr_new, wkt, *, batch, t_new, heads, nope, rope, tk):
    past, c_lat = cache.shape[1], cache.shape[2]
    tk = min(tk, past)
    kern = functools.partial(_attn_sample_kernel, heads=heads, t_new=t_new, past=past, tk=tk, nope=nope, rope=rope)
    return pl.pallas_call(
        kern,
        grid=(batch,),
        in_specs=[pl.BlockSpec((heads, t_new, c_lat), lambda b: (0, b, 0)),
                  pl.BlockSpec((heads, t_new, LANES), lambda b: (0, b, 0)),
                  pl.BlockSpec((1, past, c_lat), lambda b: (b, 0, 0)),
                  pl.BlockSpec((1, past, rope), lambda b: (b, 0, 0)),
                  pl.BlockSpec((t_new, c_lat), lambda b: (b, 0)),
                  pl.BlockSpec((t_new, LANES), lambda b: (b, 0)),
                  pl.BlockSpec(wkt.shape, lambda b: (0, 0))],
        out_specs=pl.BlockSpec((heads, t_new, c_lat), lambda b: (0, b, 0)),
        out_shape=jax.ShapeDtypeStruct((heads, batch * t_new, c_lat), BF16),
        scratch_shapes=[pltpu.VMEM((heads * t_new, past), F32)],
        compiler_params=_params("parallel"),
        name="mla_attn_sample",
    )(qt, qr_h, cache, kcache, ckv_new, kr_new, wkt)


def _hgrn_tables(blk):
    t = np.arange(blk)
    levels = [blk >> (i + 1) for i in range(int(np.log2(blk)))]
    lvl = np.full((blk, blk), -1, np.int32)
    lvl[t, t] = 0
    for li, c in enumerate(levels, start=1):
        same_pair = (t[:, None] // (2 * c)) == (t[None, :] // (2 * c))
        split = (t[:, None] // c) != (t[None, :] // c)
        lvl[same_pair & split & (t[:, None] > t[None, :])] = li
    prefix = (t[None, :] <= t[:, None]).astype(np.float32)
    rows = [prefix]
    for c in (2, 1):
        mid = (t // (2 * c)) * (2 * c) + c - 1
        rows.append((t[None, :] <= mid[:, None]).astype(np.float32))
    return levels, jnp.asarray(lvl), jnp.asarray(np.concatenate(rows, axis=0), dtype=BF16)


def _hgrn_block(q_ref, k_ref, v_ref, lf_ref, g_ref, o_ref, st_scr, rows, sel, pair_masks, upper_masks, signs,
                g_out, *, blk, levels, hb):
    cols = lambda h: slice(h * LANES, (h + 1) * LANES)
    sums = []
    for h in range(hb):
        lf = lf_ref[rows, cols(h)]
        hi = lf.astype(BF16)
        lo = (lf - hi.astype(F32)).astype(BF16)
        both = _dot(sel, jnp.concatenate([hi, lo], axis=-1))
        sums.append(both[:, :LANES] + both[:, LANES:])
    intra = []
    for h in range(hb):
        b = sums[h][:blk]
        q, k = q_ref[rows, cols(h)], k_ref[rows, cols(h)]
        qf, kf = q.astype(F32), k.astype(F32)
        a = jnp.where(pair_masks[0], _dot_nt(q, k), 0.0)
        for li, c in enumerate(levels, start=1):
            if c == 2:
                ref = sums[h][blk:2 * blk]
            elif c == 1:
                ref = sums[h][2 * blk:]
            else:
                b3 = b.reshape(blk // (2 * c), 2 * c, LANES)
                ref = jnp.broadcast_to(b3[:, c - 1:c, :], b3.shape).reshape(blk, LANES)
            decay = jnp.exp2((b - ref) * signs[li - 1])
            w = (jnp.where(upper_masks[li - 1], qf, kf) * decay).astype(BF16)
            a = jnp.where(pair_masks[li], _dot_nt(w, w), a)
        intra.append(a.astype(BF16))
    for h in range(hb):
        b = sums[h][:blk]
        b_last = b[blk - 1:blk, :]
        qf, kf = q_ref[rows, cols(h)].astype(F32), k_ref[rows, cols(h)].astype(F32)
        v = v_ref[rows, cols(h)]
        s_t = st_scr[h]
        inter = _dot_nt((qf * jnp.exp2(b)).astype(BF16), s_t.astype(BF16))
        o = inter + _dot(intra[h], v)
        k_t = (kf * jnp.exp2(b_last - b)).astype(BF16)
        st_scr[h] = s_t * jnp.exp2(b_last) + _dot_tn(v, k_t)
        gf = g_ref[rows, cols(h)].astype(F32)
        o_ref[rows, cols(h)] = (_rms(o, g_out) * (gf * jax.nn.sigmoid(gf))).astype(o_ref.dtype)


def _hgrn_kernel(q_ref, k_ref, v_ref, lf_ref, g_ref, s0_ref, sel_ref, lvl_ref, gout_ref,
                 o_ref, s_out_ref, st_scr, *, blk, levels, hb, n_blk):
    tb = pl.program_id(2)

    @pl.when(tb == 0)
    def _():
        for h in range(hb):
            st_scr[h] = s0_ref[0, h].T

    sel, lvl, g_out = sel_ref[...], lvl_ref[...], gout_ref[...]
    pair_masks = [lvl == li for li in range(len(levels) + 1)]
    row = lax.broadcasted_iota(jnp.int32, (blk, LANES), 0)
    upper_masks = [(row // c) % 2 == 1 for c in levels]
    signs = [jnp.where(u, 1.0, -1.0) for u in upper_masks]

    def step(c, carry):
        rows = pl.ds(pl.multiple_of(c * blk, blk), blk)
        _hgrn_block(q_ref, k_ref, v_ref, lf_ref, g_ref, o_ref, st_scr, rows, sel, pair_masks, upper_masks, signs,
                    g_out, blk=blk, levels=levels, hb=hb)
        return carry

    lax.fori_loop(0, n_blk, step, 0)

    @pl.when(tb == pl.num_programs(2) - 1)
    def _():
        for h in range(hb):
            s_out_ref[0, h] = st_scr[h].T


def _hgrn(q, k, v, lf, gate, s0, g_out, *, batch, t_len, blk, tt, hb):
    heads = q.shape[1] // LANES
    tt, hb = min(tt, t_len), min(hb, heads)
    levels, lvl, sel = _hgrn_tables(blk)
    nt = t_len // tt
    xspec = pl.BlockSpec((tt, hb * LANES), lambda b, h, t: (b * nt + t, h))
    sspec = pl.BlockSpec((1, hb, LANES, LANES), lambda b, h, t: (b, h, 0, 0))
    const = lambda a: pl.BlockSpec(a.shape, lambda b, h, t: (0, 0))
    kern = functools.partial(_hgrn_kernel, blk=blk, levels=levels, hb=hb, n_blk=tt // blk)
    g_out = g_out.reshape(1, LANES)
    return pl.pallas_call(
        kern,
        grid=(batch, heads // hb, nt),
        in_specs=[xspec, xspec, xspec, xspec, xspec, sspec, const(sel), const(lvl), const(g_out)],
        out_specs=[xspec, sspec],
        out_shape=[jax.ShapeDtypeStruct(q.shape, BF16), jax.ShapeDtypeStruct(s0.shape, F32)],
        scratch_shapes=[pltpu.VMEM((hb, LANES, LANES), F32)],
        compiler_params=_params("parallel", "parallel", "arbitrary"),
        name="hgrn2",
    )(q, k, v, lf, gate, s0, sel, lvl, g_out)


def _silu(x):
    return x * jax.nn.sigmoid(x)


def _ff_tile(d_ff):
    assert d_ff % LANES == 0
    return 2 * LANES if d_ff % (2 * LANES) == 0 else LANES


def _up_conv_kernel(h_ref, wg_ref, wu_ref, cwg_ref, cwu_ref, cbg_ref, cbu_ref,
                    act_ref, tailg_ref, tailu_ref, ug_scr, uu_scr, carry_scr, *, tm, cm, tiles_per_seq):
    i, j = pl.program_id(0), pl.program_id(1)
    seq_start = (i % tiles_per_seq) == 0
    halves = ((wg_ref, cwg_ref, cbg_ref, tailg_ref, ug_scr), (wu_ref, cwu_ref, cbu_ref, tailu_ref, uu_scr))
    for half, (_, _, _, _, scr) in enumerate(halves):
        @pl.when(seq_start)
        def _(scr=scr):
            scr[0:8, :] = jnp.zeros((8, scr.shape[1]), F32)

        @pl.when(jnp.logical_not(seq_start))
        def _(scr=scr, half=half):
            scr[0:8, :] = carry_scr[half, j]

    for r in range(tm // cm):
        h = h_ref[r * cm:(r + 1) * cm, :]
        conv = []
        for w_ref, cw_ref, cb_ref, _, scr in halves:
            u = _dot(h, w_ref[...])
            scr[8 + r * cm:8 + (r + 1) * cm, :] = u
            cw = cw_ref[...]
            conv.append(cb_ref[...] + cw[2:3] * u + cw[1:2] * scr[7 + r * cm:7 + (r + 1) * cm, :]
                        + cw[0:1] * scr[6 + r * cm:6 + (r + 1) * cm, :])
        act_ref[r * cm:(r + 1) * cm, :] = (_silu(conv[0]) * conv[1]).astype(act_ref.dtype)

    for half, (_, _, _, tail_ref, scr) in enumerate(halves):
        tail = scr[tm:tm + 8, :]
        carry_scr[half, j] = tail
        tail_ref[...] = tail


def _up_conv_prompt(h, w_up, conv_w, conv_b, *, seq, tm, cm):
    m, d = h.shape
    n = w_up.shape[1] // 2
    tn = _ff_tile(n)
    tm = min(tm, seq)
    cm = min(cm, tm)
    assert seq % tm == 0 and tm % cm == 0 and conv_w.shape[0] == 3
    nj = n // tn
    gate = lambda r: pl.BlockSpec((r, tn), lambda i, j: (0, j))
    up = lambda r: pl.BlockSpec((r, tn), lambda i, j: (0, j + nj))
    tail = pl.BlockSpec((8, tn), lambda i, j: (i, j))
    kern = functools.partial(_up_conv_kernel, tm=tm, cm=cm, tiles_per_seq=seq // tm)
    return pl.pallas_call(
        kern,
        grid=(m // tm, nj),
        in_specs=[pl.BlockSpec((tm, d), lambda i, j: (i, 0)), gate(d), up(d), gate(3), up(3), gate(1), up(1)],
        out_specs=[pl.BlockSpec((tm, tn), lambda i, j: (i, j)), tail, tail],
        out_shape=[jax.ShapeDtypeStruct((m, n), BF16), jax.ShapeDtypeStruct((m // tm * 8, n), F32),
                   jax.ShapeDtypeStruct((m // tm * 8, n), F32)],
        scratch_shapes=[pltpu.VMEM((tm + 8, tn), F32), pltpu.VMEM((tm + 8, tn), F32),
                        pltpu.VMEM((2, nj, 8, tn), F32)],
        compiler_params=_params("arbitrary", "arbitrary"),
        name="ffn_up_conv",
    )(h, w_up, w_up, conv_w, conv_w, conv_b, conv_b)


def _conv_sample_kernel(ug_ref, uu_ref, hg_ref, hu_ref, cwg_ref, cwu_ref, cbg_ref, cbu_ref, act_ref, *, t_len, taps):
    conv = [[None] * t_len, [None] * t_len]
    for half, (u_ref, hist_ref, cw_ref, cb_ref) in enumerate(
            ((ug_ref, hg_ref, cwg_ref, cbg_ref), (uu_ref, hu_ref, cwu_ref, cbu_ref))):
        cw, cb = cw_ref[...], cb_ref[...]
        rows = [hist_ref[:, r, :] for r in range(taps - 1)] + [u_ref[:, t, :] for t in range(t_len)]
        for t in range(t_len):
            c = cb
            for tap in range(taps):
                c = c + cw[tap:tap + 1] * rows[t + tap]
            conv[half][t] = c
    for t in range(t_len):
        act_ref[:, t, :] = _silu(conv[0][t]) * conv[1][t]


def _conv_sample(u, hist, conv_w, conv_b):
    bsz, t_len, n2 = u.shape
    n = n2 // 2
    tn = _ff_tile(n)
    nj = n // tn
    taps = conv_w.shape[0]
    gate = lambda *lead: pl.BlockSpec((*lead, tn), lambda j: (*([0] * len(lead)), j))
    up = lambda *lead: pl.BlockSpec((*lead, tn), lambda j: (*([0] * len(lead)), j + nj))
    kern = functools.partial(_conv_sample_kernel, t_len=t_len, taps=taps)
    return pl.pallas_call(
        kern,
        grid=(nj,),
        in_specs=[gate(bsz, t_len), up(bsz, t_len), gate(bsz, taps - 1), up(bsz, taps - 1),
                  gate(taps), up(taps), gate(1), up(1)],
        out_specs=gate(bsz, t_len),
        out_shape=jax.ShapeDtypeStruct((bsz, t_len, n), F32),
        compiler_params=_params("parallel"),
        name="ffn_conv_sample",
    )(u, u, hist, hist, conv_w, conv_w, conv_b, conv_b)


def _pad_cols(w, n):
    return jnp.pad(w, ((0, 0), (0, n - w.shape[1])))


def _rope_tables(pos, rope):
    half = rope // 2
    inv = 1.0 / (ROPE_THETA ** (jnp.arange(half, dtype=F32) / half))
    ang = pos.astype(F32)[:, None] * inv[None, :]
    cos, sin = jnp.cos(ang), jnp.sin(ang)
    z = jnp.zeros_like(cos)
    zz = jnp.zeros((pos.shape[0], LANES - rope), F32)
    c = jnp.concatenate([cos, cos, zz], axis=-1)
    sa = jnp.concatenate([-sin, z, zz], axis=-1)
    sb = jnp.concatenate([z, sin, zz], axis=-1)
    return c, sa, sb


def _layer_weights(i, lb, g_attn_norm, w_in, g_q_a, w_uq, g_kv_a, w_uk, w_uv, g_q_nope, g_q_rope, g_k_nope,
                   g_k_rope, g_hg_out, w_o, g_ffn_norm, w_up, conv_w, conv_b, w_down, g_ple_norm, w_ple_gate,
                   w_ple):
    q_lora, kv_lora = g_q_a.shape[1], g_kv_a.shape[1]
    nope, rope = g_q_nope.shape[1], g_q_rope.shape[1]
    heads, v_dim = w_uk.shape[2], w_uv.shape[3]
    hg_width = lb.shape[0]
    d_ff = w_down.shape[1]
    assert nope == LANES and v_dim == LANES and g_hg_out.shape[1] == LANES and 2 * rope == LANES
    scale = float((nope + rope) ** -0.5) * float(np.log2(np.e))
    o3 = q_lora + kv_lora + rope
    w_in_i = w_in[i]
    wd = {"dims": dict(q_lora=q_lora, kv_lora=kv_lora, nope=nope, rope=rope, heads=heads, hg_width=hg_width,
                       d_ff=d_ff)}
    wd["g_attn"], wd["g_ffn"], wd["g_ple"] = g_attn_norm[i], g_ffn_norm[i], g_ple_norm[i]
    wd["w_lat"] = _pad_cols(w_in_i[:, :o3], o3 + LANES - rope).astype(BF16)
    wd["g_q_a"], wd["g_kv_a"] = g_q_a[i], g_kv_a[i]
    wd["g_k_rope_pad"] = jnp.pad(g_k_rope[i], (0, LANES - rope)).reshape(1, LANES)
    wd["w_hgrn"] = w_in_i[:, o3:].astype(BF16)
    wd["lb"] = lb.reshape(1, hg_width)
    wq = w_uq[i].reshape(q_lora, heads, nope + rope)
    wd["w_qn"] = wq[:, :, :nope].reshape(q_lora, heads * nope).astype(BF16)
    half = rope // 2
    slab = lambda a, b: jnp.pad(jnp.concatenate([a, b], axis=-1), [(0, 0)] * (a.ndim - 1) + [(0, LANES - rope)])
    flat = lambda w: w.reshape(q_lora, heads * LANES).astype(BF16)
    wr1, wr2 = wq[:, :, nope:nope + half], wq[:, :, nope + half:]
    wd["w_qr"], wd["w_qr_swap"] = flat(slab(wr1, wr2)), flat(slab(wr2, wr1))
    wd["g_qn"] = (jnp.tile(g_q_nope[i], heads) * scale).reshape(1, heads * nope)
    gr = g_q_rope[i] * scale
    wd["g_qr"] = jnp.tile(slab(gr[:half], gr[half:]), heads).reshape(1, heads * LANES)
    wd["g_qr_swap"] = jnp.tile(slab(gr[half:], gr[:half]), heads).reshape(1, heads * LANES)
    wd["w_uk"] = w_uk[i].reshape(kv_lora, heads * nope).astype(BF16)
    wd["g_kn"] = jnp.tile(g_k_nope[i], heads).reshape(1, heads * nope)
    wd["w_uv_t"] = jnp.transpose(w_uv[i], (1, 2, 0)).reshape(heads * v_dim, kv_lora).astype(BF16)
    wd["w_absorb"] = jnp.transpose(w_uk[i] * g_k_nope[i][None, None, :], (1, 2, 0)).astype(BF16)
    wd["w_uk_t"] = jnp.transpose(w_uk[i], (1, 2, 0)).reshape(heads * nope, kv_lora).astype(BF16)
    wd["w_uv_h"] = jnp.transpose(w_uv[i], (1, 0, 2)).astype(BF16)
    wd["g_hg_out"] = g_hg_out[i]
    mla_width = heads * v_dim
    wd["w_o_mla"], wd["w_o_hg"] = w_o[i, :mla_width].astype(BF16), w_o[i, mla_width:].astype(BF16)
    wd["w_up"], wd["conv_w"], wd["conv_b"] = w_up[i].astype(BF16), conv_w[i], conv_b[i].reshape(1, 2 * d_ff)
    wd["w_down"] = w_down[i].astype(BF16)
    wd["w_ple_gate"], wd["w_ple"] = w_ple_gate[i].astype(BF16), w_ple[i].astype(BF16)
    return wd


def _mixer_inputs(x, wd, tabs, *, tm):
    d = wd["dims"]
    h = _rmsnorm(x, wd["g_attn"], 256)
    cq, ckv, ckv_b, kr, kr_b = _latent(h, wd["w_lat"], wd["g_q_a"], wd["g_kv_a"], wd["g_k_rope_pad"], tabs,
                                      q_lora=d["q_lora"], kv_lora=d["kv_lora"], rope=d["rope"], tm=512)
    m = x.shape[0]
    n_tab = tabs[0].shape[0] // min(tm, m)
    tab_spec = pl.BlockSpec((min(tm, m), LANES), lambda i, j: (i % n_tab, 0))

    def qn_body(acc, g):
        return jnp.concatenate(_slab_rms(acc, g, d["nope"]), axis=-1)

    def qr_body(acc, acc_swap, g, g_swap, c, s):
        outs = []
        for a in range(acc.shape[1] // LANES):
            sl = slice(a * LANES, (a + 1) * LANES)
            x, xs = acc[:, sl], acc_swap[:, sl]
            rinv = lax.rsqrt(jnp.sum(x * x, axis=-1, keepdims=True) * (1.0 / d["rope"]) + EPS)
            outs.append((x * rinv * g[:, sl]) * c + (xs * rinv * g_swap[:, sl]) * s)
        return jnp.concatenate(outs, axis=-1)

    c_tab, sa_tab, sb_tab = tabs
    qn = _mm("q_nope", qn_body, [cq], [wd["w_qn"]], tm=tm, tn=512, out_dtypes=[BF16], cols=[wd["g_qn"]])
    qr = _mm("q_rope", qr_body, [cq, cq], [wd["w_qr"], wd["w_qr_swap"]], tm=tm, tn=512, out_dtypes=[BF16],
             group_sizes=(1, 1), cols=[wd["g_qr"], wd["g_qr_swap"]],
             extras=[(c_tab, tab_spec), (sa_tab + sb_tab, tab_spec)])

    def f_body(acc, lb):
        s = jax.nn.sigmoid(acc)
        log2_f = jnp.log2(lb + (1.0 - lb) * s)
        return log2_f, (1.0 - lb) * (1.0 - s)

    ident = lambda acc: acc
    hw = d["hg_width"]
    proj = functools.partial(_mm, lhs=[h], rhs=[wd["w_hgrn"]], tm=tm, tn=512, n=hw)
    hq = proj("hgrn_q", ident, out_dtypes=[BF16], rhs_col0=0)
    lf, hk = proj("hgrn_f", f_body, out_dtypes=[F32, BF16], cols=[wd["lb"]], rhs_col0=hw)
    hv = proj("hgrn_v", ident, out_dtypes=[BF16], rhs_col0=2 * hw)
    hg = proj("hgrn_g", ident, out_dtypes=[BF16], rhs_col0=3 * hw)
    return (qn, qr, ckv, ckv_b, kr, kr_b), (hq, hk, hv, lf, hg)


def _out_proj(x, o_mla, o_hg, wd, *, tm):
    return _mm("out_proj", lambda acc, res: res + acc, [o_mla, o_hg], [wd["w_o_mla"], wd["w_o_hg"]],
               tm=tm, tn=512, out_dtypes=[F32], rows=[x])


def _down_ple(x, act, p, wd, *, tm):
    x, xg, ssq = _mm("ffn_down", lambda acc, res: res + acc, [act], [wd["w_down"]], tm=tm, tn=256,
                     out_dtypes=[F32], rows=[x], single_buffer_lhs=True, next_norm_gain=wd["g_ple"])
    width = x.shape[1]

    def ple_body(gate, emb, res, ssq_blk):
        return res + emb * jax.nn.sigmoid(gate * _row_rinv(ssq_blk, width))

    return _mm("ple", ple_body, [xg, p.astype(BF16)], [wd["w_ple_gate"], wd["w_ple"]], tm=tm, tn=512,
               out_dtypes=[F32], group_sizes=(1, 1), rows=[x], extras=[(ssq, _ssq_spec(min(tm, x.shape[0])))])


def _prompt_layer(x, p, wd, *, batch, seq):
    d = wd["dims"]
    tm = min(1024, seq)
    assert seq % tm == 0
    tabs = _rope_tables(jnp.arange(seq), d["rope"])
    (qn, qr, ckv, ckv_b, kr, kr_b), (hq, hk, hv, lf, hg) = _mixer_inputs(x, wd, tabs, tm=tm)

    def kn_body(acc, g):
        return jnp.concatenate(_slab_rms(acc, g, d["nope"]), axis=-1)

    kn = _mm("k_nope", kn_body, [ckv_b], [wd["w_uk"]], tm=tm, tn=512, out_dtypes=[BF16], cols=[wd["g_kn"]])
    tq = min(512, seq)
    vt = _value_t(ckv_b, wd["w_uv_t"], tk=tq, tn=512)
    o_mla = _attn_prompt(qn, qr, kn, kr_b, vt, batch=batch, seq=seq, heads=d["heads"], tq=tq, hb=4)

    hg_heads = d["hg_width"] // LANES
    s0 = jnp.zeros((batch, hg_heads, LANES, LANES), F32)
    o_hg, s_fin = _hgrn(hq, hk, hv, lf, hg, s0, wd["g_hg_out"], batch=batch, t_len=seq, blk=CHUNK, tt=256, hb=8)

    x = _out_proj(x, o_mla, o_hg, wd, tm=tm)
    h = _rmsnorm(x, wd["g_ffn"], 256)
    tm_up = min(2048, seq)
    act, tail_g, tail_u = _up_conv_prompt(h, wd["w_up"], wd["conv_w"], wd["conv_b"], seq=seq, tm=tm_up, cm=512)
    last = lambda t: t.reshape(batch, seq // tm_up, 8, -1)[:, -1, 6:, :]
    conv_state = jnp.concatenate([last(tail_g), last(tail_u)], axis=-1)
    x = _down_ple(x, act, p, wd, tm=tm)
    return x, ckv.reshape(batch, seq, -1), kr.reshape(batch, seq, -1), s_fin, conv_state


def _sample_layer(x, p, cache_ckv, cache_kr, state_hgrn, hist, wd, *, batch, t_new):
    d = wd["dims"]
    m = batch * t_new
    past = cache_ckv.shape[1]
    tabs = _rope_tables(jnp.tile(past + jnp.arange(t_new), batch), d["rope"])
    (qn, qr, ckv, ckv_b, kr, kr_b), (hq, hk, hv, lf, hg) = _mixer_inputs(x, wd, tabs, tm=m)

    qt = _per_head_mm("mla_absorb_q", qn, wd["w_absorb"])
    qr_h = jnp.transpose(qr.reshape(m, d["heads"], LANES), (1, 0, 2))
    o_lat = _attn_sample(qt, qr_h, cache_ckv, cache_kr, ckv_b, kr_b, wd["w_uk_t"], batch=batch, t_new=t_new,
                         heads=d["heads"], nope=d["nope"], rope=d["rope"], tk=512)
    o_mla = _value_up(o_lat, wd["w_uv_h"])

    o_hg, s_new = _hgrn(hq, hk, hv, lf, hg, state_hgrn, wd["g_hg_out"], batch=batch, t_len=t_new, blk=t_new,
                        tt=t_new, hb=8)

    x = _out_proj(x, o_mla, o_hg, wd, tm=m)
    h = _rmsnorm(x, wd["g_ffn"], 256)
    d_ff = d["d_ff"]
    u = _mm("ffn_up_sample", lambda acc: acc, [h], [wd["w_up"]], tm=m, tn=2 * _ff_tile(d_ff), out_dtypes=[F32])
    u3 = u.reshape(batch, t_new, 2 * d_ff)
    act = _conv_sample(u3, hist, wd["conv_w"], wd["conv_b"])
    taps = wd["conv_w"].shape[0]
    conv_state = jnp.concatenate([hist, u3], axis=1)[:, -(taps - 1):]
    x = _down_ple(x, act.reshape(m, d_ff).astype(BF16), p, wd, tm=m)
    return x, ckv.reshape(batch, t_new, -1), kr.reshape(batch, t_new, -1), s_new, conv_state


def kernel(x_prompt, x_sample, cache_ckv, cache_k_rope, state_hgrn, state_ffn_conv, p_prompt, p_sample,
           g_attn_norm, w_in, g_q_a, w_uq, g_kv_a, w_uk, w_uv, g_q_nope, g_q_rope, g_k_nope, g_k_rope,
           lb_logits, g_hg_out, w_o, g_ffn_norm, w_up, conv_w, conv_b, w_down, g_ple_norm, w_ple_gate, w_ple):
    depth = w_in.shape[0]
    batch, seq, d_model = x_prompt.shape
    dec_batch, t_new, _ = x_sample.shape
    assert t_new >= conv_w.shape[1] - 1
    lb_all = jnp.cumsum(jax.nn.softmax(lb_logits.astype(F32), axis=0), axis=0)
    xp = x_prompt.reshape(batch * seq, d_model)
    xs = x_sample.reshape(dec_batch * t_new, d_model)
    outs = [[] for _ in range(8)]
    for i in range(depth):
        wd = _layer_weights(i, lb_all[i], g_attn_norm, w_in, g_q_a, w_uq, g_kv_a, w_uk, w_uv, g_q_nope, g_q_rope,
                            g_k_nope, g_k_rope, g_hg_out, w_o, g_ffn_norm, w_up, conv_w, conv_b, w_down,
                            g_ple_norm, w_ple_gate, w_ple)
        xp, *new_p = _prompt_layer(xp, p_prompt[i].reshape(batch * seq, -1), wd, batch=batch, seq=seq)
        xs, *new_s = _sample_layer(xs, p_sample[i].reshape(dec_batch * t_new, -1), cache_ckv[i], cache_k_rope[i],
                                   state_hgrn[i], state_ffn_conv[i], wd, batch=dec_batch, t_new=t_new)
        for lst, val in zip(outs, new_p + new_s):
            lst.append(val)
    return (xp.reshape(batch, seq, d_model), xs.reshape(dec_batch, t_new, d_model), *[jnp.stack(o) for o in outs])
```

```python
import functools

import numpy as np
import jax
import jax.numpy as jnp
from jax import lax
from jax.experimental import pallas as pl
from jax.experimental.pallas import tpu as pltpu

F32 = jnp.float32
BF16 = jnp.bfloat16

EPS = 1e-6
CHUNK = 64
ROPE_THETA = 10000.0
LANES = 128
NEG_BIG = -1e30
VMEM_LIMIT_BYTES = 56 * 1024 * 1024


def _params(*sem):
    return pltpu.CompilerParams(dimension_semantics=sem, vmem_limit_bytes=VMEM_LIMIT_BYTES)


def _dot(a, b):
    return jnp.dot(a, b, preferred_element_type=F32)


def _dot_nt(a, b):
    return lax.dot_general(a, b, (((1,), (1,)), ((), ())), preferred_element_type=F32)


def _dot_tn(a, b):
    return lax.dot_general(a, b, (((0,), (0,)), ((), ())), preferred_element_type=F32)


def _rms(x, g):
    ms = jnp.mean(x * x, axis=-1, keepdims=True)
    return x * lax.rsqrt(ms + EPS) * g


def _slab_rms(x, g, valid):
    outs = []
    for a in range(x.shape[1] // LANES):
        blk = x[:, a * LANES:(a + 1) * LANES]
        ms = jnp.sum(blk * blk, axis=-1, keepdims=True) * (1.0 / valid)
        outs.append(blk * lax.rsqrt(ms + EPS) * g[:, a * LANES:(a + 1) * LANES])
    return outs


def _rope_slab(y, c, sa, sb):
    return y * c + pltpu.roll(y, 96, 1) * sa + pltpu.roll(y, 32, 1) * sb


def _rmsnorm_kernel(x_ref, g_ref, o_ref):
    o_ref[...] = _rms(x_ref[...], g_ref[...]).astype(o_ref.dtype)


def _rmsnorm(x, g, tm):
    m, d = x.shape
    tm = min(tm, m)
    assert m % tm == 0
    return pl.pallas_call(
        _rmsnorm_kernel,
        grid=(m // tm,),
        in_specs=[pl.BlockSpec((tm, d), lambda i: (i, 0)), pl.BlockSpec((1, d), lambda i: (0, 0))],
        out_specs=pl.BlockSpec((tm, d), lambda i: (i, 0)),
        out_shape=jax.ShapeDtypeStruct((m, d), BF16),
        compiler_params=_params("parallel"),
        name="rmsnorm",
    )(x, g.reshape(1, d))


def _lane_fold(x):
    out = x[:, :LANES]
    for a in range(1, x.shape[1] // LANES):
        out = out + x[:, a * LANES:(a + 1) * LANES]
    return out


def _row_rinv(ssq, width):
    return lax.rsqrt(jnp.sum(ssq, axis=-1, keepdims=True) * (1.0 / width) + EPS)


def _mm_kernel(*refs, body, n_lhs, group_sizes, n_in, next_norm):
    ins, outs = refs[:n_in], refs[n_in:]
    lhs, rhs, rest = ins[:n_lhs], ins[n_lhs:2 * n_lhs], ins[2 * n_lhs:]
    accs, k = [], 0
    for gs in group_sizes:
        acc = None
        for _ in range(gs):
            d = _dot(lhs[k][...], rhs[k][...])
            acc = d if acc is None else acc + d
            k += 1
        accs.append(acc)
    if next_norm:
        *rest, gain_ref = rest
        *outs, scaled_ref, ssq_ref = outs
    res = body(*accs, *[r[...] for r in rest])
    if not isinstance(res, (tuple, list)):
        res = (res,)
    for o, r in zip(outs, res):
        o[...] = r.astype(o.dtype)
    if next_norm:
        x = res[0]
        scaled_ref[...] = (x * gain_ref[...]).astype(scaled_ref.dtype)
        @pl.when(pl.program_id(1) == 0)
        def _():
            ssq_ref[...] = jnp.zeros(ssq_ref.shape, F32)

        ssq_ref[...] += _lane_fold(x * x)


def _mm(name, body, lhs, rhs, *, tm, tn, out_dtypes, group_sizes=None, cols=(), rows=(), extras=(),
        single_buffer_lhs=False, next_norm_gain=None, n=None, rhs_col0=0):
    m = lhs[0].shape[0]
    n = n or rhs[0].shape[1]
    tm, tn = min(tm, m), min(tn, n)
    assert m % tm == 0 and n % tn == 0 and rhs_col0 % tn == 0, (name, m, n, tm, tn)
    j0 = rhs_col0 // tn
    group_sizes = tuple(group_sizes or (len(lhs),))
    lhs_mode = dict(pipeline_mode=pl.Buffered(1)) if single_buffer_lhs else {}
    tile = pl.BlockSpec((tm, tn), lambda i, j: (i, j))
    in_specs = [pl.BlockSpec((tm, a.shape[1]), lambda i, j: (i, 0), **lhs_mode) for a in lhs]
    in_specs += [pl.BlockSpec((b.shape[0], tn), lambda i, j: (0, j + j0)) for b in rhs]
    in_specs += [pl.BlockSpec((c.shape[0], tn), lambda i, j: (0, j)) for c in cols]
    in_specs += [tile for _ in rows]
    in_specs += [spec for _, spec in extras]
    args = list(lhs) + list(rhs) + list(cols) + list(rows) + [a for a, _ in extras]
    out_specs = [tile for _ in out_dtypes]
    out_shape = [jax.ShapeDtypeStruct((m, n), dt) for dt in out_dtypes]
    if next_norm_gain is not None:
        in_specs.append(pl.BlockSpec((1, tn), lambda i, j: (0, j)))
        args.append(next_norm_gain.reshape(1, n))
        out_specs += [tile, pl.BlockSpec((tm, LANES), lambda i, j: (i, 0))]
        out_shape += [jax.ShapeDtypeStruct((m, n), BF16), jax.ShapeDtypeStruct((m, LANES), F32)]
    kern = functools.partial(_mm_kernel, body=body, n_lhs=len(lhs), group_sizes=group_sizes, n_in=len(args),
                             next_norm=next_norm_gain is not None)
    outs = pl.pallas_call(
        kern,
        grid=(m // tm, n // tn),
        in_specs=in_specs,
        out_specs=out_specs,
        out_shape=out_shape,
        compiler_params=_params("parallel", "arbitrary"),
        name=name,
    )(*args)
    return outs if len(outs) > 1 else outs[0]


def _ssq_spec(tm):
    return pl.BlockSpec((tm, LANES), lambda i, j: (i, 0))


def _latent_kernel(h_ref, w_ref, gq_ref, gkv_ref, gkr_ref, c_ref, sa_ref, sb_ref,
                   cq_ref, ckv_ref, ckvb_ref, kr_ref, krb_ref, *, q_lora, kv_lora, rope):
    z = _dot(h_ref[...], w_ref[...])
    cq_ref[...] = _rms(z[:, :q_lora], gq_ref[...]).astype(cq_ref.dtype)
    ckv = _rms(z[:, q_lora:q_lora + kv_lora], gkv_ref[...])
    ckv_ref[...] = ckv
    ckvb_ref[...] = ckv.astype(ckvb_ref.dtype)
    (kr,) = _slab_rms(z[:, q_lora + kv_lora:], gkr_ref[...], rope)
    kr = _rope_slab(kr, c_ref[...], sa_ref[...], sb_ref[...])
    kr_ref[...] = kr[:, :rope]
    krb_ref[...] = kr.astype(krb_ref.dtype)


def _latent(h, w_lat, g_q_a, g_kv_a, g_k_rope_pad, tabs, *, q_lora, kv_lora, rope, tm):
    m, d = h.shape
    tm = min(tm, m)
    n = w_lat.shape[1]
    n_tab = tabs[0].shape[0] // tm
    row = lambda w: pl.BlockSpec((tm, w), lambda i: (i, 0))
    const = lambda r, w: pl.BlockSpec((r, w), lambda i: (0, 0))
    tab = pl.BlockSpec((tm, LANES), lambda i: (i % n_tab, 0))
    kern = functools.partial(_latent_kernel, q_lora=q_lora, kv_lora=kv_lora, rope=rope)
    return pl.pallas_call(
        kern,
        grid=(m // tm,),
        in_specs=[row(d), const(d, n), const(1, q_lora), const(1, kv_lora), const(1, LANES), tab, tab, tab],
        out_specs=[row(q_lora), row(kv_lora), row(kv_lora), row(rope), row(LANES)],
        out_shape=[jax.ShapeDtypeStruct((m, q_lora), BF16), jax.ShapeDtypeStruct((m, kv_lora), F32),
                   jax.ShapeDtypeStruct((m, kv_lora), BF16), jax.ShapeDtypeStruct((m, rope), F32),
                   jax.ShapeDtypeStruct((m, LANES), BF16)],
        compiler_params=_params("parallel"),
        name="mla_latent",
    )(h, w_lat, g_q_a.reshape(1, -1), g_kv_a.reshape(1, -1), g_k_rope_pad, *tabs)


def _attn_kernel(qn_ref, qr_ref, kn_ref, kr_ref, vt_ref, o_ref, m_scr, l_scr, acc_scr, *, tq, chunk, hb):
    qi = pl.program_id(2)
    lanes = lambda h: slice(h * LANES, (h + 1) * LANES)
    qs = [jnp.concatenate([qn_ref[:, lanes(h)], qr_ref[:, lanes(h)]], axis=-1) for h in range(hb)]
    m_scr[...] = jnp.full(m_scr.shape, NEG_BIG, F32)
    l_scr[...] = jnp.zeros(l_scr.shape, F32)
    acc_scr[...] = jnp.zeros(acc_scr.shape, F32)

    def block(kb, masked):
        off = pl.multiple_of(kb * tq, tq)
        kr = kr_ref[pl.ds(off, tq), :]
        scores = []
        for h in range(hb):
            k = jnp.concatenate([kn_ref[pl.ds(off, tq), lanes(h)], kr], axis=-1)
            s = _dot_nt(k, qs[h])
            if masked:
                kc = lax.broadcasted_iota(jnp.int32, (tq, tq), 0) // chunk
                qc = lax.broadcasted_iota(jnp.int32, (tq, tq), 1) // chunk
                s = jnp.where(kc <= qc, s, NEG_BIG)
            scores.append(s)
        probs = []
        for h in range(hb):
            m_i = m_scr[h]
            m_new = jnp.maximum(m_i, jnp.max(scores[h], axis=0, keepdims=True))
            alpha = jnp.exp2(m_i - m_new)
            p = jnp.exp2(scores[h] - m_new)
            l_scr[h] = alpha * l_scr[h] + jnp.sum(p, axis=0, keepdims=True)
            m_scr[h] = m_new
            probs.append((alpha, p.astype(BF16)))
        for h in range(hb):
            alpha, p = probs[h]
            acc_scr[h] = alpha * acc_scr[h] + _dot(vt_ref[kb, lanes(h), :], p)

    def body(kb, carry):
        block(kb, False)
        return carry

    lax.fori_loop(0, qi, body, 0)
    block(qi, True)
    for h in range(hb):
        o_ref[:, lanes(h)] = (acc_scr[h] / l_scr[h]).T.astype(o_ref.dtype)


def _attn_prompt(qn, qr, kn, krb, vt, *, batch, seq, heads, tq, hb):
    nq = seq // tq
    hb = min(hb, heads)
    assert heads % hb == 0
    qspec = pl.BlockSpec((tq, hb * LANES), lambda b, h, i: (b * nq + i, h))
    kspec = pl.BlockSpec((seq, hb * LANES), lambda b, h, i: (b, h))
    kern = functools.partial(_attn_kernel, tq=tq, chunk=CHUNK, hb=hb)
    return pl.pallas_call(
        kern,
        grid=(batch, heads // hb, nq),
        in_specs=[qspec, qspec, kspec, pl.BlockSpec((seq, LANES), lambda b, h, i: (b, 0)),
                  pl.BlockSpec((nq, hb * LANES, tq), lambda b, h, i: (b, h, 0))],
        out_specs=qspec,
        out_shape=jax.ShapeDtypeStruct(qn.shape, BF16),
        scratch_shapes=[pltpu.VMEM((hb, 1, tq), F32), pltpu.VMEM((hb, 1, tq), F32),
                        pltpu.VMEM((hb, LANES, tq), F32)],
        compiler_params=_params("parallel", "parallel", "arbitrary"),
        name="mla_attn_prompt",
    )(qn, qr, kn, krb, vt)


def _value_t_kernel(w_ref, x_ref, o_ref):
    o_ref[0] = _dot_nt(w_ref[...], x_ref[...]).astype(o_ref.dtype)


def _value_t(x, w_t, *, tk, tn):
    m, c = x.shape
    n = w_t.shape[0]
    tn = min(tn, n)
    return pl.pallas_call(
        _value_t_kernel,
        grid=(m // tk, n // tn),
        in_specs=[pl.BlockSpec((tn, c), lambda i, j: (j, 0)), pl.BlockSpec((tk, c), lambda i, j: (i, 0))],
        out_specs=pl.BlockSpec((1, tn, tk), lambda i, j: (i, j, 0)),
        out_shape=jax.ShapeDtypeStruct((m // tk, n, tk), BF16),
        compiler_params=_params("parallel", "arbitrary"),
        name="v_up_t",
    )(w_t, x)


def _absorb_kernel(x_ref, w_ref, o_ref):
    o_ref[0] = _dot(x_ref[...], w_ref[0]).astype(o_ref.dtype)


def _per_head_mm(name, x, w):
    m = x.shape[0]
    heads, _, n = w.shape
    return pl.pallas_call(
        _absorb_kernel,
        grid=(heads,),
        in_specs=[pl.BlockSpec((m, LANES), lambda h: (0, h)), pl.BlockSpec((1, LANES, n), lambda h: (h, 0, 0))],
        out_specs=pl.BlockSpec((1, m, n), lambda h: (h, 0, 0)),
        out_shape=jax.ShapeDtypeStruct((heads, m, n), BF16),
        compiler_params=_params("parallel"),
        name=name,
    )(x, w)


def _vup_kernel(x_ref, w_ref, o_ref):
    o_ref[...] = _dot(x_ref[0], w_ref[0]).astype(o_ref.dtype)


def _value_up(o_lat, w):
    heads, m, c = o_lat.shape
    return pl.pallas_call(
        _vup_kernel,
        grid=(heads,),
        in_specs=[pl.BlockSpec((1, m, c), lambda h: (h, 0, 0)), pl.BlockSpec((1, c, LANES), lambda h: (h, 0, 0))],
        out_specs=pl.BlockSpec((m, LANES), lambda h: (0, h)),
        out_shape=jax.ShapeDtypeStruct((m, heads * LANES), BF16),
        compiler_params=_params("parallel"),
        name="mla_value_up",
    )(o_lat, w)


def _attn_sample_kernel(qt_ref, qr_ref, cache_ref, kcache_ref, cnew_ref, knew_ref, wkt_ref, o_ref,
                        s_scr, *, heads, t_new, past, tk, nope, rope):
    hq = heads * t_new
    c_lat = cache_ref.shape[-1]
    qt = qt_ref[...].reshape(hq, c_lat)
    qr = qr_ref[...].reshape(hq, LANES)[:, :rope]
    wkt = wkt_ref[...]

    def scores(c_b, kr_b):
        n = c_b.shape[0]
        kn_t = _dot_nt(wkt, c_b)
        ssq = jnp.sum((kn_t * kn_t).reshape(heads, nope, n), axis=1)
        rinv = lax.rsqrt(ssq * (1.0 / nope) + EPS)
        rinv = jnp.broadcast_to(rinv[:, None, :], (heads, t_new, n)).reshape(hq, n)
        return _dot_nt(qt, c_b) * rinv + _dot_nt(qr, kr_b)

    for kt in range(past // tk):
        c_b = cache_ref[0, kt * tk:(kt + 1) * tk, :].astype(BF16)
        kr_b = kcache_ref[0, kt * tk:(kt + 1) * tk, :].astype(BF16)
        s_scr[:, kt * tk:(kt + 1) * tk] = scores(c_b, kr_b)
    c_new = cnew_ref[...]
    s_new = scores(c_new, knew_ref[...][:, :rope])

    s_old = s_scr[...]
    m_i = jnp.maximum(jnp.max(s_old, axis=-1, keepdims=True), jnp.max(s_new, axis=-1, keepdims=True))
    p_new = jnp.exp2(s_new - m_i)
    l_i = jnp.sum(p_new, axis=-1, keepdims=True)
    acc = _dot(p_new.astype(BF16), c_new)
    for kt in range(past // tk):
        p = jnp.exp2(s_scr[:, kt * tk:(kt + 1) * tk] - m_i)
        l_i = l_i + jnp.sum(p, axis=-1, keepdims=True)
        acc = acc + _dot(p.astype(BF16), cache_ref[0, kt * tk:(kt + 1) * tk, :].astype(BF16))
    o_ref[...] = (acc / l_i).reshape(heads, t_new, c_lat).astype(o_ref.dtype)


def _attn_sample(qt, qr_h, cache, kcache, ckv_new, kr_new, wkt, *, batch, t_new, heads, nope, rope, tk):
    past, c_lat = cache.shape[1], cache.shape[2]
    tk = min(tk, past)
    kern = functools.partial(_attn_sample_kernel, heads=heads, t_new=t_new, past=past, tk=tk, nope=nope, rope=rope)
    return pl.pallas_call(
        kern,
        grid=(batch,),
        in_specs=[pl.BlockSpec((heads, t_new, c_lat), lambda b: (0, b, 0)),
                  pl.BlockSpec((heads, t_new, LANES), lambda b: (0, b, 0)),
                  pl.BlockSpec((1, past, c_lat), lambda b: (b, 0, 0)),
                  pl.BlockSpec((1, past, rope), lambda b: (b, 0, 0)),
                  pl.BlockSpec((t_new, c_lat), lambda b: (b, 0)),
                  pl.BlockSpec((t_new, LANES), lambda b: (b, 0)),
                  pl.BlockSpec(wkt.shape, lambda b: (0, 0))],
        out_specs=pl.BlockSpec((heads, t_new, c_lat), lambda b: (0, b, 0)),
        out_shape=jax.ShapeDtypeStruct((heads, batch * t_new, c_lat), BF16),
        scratch_shapes=[pltpu.VMEM((heads * t_new, past), F32)],
        compiler_params=_params("parallel"),
        name="mla_attn_sample",
    )(qt, qr_h, cache, kcache, ckv_new, kr_new, wkt)


def _hgrn_tables(blk):
    t = np.arange(blk)
    levels = [blk >> (i + 1) for i in range(int(np.log2(blk)))]
    lvl = np.full((blk, blk), -1, np.int32)
    lvl[t, t] = 0
    for li, c in enumerate(levels, start=1):
        same_pair = (t[:, None] // (2 * c)) == (t[None, :] // (2 * c))
        split = (t[:, None] // c) != (t[None, :] // c)
        lvl[same_pair & split & (t[:, None] > t[None, :])] = li
    prefix = (t[None, :] <= t[:, None]).astype(np.float32)
    rows = [prefix]
    for c in (2, 1):
        mid = (t // (2 * c)) * (2 * c) + c - 1
        rows.append((t[None, :] <= mid[:, None]).astype(np.float32))
    return levels, jnp.asarray(lvl), jnp.asarray(np.concatenate(rows, axis=0), dtype=BF16)


def _hgrn_block(q_ref, k_ref, v_ref, lf_ref, g_ref, o_ref, st_scr, rows, sel, pair_masks, upper_masks, signs,
                g_out, *, blk, levels, hb):
    cols = lambda h: slice(h * LANES, (h + 1) * LANES)
    sums = []
    for h in range(hb):
        lf = lf_ref[rows, cols(h)]
        hi = lf.astype(BF16)
        lo = (lf - hi.astype(F32)).astype(BF16)
        both = _dot(sel, jnp.concatenate([hi, lo], axis=-1))
        sums.append(both[:, :LANES] + both[:, LANES:])
    intra = []
    for h in range(hb):
        b = sums[h][:blk]
        q, k = q_ref[rows, cols(h)], k_ref[rows, cols(h)]
        qf, kf = q.astype(F32), k.astype(F32)
        a = jnp.where(pair_masks[0], _dot_nt(q, k), 0.0)
        for li, c in enumerate(levels, start=1):
            if c == 2:
                ref = sums[h][blk:2 * blk]
            elif c == 1:
                ref = sums[h][2 * blk:]
            else:
                b3 = b.reshape(blk // (2 * c), 2 * c, LANES)
                ref = jnp.broadcast_to(b3[:, c - 1:c, :], b3.shape).reshape(blk, LANES)
            decay = jnp.exp2((b - ref) * signs[li - 1])
            w = (jnp.where(upper_masks[li - 1], qf, kf) * decay).astype(BF16)
            a = jnp.where(pair_masks[li], _dot_nt(w, w), a)
        intra.append(a.astype(BF16))
    for h in range(hb):
        b = sums[h][:blk]
        b_last = b[blk - 1:blk, :]
        qf, kf = q_ref[rows, cols(h)].astype(F32), k_ref[rows, cols(h)].astype(F32)
        v = v_ref[rows, cols(h)]
        s_t = st_scr[h]
        inter = _dot_nt((qf * jnp.exp2(b)).astype(BF16), s_t.astype(BF16))
        o = inter + _dot(intra[h], v)
        k_t = (kf * jnp.exp2(b_last - b)).astype(BF16)
        st_scr[h] = s_t * jnp.exp2(b_last) + _dot_tn(v, k_t)
        gf = g_ref[rows, cols(h)].astype(F32)
        o_ref[rows, cols(h)] = (_rms(o, g_out) * (gf * jax.nn.sigmoid(gf))).astype(o_ref.dtype)


def _hgrn_kernel(q_ref, k_ref, v_ref, lf_ref, g_ref, s0_ref, sel_ref, lvl_ref, gout_ref,
                 o_ref, s_out_ref, st_scr, *, blk, levels, hb, n_blk):
    tb = pl.program_id(2)

    @pl.when(tb == 0)
    def _():
        for h in range(hb):
            st_scr[h] = s0_ref[0, h].T

    sel, lvl, g_out = sel_ref[...], lvl_ref[...], gout_ref[...]
    pair_masks = [lvl == li for li in range(len(levels) + 1)]
    row = lax.broadcasted_iota(jnp.int32, (blk, LANES), 0)
    upper_masks = [(row // c) % 2 == 1 for c in levels]
    signs = [jnp.where(u, 1.0, -1.0) for u in upper_masks]

    def step(c, carry):
        rows = pl.ds(pl.multiple_of(c * blk, blk), blk)
        _hgrn_block(q_ref, k_ref, v_ref, lf_ref, g_ref, o_ref, st_scr, rows, sel, pair_masks, upper_masks, signs,
                    g_out, blk=blk, levels=levels, hb=hb)
        return carry

    lax.fori_loop(0, n_blk, step, 0)

    @pl.when(tb == pl.num_programs(2) - 1)
    def _():
        for h in range(hb):
            s_out_ref[0, h] = st_scr[h].T


def _hgrn(q, k, v, lf, gate, s0, g_out, *, batch, t_len, blk, tt, hb):
    heads = q.shape[1] // LANES
    tt, hb = min(tt, t_len), min(hb, heads)
    levels, lvl, sel = _hgrn_tables(blk)
    nt = t_len // tt
    xspec = pl.BlockSpec((tt, hb * LANES), lambda b, h, t: (b * nt + t, h))
    sspec = pl.BlockSpec((1, hb, LANES, LANES), lambda b, h, t: (b, h, 0, 0))
    const = lambda a: pl.BlockSpec(a.shape, lambda b, h, t: (0, 0))
    kern = functools.partial(_hgrn_kernel, blk=blk, levels=levels, hb=hb, n_blk=tt // blk)
    g_out = g_out.reshape(1, LANES)
    return pl.pallas_call(
        kern,
        grid=(batch, heads // hb, nt),
        in_specs=[xspec, xspec, xspec, xspec, xspec, sspec, const(sel), const(lvl), const(g_out)],
        out_specs=[xspec, sspec],
        out_shape=[jax.ShapeDtypeStruct(q.shape, BF16), jax.ShapeDtypeStruct(s0.shape, F32)],
        scratch_shapes=[pltpu.VMEM((hb, LANES, LANES), F32)],
        compiler_params=_params("parallel", "parallel", "arbitrary"),
        name="hgrn2",
    )(q, k, v, lf, gate, s0, sel, lvl, g_out)


def _silu(x):
    return x * jax.nn.sigmoid(x)


def _ff_tile(d_ff):
    assert d_ff % LANES == 0
    return 2 * LANES if d_ff % (2 * LANES) == 0 else LANES


def _up_conv_kernel(h_ref, wg_ref, wu_ref, cwg_ref, cwu_ref, cbg_ref, cbu_ref,
                    act_ref, tailg_ref, tailu_ref, ug_scr, uu_scr, carry_scr, *, tm, cm, tiles_per_seq):
    i, j = pl.program_id(0), pl.program_id(1)
    seq_start = (i % tiles_per_seq) == 0
    halves = ((wg_ref, cwg_ref, cbg_ref, tailg_ref, ug_scr), (wu_ref, cwu_ref, cbu_ref, tailu_ref, uu_scr))
    for half, (_, _, _, _, scr) in enumerate(halves):
        @pl.when(seq_start)
        def _(scr=scr):
            scr[0:8, :] = jnp.zeros((8, scr.shape[1]), F32)

        @pl.when(jnp.logical_not(seq_start))
        def _(scr=scr, half=half):
            scr[0:8, :] = carry_scr[half, j]

    for r in range(tm // cm):
        h = h_ref[r * cm:(r + 1) * cm, :]
        conv = []
        for w_ref, cw_ref, cb_ref, _, scr in halves:
            u = _dot(h, w_ref[...])
            scr[8 + r * cm:8 + (r + 1) * cm, :] = u
            cw = cw_ref[...]
            conv.append(cb_ref[...] + cw[2:3] * u + cw[1:2] * scr[7 + r * cm:7 + (r + 1) * cm, :]
                        + cw[0:1] * scr[6 + r * cm:6 + (r + 1) * cm, :])
        act_ref[r * cm:(r + 1) * cm, :] = (_silu(conv[0]) * conv[1]).astype(act_ref.dtype)

    for half, (_, _, _, tail_ref, scr) in enumerate(halves):
        tail = scr[tm:tm + 8, :]
        carry_scr[half, j] = tail
        tail_ref[...] = tail


def _up_conv_prompt(h, w_up, conv_w, conv_b, *, seq, tm, cm):
    m, d = h.shape
    n = w_up.shape[1] // 2
    tn = _ff_tile(n)
    tm = min(tm, seq)
    cm = min(cm, tm)
    assert seq % tm == 0 and tm % cm == 0 and conv_w.shape[0] == 3
    nj = n // tn
    gate = lambda r: pl.BlockSpec((r, tn), lambda i, j: (0, j))
    up = lambda r: pl.BlockSpec((r, tn), lambda i, j: (0, j + nj))
    tail = pl.BlockSpec((8, tn), lambda i, j: (i, j))
    kern = functools.partial(_up_conv_kernel, tm=tm, cm=cm, tiles_per_seq=seq // tm)
    return pl.pallas_call(
        kern,
        grid=(m // tm, nj),
        in_specs=[pl.BlockSpec((tm, d), lambda i, j: (i, 0)), gate(d), up(d), gate(3), up(3), gate(1), up(1)],
        out_specs=[pl.BlockSpec((tm, tn), lambda i, j: (i, j)), tail, tail],
        out_shape=[jax.ShapeDtypeStruct((m, n), BF16), jax.ShapeDtypeStruct((m // tm * 8, n), F32),
                   jax.ShapeDtypeStruct((m // tm * 8, n), F32)],
        scratch_shapes=[pltpu.VMEM((tm + 8, tn), F32), pltpu.VMEM((tm + 8, tn), F32),
                        pltpu.VMEM((2, nj, 8, tn), F32)],
        compiler_params=_params("arbitrary", "arbitrary"),
        name="ffn_up_conv",
    )(h, w_up, w_up, conv_w, conv_w, conv_b, conv_b)


def _conv_sample_kernel(ug_ref, uu_ref, hg_ref, hu_ref, cwg_ref, cwu_ref, cbg_ref, cbu_ref, act_ref, *, t_len, taps):
    conv = [[None] * t_len, [None] * t_len]
    for half, (u_ref, hist_ref, cw_ref, cb_ref) in enumerate(
            ((ug_ref, hg_ref, cwg_ref, cbg_ref), (uu_ref, hu_ref, cwu_ref, cbu_ref))):
        cw, cb = cw_ref[...], cb_ref[...]
        rows = [hist_ref[:, r, :] for r in range(taps - 1)] + [u_ref[:, t, :] for t in range(t_len)]
        for t in range(t_len):
            c = cb
            for tap in range(taps):
                c = c + cw[tap:tap + 1] * rows[t + tap]
            conv[half][t] = c
    for t in range(t_len):
        act_ref[:, t, :] = _silu(conv[0][t]) * conv[1][t]


def _conv_sample(u, hist, conv_w, conv_b):
    bsz, t_len, n2 = u.shape
    n = n2 // 2
    tn = _ff_tile(n)
    nj = n // tn
    taps = conv_w.shape[0]
    gate = lambda *lead: pl.BlockSpec((*lead, tn), lambda j: (*([0] * len(lead)), j))
    up = lambda *lead: pl.BlockSpec((*lead, tn), lambda j: (*([0] * len(lead)), j + nj))
    kern = functools.partial(_conv_sample_kernel, t_len=t_len, taps=taps)
    return pl.pallas_call(
        kern,
        grid=(nj,),
        in_specs=[gate(bsz, t_len), up(bsz, t_len), gate(bsz, taps - 1), up(bsz, taps - 1),
                  gate(taps), up(taps), gate(1), up(1)],
        out_specs=gate(bsz, t_len),
        out_shape=jax.ShapeDtypeStruct((bsz, t_len, n), F32),
        compiler_params=_params("parallel"),
        name="ffn_conv_sample",
    )(u, u, hist, hist, conv_w, conv_w, conv_b, conv_b)


def _pad_cols(w, n):
    return jnp.pad(w, ((0, 0), (0, n - w.shape[1])))


def _rope_tables(pos, rope):
    half = rope // 2
    inv = 1.0 / (ROPE_THETA ** (jnp.arange(half, dtype=F32) / half))
    ang = pos.astype(F32)[:, None] * inv[None, :]
    cos, sin = jnp.cos(ang), jnp.sin(ang)
    z = jnp.zeros_like(cos)
    zz = jnp.zeros((pos.shape[0], LANES - rope), F32)
    c = jnp.concatenate([cos, cos, zz], axis=-1)
    sa = jnp.concatenate([-sin, z, zz], axis=-1)
    sb = jnp.concatenate([z, sin, zz], axis=-1)
    return c, sa, sb


def _layer_weights(i, lb, g_attn_norm, w_in, g_q_a, w_uq, g_kv_a, w_uk, w_uv, g_q_nope, g_q_rope, g_k_nope,
                   g_k_rope, g_hg_out, w_o, g_ffn_norm, w_up, conv_w, conv_b, w_down, g_ple_norm, w_ple_gate,
                   w_ple):
    q_lora, kv_lora = g_q_a.shape[1], g_kv_a.shape[1]
    nope, rope = g_q_nope.shape[1], g_q_rope.shape[1]
    heads, v_dim = w_uk.shape[2], w_uv.shape[3]
    hg_width = lb.shape[0]
    d_ff = w_down.shape[1]
    assert nope == LANES and v_dim == LANES and g_hg_out.shape[1] == LANES and 2 * rope == LANES
    scale = float((nope + rope) ** -0.5) * float(np.log2(np.e))
    o3 = q_lora + kv_lora + rope
    w_in_i = w_in[i]
    wd = {"dims": dict(q_lora=q_lora, kv_lora=kv_lora, nope=nope, rope=rope, heads=heads, hg_width=hg_width,
                       d_ff=d_ff)}
    wd["g_attn"], wd["g_ffn"], wd["g_ple"] = g_attn_norm[i], g_ffn_norm[i], g_ple_norm[i]
    wd["w_lat"] = _pad_cols(w_in_i[:, :o3], o3 + LANES - rope).astype(BF16)
    wd["g_q_a"], wd["g_kv_a"] = g_q_a[i], g_kv_a[i]
    wd["g_k_rope_pad"] = jnp.pad(g_k_rope[i], (0, LANES - rope)).reshape(1, LANES)
    wd["w_hgrn"] = w_in_i[:, o3:].astype(BF16)
    wd["lb"] = lb.reshape(1, hg_width)
    wq = w_uq[i].reshape(q_lora, heads, nope + rope)
    wd["w_qn"] = wq[:, :, :nope].reshape(q_lora, heads * nope).astype(BF16)
    half = rope // 2
    slab = lambda a, b: jnp.pad(jnp.concatenate([a, b], axis=-1), [(0, 0)] * (a.ndim - 1) + [(0, LANES - rope)])
    flat = lambda w: w.reshape(q_lora, heads * LANES).astype(BF16)
    wr1, wr2 = wq[:, :, nope:nope + half], wq[:, :, nope + half:]
    wd["w_qr"], wd["w_qr_swap"] = flat(slab(wr1, wr2)), flat(slab(wr2, wr1))
    wd["g_qn"] = (jnp.tile(g_q_nope[i], heads) * scale).reshape(1, heads * nope)
    gr = g_q_rope[i] * scale
    wd["g_qr"] = jnp.tile(slab(gr[:half], gr[half:]), heads).reshape(1, heads * LANES)
    wd["g_qr_swap"] = jnp.tile(slab(gr[half:], gr[:half]), heads).reshape(1, heads * LANES)
    wd["w_uk"] = w_uk[i].reshape(kv_lora, heads * nope).astype(BF16)
    wd["g_kn"] = jnp.tile(g_k_nope[i], heads).reshape(1, heads * nope)
    wd["w_uv_t"] = jnp.transpose(w_uv[i], (1, 2, 0)).reshape(heads * v_dim, kv_lora).astype(BF16)
    wd["w_absorb"] = jnp.transpose(w_uk[i] * g_k_nope[i][None, None, :], (1, 2, 0)).astype(BF16)
    wd["w_uk_t"] = jnp.transpose(w_uk[i], (1, 2, 0)).reshape(heads * nope, kv_lora).astype(BF16)
    wd["w_uv_h"] = jnp.transpose(w_uv[i], (1, 0, 2)).astype(BF16)
    wd["g_hg_out"] = g_hg_out[i]
    mla_width = heads * v_dim
    wd["w_o_mla"], wd["w_o_hg"] = w_o[i, :mla_width].astype(BF16), w_o[i, mla_width:].astype(BF16)
    wd["w_up"], wd["conv_w"], wd["conv_b"] = w_up[i].astype(BF16), conv_w[i], conv_b[i].reshape(1, 2 * d_ff)
    wd["w_down"] = w_down[i].astype(BF16)
    wd["w_ple_gate"], wd["w_ple"] = w_ple_gate[i].astype(BF16), w_ple[i].astype(BF16)
    return wd


def _mixer_inputs(x, wd, tabs, *, tm):
    d = wd["dims"]
    h = _rmsnorm(x, wd["g_attn"], 256)
    cq, ckv, ckv_b, kr, kr_b = _latent(h, wd["w_lat"], wd["g_q_a"], wd["g_kv_a"], wd["g_k_rope_pad"], tabs,
                                      q_lora=d["q_lora"], kv_lora=d["kv_lora"], rope=d["rope"], tm=512)
    m = x.shape[0]
    n_tab = tabs[0].shape[0] // min(tm, m)
    tab_spec = pl.BlockSpec((min(tm, m), LANES), lambda i, j: (i % n_tab, 0))

    def qn_body(acc, g):
        return jnp.concatenate(_slab_rms(acc, g, d["nope"]), axis=-1)

    def qr_body(acc, acc_swap, g, g_swap, c, s):
        outs = []
        for a in range(acc.shape[1] // LANES):
            sl = slice(a * LANES, (a + 1) * LANES)
            x, xs = acc[:, sl], acc_swap[:, sl]
            rinv = lax.rsqrt(jnp.sum(x * x, axis=-1, keepdims=True) * (1.0 / d["rope"]) + EPS)
            outs.append((x * rinv * g[:, sl]) * c + (xs * rinv * g_swap[:, sl]) * s)
        return jnp.concatenate(outs, axis=-1)

    c_tab, sa_tab, sb_tab = tabs
    qn = _mm("q_nope", qn_body, [cq], [wd["w_qn"]], tm=tm, tn=2048, out_dtypes=[BF16], cols=[wd["g_qn"]])
    qr = _mm("q_rope", qr_body, [cq, cq], [wd["w_qr"], wd["w_qr_swap"]], tm=tm, tn=1024, out_dtypes=[BF16],
             group_sizes=(1, 1), cols=[wd["g_qr"], wd["g_qr_swap"]],
             extras=[(c_tab, tab_spec), (sa_tab + sb_tab, tab_spec)])

    def f_body(acc, lb):
        s = jax.nn.sigmoid(acc)
        log2_f = jnp.log2(lb + (1.0 - lb) * s)
        return log2_f, (1.0 - lb) * (1.0 - s)

    ident = lambda acc: acc
    hw = d["hg_width"]
    proj = functools.partial(_mm, lhs=[h], rhs=[wd["w_hgrn"]], tm=tm, tn=512, n=hw)
    hq = proj("hgrn_q", ident, out_dtypes=[BF16], rhs_col0=0)
    lf, hk = proj("hgrn_f", f_body, out_dtypes=[F32, BF16], cols=[wd["lb"]], rhs_col0=hw)
    hv = proj("hgrn_v", ident, out_dtypes=[BF16], rhs_col0=2 * hw)
    hg = proj("hgrn_g", ident, out_dtypes=[BF16], rhs_col0=3 * hw)
    return (qn, qr, ckv, ckv_b, kr, kr_b), (hq, hk, hv, lf, hg)


def _out_proj(x, o_mla, o_hg, wd, *, tm):
    return _mm("out_proj", lambda acc, res: res + acc, [o_mla, o_hg], [wd["w_o_mla"], wd["w_o_hg"]],
               tm=tm, tn=512, out_dtypes=[F32], rows=[x])


def _down_ple(x, act, p, wd, *, tm):
    x, xg, ssq = _mm("ffn_down", lambda acc, res: res + acc, [act], [wd["w_down"]], tm=tm, tn=256,
                     out_dtypes=[F32], rows=[x], single_buffer_lhs=True, next_norm_gain=wd["g_ple"])
    width = x.shape[1]

    def ple_body(gate, emb, res, ssq_blk):
        return res + emb * jax.nn.sigmoid(gate * _row_rinv(ssq_blk, width))

    return _mm("ple", ple_body, [xg, p.astype(BF16)], [wd["w_ple_gate"], wd["w_ple"]], tm=tm, tn=512,
               out_dtypes=[F32], group_sizes=(1, 1), rows=[x], extras=[(ssq, _ssq_spec(min(tm, x.shape[0])))])


def _prompt_layer(x, p, wd, *, batch, seq):
    d = wd["dims"]
    tm = min(1024, seq)
    assert seq % tm == 0
    tabs = _rope_tables(jnp.arange(seq), d["rope"])
    (qn, qr, ckv, ckv_b, kr, kr_b), (hq, hk, hv, lf, hg) = _mixer_inputs(x, wd, tabs, tm=tm)

    def kn_body(acc, g):
        return jnp.concatenate(_slab_rms(acc, g, d["nope"]), axis=-1)

    kn = _mm("k_nope", kn_body, [ckv_b], [wd["w_uk"]], tm=tm, tn=2048, out_dtypes=[BF16], cols=[wd["g_kn"]])
    tq = min(512, seq)
    vt = _value_t(ckv_b, wd["w_uv_t"], tk=tq, tn=2048)
    o_mla = _attn_prompt(qn, qr, kn, kr_b, vt, batch=batch, seq=seq, heads=d["heads"], tq=tq, hb=4)

    hg_heads = d["hg_width"] // LANES
    s0 = jnp.zeros((batch, hg_heads, LANES, LANES), F32)
    o_hg, s_fin = _hgrn(hq, hk, hv, lf, hg, s0, wd["g_hg_out"], batch=batch, t_len=seq, blk=CHUNK, tt=256, hb=8)

    x = _out_proj(x, o_mla, o_hg, wd, tm=tm)
    h = _rmsnorm(x, wd["g_ffn"], 256)
    tm_up = min(2048, seq)
    act, tail_g, tail_u = _up_conv_prompt(h, wd["w_up"], wd["conv_w"], wd["conv_b"], seq=seq, tm=tm_up, cm=512)
    last = lambda t: t.reshape(batch, seq // tm_up, 8, -1)[:, -1, 6:, :]
    conv_state = jnp.concatenate([last(tail_g), last(tail_u)], axis=-1)
    x = _down_ple(x, act, p, wd, tm=tm)
    return x, ckv.reshape(batch, seq, -1), kr.reshape(batch, seq, -1), s_fin, conv_state


def _sample_layer(x, p, cache_ckv, cache_kr, state_hgrn, hist, wd, *, batch, t_new):
    d = wd["dims"]
    m = batch * t_new
    past = cache_ckv.shape[1]
    tabs = _rope_tables(jnp.tile(past + jnp.arange(t_new), batch), d["rope"])
    (qn, qr, ckv, ckv_b, kr, kr_b), (hq, hk, hv, lf, hg) = _mixer_inputs(x, wd, tabs, tm=m)

    qt = _per_head_mm("mla_absorb_q", qn, wd["w_absorb"])
    qr_h = jnp.transpose(qr.reshape(m, d["heads"], LANES), (1, 0, 2))
    o_lat = _attn_sample(qt, qr_h, cache_ckv, cache_kr, ckv_b, kr_b, wd["w_uk_t"], batch=batch, t_new=t_new,
                         heads=d["heads"], nope=d["nope"], rope=d["rope"], tk=512)
    o_mla = _value_up(o_lat, wd["w_uv_h"])

    o_hg, s_new = _hgrn(hq, hk, hv, lf, hg, state_hgrn, wd["g_hg_out"], batch=batch, t_len=t_new, blk=t_new,
                        tt=t_new, hb=8)

    x = _out_proj(x, o_mla, o_hg, wd, tm=m)
    h = _rmsnorm(x, wd["g_ffn"], 256)
    d_ff = d["d_ff"]
    u = _mm("ffn_up_sample", lambda acc: acc, [h], [wd["w_up"]], tm=m, tn=2 * _ff_tile(d_ff), out_dtypes=[F32])
    u3 = u.reshape(batch, t_new, 2 * d_ff)
    act = _conv_sample(u3, hist, wd["conv_w"], wd["conv_b"])
    taps = wd["conv_w"].shape[0]
    conv_state = jnp.concatenate([hist, u3], axis=1)[:, -(taps - 1):]
    x = _down_ple(x, act.reshape(m, d_ff).astype(BF16), p, wd, tm=m)
    return x, ckv.reshape(batch, t_new, -1), kr.reshape(batch, t_new, -1), s_new, conv_state


def kernel(x_prompt, x_sample, cache_ckv, cache_k_rope, state_hgrn, state_ffn_conv, p_prompt, p_sample,
           g_attn_norm, w_in, g_q_a, w_uq, g_kv_a, w_uk, w_uv, g_q_nope, g_q_rope, g_k_nope, g_k_rope,
           lb_logits, g_hg_out, w_o, g_ffn_norm, w_up, conv_w, conv_b, w_down, g_ple_norm, w_ple_gate, w_ple):
    depth = w_in.shape[0]
    batch, seq, d_model = x_prompt.shape
    dec_batch, t_new, _ = x_sample.shape
    assert t_new >= conv_w.shape[1] - 1
    lb_all = jnp.cumsum(jax.nn.softmax(lb_logits.astype(F32), axis=0), axis=0)
    xp = x_prompt.reshape(batch * seq, d_model)
    xs = x_sample.reshape(dec_batch * t_new, d_model)
    outs = [[] for _ in range(8)]
    for i in range(depth):
        wd = _layer_weights(i, lb_all[i], g_attn_norm, w_in, g_q_a, w_uq, g_kv_a, w_uk, w_uv, g_q_nope, g_q_rope,
                            g_k_nope, g_k_rope, g_hg_out, w_o, g_ffn_norm, w_up, conv_w, conv_b, w_down,
                            g_ple_norm, w_ple_gate, w_ple)
        xp, *new_p = _prompt_layer(xp, p_prompt[i].reshape(batch * seq, -1), wd, batch=batch, seq=seq)
        xs, *new_s = _sample_layer(xs, p_sample[i].reshape(dec_batch * t_new, -1), cache_ckv[i], cache_k_rope[i],
                                   state_hgrn[i], state_ffn_conv[i], wd, batch=dec_batch, t_new=t_new)
        for lst, val in zip(outs, new_p + new_s):
            lst.append(val)
    return (xp.reshape(batch, seq, d_model), xs.reshape(dec_batch, t_new, d_model), *[jnp.stack(o) for o in outs])
```

```python
import functools
from typing import NamedTuple

import numpy as np
import jax
import jax.numpy as jnp
from jax import lax
from jax.experimental import pallas as pl
from jax.experimental.pallas import tpu as pltpu

F32 = jnp.float32
BF16 = jnp.bfloat16

EPS = 1e-6
CHUNK = 64
ROPE_THETA = 10000.0
LANES = 128
NEG_BIG = -1e30
VMEM_LIMIT_BYTES = 56 * 1024 * 1024


class _Tiles(NamedTuple):
    rows: int = 1024
    cols: int = 512
    cols_low_rank: int = 2048
    cols_rope: int = 1024
    down_rows: int = 512
    down_cols: int = 512
    norm_rows: int = 256
    latent_rows: int = 512
    attn: int = 512
    attn_heads: int = 4
    hgrn_rows: int = 256
    hgrn_heads: int = 8
    ff_rows: int = 2048
    ff_chunk: int = 512
    sample_keys: int = 512


TILES = _Tiles()


def _params(*sem):
    return pltpu.CompilerParams(dimension_semantics=sem, vmem_limit_bytes=VMEM_LIMIT_BYTES)


def _dot(a, b):
    return jnp.dot(a, b, preferred_element_type=F32)


def _dot_nt(a, b):
    return lax.dot_general(a, b, (((1,), (1,)), ((), ())), preferred_element_type=F32)


def _dot_tn(a, b):
    return lax.dot_general(a, b, (((0,), (0,)), ((), ())), preferred_element_type=F32)


def _rms(x, g):
    ms = jnp.mean(x * x, axis=-1, keepdims=True)
    return x * lax.rsqrt(ms + EPS) * g


def _slab_rms(x, g, valid):
    outs = []
    for a in range(x.shape[1] // LANES):
        blk = x[:, a * LANES:(a + 1) * LANES]
        ms = jnp.sum(blk * blk, axis=-1, keepdims=True) * (1.0 / valid)
        outs.append(blk * lax.rsqrt(ms + EPS) * g[:, a * LANES:(a + 1) * LANES])
    return outs


def _rope_slab(y, c, sa, sb):
    return y * c + pltpu.roll(y, 96, 1) * sa + pltpu.roll(y, 32, 1) * sb


def _rmsnorm_kernel(x_ref, g_ref, o_ref):
    o_ref[...] = _rms(x_ref[...], g_ref[...]).astype(o_ref.dtype)


def _rmsnorm(x, g, tm):
    m, d = x.shape
    tm = min(tm, m)
    assert m % tm == 0
    return pl.pallas_call(
        _rmsnorm_kernel,
        grid=(m // tm,),
        in_specs=[pl.BlockSpec((tm, d), lambda i: (i, 0)), pl.BlockSpec((1, d), lambda i: (0, 0))],
        out_specs=pl.BlockSpec((tm, d), lambda i: (i, 0)),
        out_shape=jax.ShapeDtypeStruct((m, d), BF16),
        compiler_params=_params("parallel"),
        name="rmsnorm",
    )(x, g.reshape(1, d))


def _lane_fold(x):
    out = x[:, :LANES]
    for a in range(1, x.shape[1] // LANES):
        out = out + x[:, a * LANES:(a + 1) * LANES]
    return out


def _row_rinv(ssq, width):
    return lax.rsqrt(jnp.sum(ssq, axis=-1, keepdims=True) * (1.0 / width) + EPS)


def _mm_kernel(*refs, body, n_lhs, group_sizes, n_in, next_norm):
    ins, outs = refs[:n_in], refs[n_in:]
    lhs, rhs, rest = ins[:n_lhs], ins[n_lhs:2 * n_lhs], ins[2 * n_lhs:]
    accs, k = [], 0
    for gs in group_sizes:
        acc = None
        for _ in range(gs):
            d = _dot(lhs[k][...], rhs[k][...])
            acc = d if acc is None else acc + d
            k += 1
        accs.append(acc)
    if next_norm:
        *rest, gain_ref = rest
        *outs, scaled_ref, ssq_ref = outs
    res = body(*accs, *[r[...] for r in rest])
    if not isinstance(res, (tuple, list)):
        res = (res,)
    for o, r in zip(outs, res):
        o[...] = r.astype(o.dtype)
    if next_norm:
        x = res[0]
        scaled_ref[...] = (x * gain_ref[...]).astype(scaled_ref.dtype)
        @pl.when(pl.program_id(1) == 0)
        def _():
            ssq_ref[...] = jnp.zeros(ssq_ref.shape, F32)

        ssq_ref[...] += _lane_fold(x * x)


def _mm(name, body, lhs, rhs, *, tm, tn, out_dtypes, group_sizes=None, cols=(), rows=(), extras=(),
        next_norm_gain=None, n=None, rhs_col0=0, rhs_row_blocks=None):
    m = lhs[0].shape[0]
    n = n or rhs[0].shape[1]
    tm, tn = min(tm, m), min(tn, n)
    assert m % tm == 0 and n % tn == 0 and rhs_col0 % tn == 0, (name, m, n, tm, tn)
    j0 = rhs_col0 // tn
    group_sizes = tuple(group_sizes or (len(lhs),))
    rhs_row_blocks = rhs_row_blocks or (0,) * len(rhs)
    tile = pl.BlockSpec((tm, tn), lambda i, j: (i, j))
    in_specs = [pl.BlockSpec((tm, a.shape[1]), lambda i, j: (i, 0)) for a in lhs]
    in_specs += [pl.BlockSpec((a.shape[1], tn), lambda i, j, rb=rb: (rb, j + j0)) for a, rb in zip(lhs, rhs_row_blocks)]
    in_specs += [pl.BlockSpec((c.shape[0], tn), lambda i, j: (0, j)) for c in cols]
    in_specs += [tile for _ in rows]
    in_specs += [spec for _, spec in extras]
    args = list(lhs) + list(rhs) + list(cols) + list(rows) + [a for a, _ in extras]
    out_specs = [tile for _ in out_dtypes]
    out_shape = [jax.ShapeDtypeStruct((m, n), dt) for dt in out_dtypes]
    if next_norm_gain is not None:
        in_specs.append(pl.BlockSpec((1, tn), lambda i, j: (0, j)))
        args.append(next_norm_gain.reshape(1, n))
        out_specs += [tile, pl.BlockSpec((tm, LANES), lambda i, j: (i, 0))]
        out_shape += [jax.ShapeDtypeStruct((m, n), BF16), jax.ShapeDtypeStruct((m, LANES), F32)]
    kern = functools.partial(_mm_kernel, body=body, n_lhs=len(lhs), group_sizes=group_sizes, n_in=len(args),
                             next_norm=next_norm_gain is not None)
    outs = pl.pallas_call(
        kern,
        grid=(m // tm, n // tn),
        in_specs=in_specs,
        out_specs=out_specs,
        out_shape=out_shape,
        compiler_params=_params("parallel", "arbitrary"),
        name=name,
    )(*args)
    return outs if len(outs) > 1 else outs[0]


def _ssq_spec(tm):
    return pl.BlockSpec((tm, LANES), lambda i, j: (i, 0))


def _latent_kernel(h_ref, w_ref, gq_ref, gkv_ref, gkr_ref, c_ref, sa_ref, sb_ref,
                   cq_ref, ckv_ref, ckvb_ref, kr_ref, krb_ref, *, q_lora, kv_lora, rope):
    z = _dot(h_ref[...], w_ref[...])
    cq_ref[...] = _rms(z[:, :q_lora], gq_ref[...]).astype(cq_ref.dtype)
    ckv = _rms(z[:, q_lora:q_lora + kv_lora], gkv_ref[...])
    ckv_ref[...] = ckv
    ckvb_ref[...] = ckv.astype(ckvb_ref.dtype)
    (kr,) = _slab_rms(z[:, q_lora + kv_lora:], gkr_ref[...], rope)
    kr = _rope_slab(kr, c_ref[...], sa_ref[...], sb_ref[...])
    kr_ref[...] = kr[:, :rope]
    krb_ref[...] = kr.astype(krb_ref.dtype)


def _latent(h, w_lat, g_q_a, g_kv_a, g_k_rope_pad, tabs, *, q_lora, kv_lora, rope, tm):
    m, d = h.shape
    tm = min(tm, m)
    n = w_lat.shape[1]
    n_tab = tabs[0].shape[0] // tm
    row = lambda w: pl.BlockSpec((tm, w), lambda i: (i, 0))
    const = lambda r, w: pl.BlockSpec((r, w), lambda i: (0, 0))
    tab = pl.BlockSpec((tm, LANES), lambda i: (i % n_tab, 0))
    kern = functools.partial(_latent_kernel, q_lora=q_lora, kv_lora=kv_lora, rope=rope)
    return pl.pallas_call(
        kern,
        grid=(m // tm,),
        in_specs=[row(d), const(d, n), const(1, q_lora), const(1, kv_lora), const(1, LANES), tab, tab, tab],
        out_specs=[row(q_lora), row(kv_lora), row(kv_lora), row(rope), row(LANES)],
        out_shape=[jax.ShapeDtypeStruct((m, q_lora), BF16), jax.ShapeDtypeStruct((m, kv_lora), F32),
                   jax.ShapeDtypeStruct((m, kv_lora), BF16), jax.ShapeDtypeStruct((m, rope), F32),
                   jax.ShapeDtypeStruct((m, LANES), BF16)],
        compiler_params=_params("parallel"),
        name="mla_latent",
    )(h, w_lat, g_q_a.reshape(1, -1), g_kv_a.reshape(1, -1), g_k_rope_pad, *tabs)


def _attn_kernel(qn_ref, qr_ref, kn_ref, kr_ref, vt_ref, o_ref, m_scr, l_scr, acc_scr, *, tq, chunk, hb):
    qi = pl.program_id(2)
    lanes = lambda h: slice(h * LANES, (h + 1) * LANES)
    qs = [jnp.concatenate([qn_ref[:, lanes(h)], qr_ref[:, lanes(h)]], axis=-1) for h in range(hb)]
    m_scr[...] = jnp.full(m_scr.shape, NEG_BIG, F32)
    l_scr[...] = jnp.zeros(l_scr.shape, F32)
    acc_scr[...] = jnp.zeros(acc_scr.shape, F32)

    def block(kb, masked):
        off = pl.multiple_of(kb * tq, tq)
        kr = kr_ref[pl.ds(off, tq), :]
        scores = []
        for h in range(hb):
            k = jnp.concatenate([kn_ref[pl.ds(off, tq), lanes(h)], kr], axis=-1)
            s = _dot_nt(k, qs[h])
            if masked:
                kc = lax.broadcasted_iota(jnp.int32, (tq, tq), 0) // chunk
                qc = lax.broadcasted_iota(jnp.int32, (tq, tq), 1) // chunk
                s = jnp.where(kc <= qc, s, NEG_BIG)
            scores.append(s)
        probs = []
        for h in range(hb):
            m_i = m_scr[h]
            m_new = jnp.maximum(m_i, jnp.max(scores[h], axis=0, keepdims=True))
            alpha = jnp.exp2(m_i - m_new)
            p = jnp.exp2(scores[h] - m_new)
            l_scr[h] = alpha * l_scr[h] + jnp.sum(p, axis=0, keepdims=True)
            m_scr[h] = m_new
            probs.append((alpha, p.astype(BF16)))
        for h in range(hb):
            alpha, p = probs[h]
            acc_scr[h] = alpha * acc_scr[h] + _dot(vt_ref[kb, lanes(h), :], p)

    def body(kb, carry):
        block(kb, False)
        return carry

    lax.fori_loop(0, qi, body, 0)
    block(qi, True)
    for h in range(hb):
        o_ref[:, lanes(h)] = (acc_scr[h] / l_scr[h]).T.astype(o_ref.dtype)


def _attn_prompt(qn, qr, kn, krb, vt, *, batch, seq, heads, tq, hb):
    nq = seq // tq
    hb = min(hb, heads)
    assert heads % hb == 0
    qspec = pl.BlockSpec((tq, hb * LANES), lambda b, h, i: (b * nq + i, h))
    kspec = pl.BlockSpec((seq, hb * LANES), lambda b, h, i: (b, h))
    kern = functools.partial(_attn_kernel, tq=tq, chunk=CHUNK, hb=hb)
    return pl.pallas_call(
        kern,
        grid=(batch, heads // hb, nq),
        in_specs=[qspec, qspec, kspec, pl.BlockSpec((seq, LANES), lambda b, h, i: (b, 0)),
                  pl.BlockSpec((nq, hb * LANES, tq), lambda b, h, i: (b, h, 0))],
        out_specs=qspec,
        out_shape=jax.ShapeDtypeStruct(qn.shape, BF16),
        scratch_shapes=[pltpu.VMEM((hb, 1, tq), F32), pltpu.VMEM((hb, 1, tq), F32),
                        pltpu.VMEM((hb, LANES, tq), F32)],
        compiler_params=_params("parallel", "parallel", "arbitrary"),
        name="mla_attn_prompt",
    )(qn, qr, kn, krb, vt)


def _value_t_kernel(w_ref, x_ref, o_ref):
    o_ref[0] = _dot_nt(w_ref[...], x_ref[...]).astype(o_ref.dtype)


def _value_t(x, w_t, *, tk, tn):
    m, c = x.shape
    n = w_t.shape[0]
    tn = min(tn, n)
    return pl.pallas_call(
        _value_t_kernel,
        grid=(m // tk, n // tn),
        in_specs=[pl.BlockSpec((tn, c), lambda i, j: (j, 0)), pl.BlockSpec((tk, c), lambda i, j: (i, 0))],
        out_specs=pl.BlockSpec((1, tn, tk), lambda i, j: (i, j, 0)),
        out_shape=jax.ShapeDtypeStruct((m // tk, n, tk), BF16),
        compiler_params=_params("parallel", "arbitrary"),
        name="v_up_t",
    )(w_t, x)


def _absorb_kernel(x_ref, w_ref, o_ref):
    o_ref[0] = _dot(x_ref[...], w_ref[0]).astype(o_ref.dtype)


def _per_head_mm(name, x, w):
    m = x.shape[0]
    heads, _, n = w.shape
    return pl.pallas_call(
        _absorb_kernel,
        grid=(heads,),
        in_specs=[pl.BlockSpec((m, LANES), lambda h: (0, h)), pl.BlockSpec((1, LANES, n), lambda h: (h, 0, 0))],
        out_specs=pl.BlockSpec((1, m, n), lambda h: (h, 0, 0)),
        out_shape=jax.ShapeDtypeStruct((heads, m, n), BF16),
        compiler_params=_params("parallel"),
        name=name,
    )(x, w)


def _vup_kernel(x_ref, w_ref, o_ref):
    o_ref[...] = _dot(x_ref[0], w_ref[0]).astype(o_ref.dtype)


def _value_up(o_lat, w):
    heads, m, c = o_lat.shape
    return pl.pallas_call(
        _vup_kernel,
        grid=(heads,),
        in_specs=[pl.BlockSpec((1, m, c), lambda h: (h, 0, 0)), pl.BlockSpec((1, c, LANES), lambda h: (h, 0, 0))],
        out_specs=pl.BlockSpec((m, LANES), lambda h: (0, h)),
        out_shape=jax.ShapeDtypeStruct((m, heads * LANES), BF16),
        compiler_params=_params("parallel"),
        name="mla_value_up",
    )(o_lat, w)


def _attn_sample_kernel(qt_ref, qr_ref, cache_ref, kcache_ref, cnew_ref, knew_ref, wkt_ref, o_ref,
                        s_scr, *, heads, t_new, past, tk, nope, rope):
    hq = heads * t_new
    c_lat = cache_ref.shape[-1]
    qt = qt_ref[...].reshape(hq, c_lat)
    qr = qr_ref[...].reshape(hq, LANES)[:, :rope]
    wkt = wkt_ref[...]

    def scores(c_b, kr_b):
        n = c_b.shape[0]
        kn_t = _dot_nt(wkt, c_b)
        ssq = jnp.sum((kn_t * kn_t).reshape(heads, nope, n), axis=1)
        rinv = lax.rsqrt(ssq * (1.0 / nope) + EPS)
        rinv = jnp.broadcast_to(rinv[:, None, :], (heads, t_new, n)).reshape(hq, n)
        return _dot_nt(qt, c_b) * rinv + _dot_nt(qr, kr_b)

    for kt in range(past // tk):
        c_b = cache_ref[0, kt * tk:(kt + 1) * tk, :].astype(BF16)
        kr_b = kcache_ref[0, kt * tk:(kt + 1) * tk, :].astype(BF16)
        s_scr[:, kt * tk:(kt + 1) * tk] = scores(c_b, kr_b)
    c_new = cnew_ref[...]
    s_new = scores(c_new, knew_ref[...][:, :rope])

    s_old = s_scr[...]
    m_i = jnp.maximum(jnp.max(s_old, axis=-1, keepdims=True), jnp.max(s_new, axis=-1, keepdims=True))
    p_new = jnp.exp2(s_new - m_i)
    l_i = jnp.sum(p_new, axis=-1, keepdims=True)
    acc = _dot(p_new.astype(BF16), c_new)
    for kt in range(past // tk):
        p = jnp.exp2(s_scr[:, kt * tk:(kt + 1) * tk] - m_i)
        l_i = l_i + jnp.sum(p, axis=-1, keepdims=True)
        acc = acc + _dot(p.astype(BF16), cache_ref[0, kt * tk:(kt + 1) * tk, :].astype(BF16))
    o_ref[...] = (acc / l_i).reshape(heads, t_new, c_lat).astype(o_ref.dtype)


def _attn_sample(qt, qr_h, cache, kcache, ckv_new, kr_new, wkt, *, batch, t_new, heads, nope, rope, tk):
    past, c_lat = cache.shape[1], cache.shape[2]
    tk = min(tk, past)
    kern = functools.partial(_attn_sample_kernel, heads=heads, t_new=t_new, past=past, tk=tk, nope=nope, rope=rope)
    return pl.pallas_call(
        kern,
        grid=(batch,),
        in_specs=[pl.BlockSpec((heads, t_new, c_lat), lambda b: (0, b, 0)),
                  pl.BlockSpec((heads, t_new, LANES), lambda b: (0, b, 0)),
                  pl.BlockSpec((1, past, c_lat), lambda b: (b, 0, 0)),
                  pl.BlockSpec((1, past, rope), lambda b: (b, 0, 0)),
                  pl.BlockSpec((t_new, c_lat), lambda b: (b, 0)),
                  pl.BlockSpec((t_new, LANES), lambda b: (b, 0)),
                  pl.BlockSpec(wkt.shape, lambda b: (0, 0))],
        out_specs=pl.BlockSpec((heads, t_new, c_lat), lambda b: (0, b, 0)),
        out_shape=jax.ShapeDtypeStruct((heads, batch * t_new, c_lat), BF16),
        scratch_shapes=[pltpu.VMEM((heads * t_new, past), F32)],
        compiler_params=_params("parallel"),
        name="mla_attn_sample",
    )(qt, qr_h, cache, kcache, ckv_new, kr_new, wkt)


def _hgrn_tables(blk):
    t = np.arange(blk)
    levels = [blk >> (i + 1) for i in range(int(np.log2(blk)))]
    lvl = np.full((blk, blk), -1, np.int32)
    lvl[t, t] = 0
    for li, c in enumerate(levels, start=1):
        same_pair = (t[:, None] // (2 * c)) == (t[None, :] // (2 * c))
        split = (t[:, None] // c) != (t[None, :] // c)
        lvl[same_pair & split & (t[:, None] > t[None, :])] = li
    prefix = (t[None, :] <= t[:, None]).astype(np.float32)
    rows = [prefix]
    for c in (2, 1):
        mid = (t // (2 * c)) * (2 * c) + c - 1
        rows.append((t[None, :] <= mid[:, None]).astype(np.float32))
    return levels, jnp.asarray(lvl), jnp.asarray(np.concatenate(rows, axis=0), dtype=BF16)


def _hgrn_block(q_ref, k_ref, v_ref, lf_ref, g_ref, o_ref, st_scr, rows, sel, pair_masks, upper_masks, signs,
                g_out, *, blk, levels, hb):
    cols = lambda h: slice(h * LANES, (h + 1) * LANES)
    sums = []
    for h in range(hb):
        lf = lf_ref[rows, cols(h)]
        hi = lf.astype(BF16)
        lo = (lf - hi.astype(F32)).astype(BF16)
        both = _dot(sel, jnp.concatenate([hi, lo], axis=-1))
        sums.append(both[:, :LANES] + both[:, LANES:])
    intra = []
    for h in range(hb):
        b = sums[h][:blk]
        q, k = q_ref[rows, cols(h)], k_ref[rows, cols(h)]
        qf, kf = q.astype(F32), k.astype(F32)
        a = jnp.where(pair_masks[0], _dot_nt(q, k), 0.0)
        for li, c in enumerate(levels, start=1):
            if c == 2:
                ref = sums[h][blk:2 * blk]
            elif c == 1:
                ref = sums[h][2 * blk:]
            else:
                b3 = b.reshape(blk // (2 * c), 2 * c, LANES)
                ref = jnp.broadcast_to(b3[:, c - 1:c, :], b3.shape).reshape(blk, LANES)
            decay = jnp.exp2((b - ref) * signs[li - 1])
            w = (jnp.where(upper_masks[li - 1], qf, kf) * decay).astype(BF16)
            a = jnp.where(pair_masks[li], _dot_nt(w, w), a)
        intra.append(a.astype(BF16))
    for h in range(hb):
        b = sums[h][:blk]
        b_last = b[blk - 1:blk, :]
        qf, kf = q_ref[rows, cols(h)].astype(F32), k_ref[rows, cols(h)].astype(F32)
        v = v_ref[rows, cols(h)]
        s_t = st_scr[h]
        inter = _dot_nt((qf * jnp.exp2(b)).astype(BF16), s_t.astype(BF16))
        o = inter + _dot(intra[h], v)
        k_t = (kf * jnp.exp2(b_last - b)).astype(BF16)
        st_scr[h] = s_t * jnp.exp2(b_last) + _dot_tn(v, k_t)
        gf = g_ref[rows, cols(h)].astype(F32)
        o_ref[rows, cols(h)] = (_rms(o, g_out) * (gf * jax.nn.sigmoid(gf))).astype(o_ref.dtype)


def _hgrn_kernel(q_ref, k_ref, v_ref, lf_ref, g_ref, s0_ref, sel_ref, lvl_ref, gout_ref,
                 o_ref, s_out_ref, st_scr, *, blk, levels, hb, n_blk):
    tb = pl.program_id(2)

    @pl.when(tb == 0)
    def _():
        for h in range(hb):
            st_scr[h] = s0_ref[0, h].T

    sel, lvl, g_out = sel_ref[...], lvl_ref[...], gout_ref[...]
    pair_masks = [lvl == li for li in range(len(levels) + 1)]
    row = lax.broadcasted_iota(jnp.int32, (blk, LANES), 0)
    upper_masks = [(row // c) % 2 == 1 for c in levels]
    signs = [jnp.where(u, 1.0, -1.0) for u in upper_masks]

    def step(c, carry):
        rows = pl.ds(pl.multiple_of(c * blk, blk), blk)
        _hgrn_block(q_ref, k_ref, v_ref, lf_ref, g_ref, o_ref, st_scr, rows, sel, pair_masks, upper_masks, signs,
                    g_out, blk=blk, levels=levels, hb=hb)
        return carry

    lax.fori_loop(0, n_blk, step, 0)

    @pl.when(tb == pl.num_programs(2) - 1)
    def _():
        for h in range(hb):
            s_out_ref[0, h] = st_scr[h].T


def _hgrn(q, k, v, lf, gate, s0, g_out, *, batch, t_len, blk, tt, hb):
    heads = q.shape[1] // LANES
    tt, hb = min(tt, t_len), min(hb, heads)
    levels, lvl, sel = _hgrn_tables(blk)
    nt = t_len // tt
    xspec = pl.BlockSpec((tt, hb * LANES), lambda b, h, t: (b * nt + t, h))
    sspec = pl.BlockSpec((1, hb, LANES, LANES), lambda b, h, t: (b, h, 0, 0))
    const = lambda a: pl.BlockSpec(a.shape, lambda b, h, t: (0, 0))
    kern = functools.partial(_hgrn_kernel, blk=blk, levels=levels, hb=hb, n_blk=tt // blk)
    g_out = g_out.reshape(1, LANES)
    return pl.pallas_call(
        kern,
        grid=(batch, heads // hb, nt),
        in_specs=[xspec, xspec, xspec, xspec, xspec, sspec, const(sel), const(lvl), const(g_out)],
        out_specs=[xspec, sspec],
        out_shape=[jax.ShapeDtypeStruct(q.shape, BF16), jax.ShapeDtypeStruct(s0.shape, F32)],
        scratch_shapes=[pltpu.VMEM((hb, LANES, LANES), F32)],
        compiler_params=_params("parallel", "parallel", "arbitrary"),
        name="hgrn2",
    )(q, k, v, lf, gate, s0, sel, lvl, g_out)


def _silu(x):
    h = 0.5 * x
    return h + h * jnp.tanh(h)


def _ff_tile(d_ff):
    assert d_ff % LANES == 0
    return 2 * LANES if d_ff % (2 * LANES) == 0 else LANES


def _up_conv_kernel(h_ref, wg_ref, wu_ref, cwg_ref, cwu_ref, cbg_ref, cbu_ref,
                    act_ref, tailg_ref, tailu_ref, ug_scr, uu_scr, carry_scr, *, tm, cm, tiles_per_seq):
    i, j = pl.program_id(0), pl.program_id(1)
    seq_start = (i % tiles_per_seq) == 0
    halves = ((wg_ref, cwg_ref, cbg_ref, tailg_ref, ug_scr), (wu_ref, cwu_ref, cbu_ref, tailu_ref, uu_scr))
    for half, (_, _, _, _, scr) in enumerate(halves):
        @pl.when(seq_start)
        def _(scr=scr):
            scr[0:8, :] = jnp.zeros((8, scr.shape[1]), F32)

        @pl.when(jnp.logical_not(seq_start))
        def _(scr=scr, half=half):
            scr[0:8, :] = carry_scr[half, j]

    for r in range(tm // cm):
        h = h_ref[r * cm:(r + 1) * cm, :]
        conv = []
        for w_ref, cw_ref, cb_ref, _, scr in halves:
            u = _dot(h, w_ref[...])
            scr[8 + r * cm:8 + (r + 1) * cm, :] = u
            cw = cw_ref[...]
            conv.append(cb_ref[...] + cw[2:3] * u + cw[1:2] * scr[7 + r * cm:7 + (r + 1) * cm, :]
                        + cw[0:1] * scr[6 + r * cm:6 + (r + 1) * cm, :])
        act_ref[r * cm:(r + 1) * cm, :] = (_silu(conv[0]) * conv[1]).astype(act_ref.dtype)

    for half, (_, _, _, tail_ref, scr) in enumerate(halves):
        tail = scr[tm:tm + 8, :]
        carry_scr[half, j] = tail
        tail_ref[...] = tail


def _up_conv_prompt(h, w_up, conv_w, conv_b, *, seq, tm, cm):
    m, d = h.shape
    n = w_up.shape[1] // 2
    tn = _ff_tile(n)
    tm = min(tm, seq)
    cm = min(cm, tm)
    assert seq % tm == 0 and tm % cm == 0 and conv_w.shape[0] == 3
    nj = n // tn
    gate = lambda r: pl.BlockSpec((r, tn), lambda i, j: (0, j))
    up = lambda r: pl.BlockSpec((r, tn), lambda i, j: (0, j + nj))
    tail = pl.BlockSpec((8, tn), lambda i, j: (i, j))
    kern = functools.partial(_up_conv_kernel, tm=tm, cm=cm, tiles_per_seq=seq // tm)
    return pl.pallas_call(
        kern,
        grid=(m // tm, nj),
        in_specs=[pl.BlockSpec((tm, d), lambda i, j: (i, 0)), gate(d), up(d), gate(3), up(3), gate(1), up(1)],
        out_specs=[pl.BlockSpec((tm, tn), lambda i, j: (i, j)), tail, tail],
        out_shape=[jax.ShapeDtypeStruct((m, n), BF16), jax.ShapeDtypeStruct((m // tm * 8, n), F32),
                   jax.ShapeDtypeStruct((m // tm * 8, n), F32)],
        scratch_shapes=[pltpu.VMEM((tm + 8, tn), F32), pltpu.VMEM((tm + 8, tn), F32),
                        pltpu.VMEM((2, nj, 8, tn), F32)],
        compiler_params=_params("arbitrary", "arbitrary"),
        name="ffn_up_conv",
    )(h, w_up, w_up, conv_w, conv_w, conv_b, conv_b)


def _conv_sample_kernel(ug_ref, uu_ref, hg_ref, hu_ref, cwg_ref, cwu_ref, cbg_ref, cbu_ref, act_ref, *, t_len, taps):
    conv = [[None] * t_len, [None] * t_len]
    for half, (u_ref, hist_ref, cw_ref, cb_ref) in enumerate(
            ((ug_ref, hg_ref, cwg_ref, cbg_ref), (uu_ref, hu_ref, cwu_ref, cbu_ref))):
        cw, cb = cw_ref[...], cb_ref[...]
        rows = [hist_ref[:, r, :] for r in range(taps - 1)] + [u_ref[:, t, :] for t in range(t_len)]
        for t in range(t_len):
            c = cb
            for tap in range(taps):
                c = c + cw[tap:tap + 1] * rows[t + tap]
            conv[half][t] = c
    for t in range(t_len):
        act_ref[:, t, :] = _silu(conv[0][t]) * conv[1][t]


def _conv_sample(u, hist, conv_w, conv_b):
    bsz, t_len, n2 = u.shape
    n = n2 // 2
    tn = _ff_tile(n)
    nj = n // tn
    taps = conv_w.shape[0]
    gate = lambda *lead: pl.BlockSpec((*lead, tn), lambda j: (*([0] * len(lead)), j))
    up = lambda *lead: pl.BlockSpec((*lead, tn), lambda j: (*([0] * len(lead)), j + nj))
    kern = functools.partial(_conv_sample_kernel, t_len=t_len, taps=taps)
    return pl.pallas_call(
        kern,
        grid=(nj,),
        in_specs=[gate(bsz, t_len), up(bsz, t_len), gate(bsz, taps - 1), up(bsz, taps - 1),
                  gate(taps), up(taps), gate(1), up(1)],
        out_specs=gate(bsz, t_len),
        out_shape=jax.ShapeDtypeStruct((bsz, t_len, n), F32),
        compiler_params=_params("parallel"),
        name="ffn_conv_sample",
    )(u, u, hist, hist, conv_w, conv_w, conv_b, conv_b)


def _pad_cols(w, n):
    return jnp.pad(w, ((0, 0), (0, n - w.shape[1])))


def _rope_tables(pos, rope):
    half = rope // 2
    inv = 1.0 / (ROPE_THETA ** (jnp.arange(half, dtype=F32) / half))
    ang = pos.astype(F32)[:, None] * inv[None, :]
    cos, sin = jnp.cos(ang), jnp.sin(ang)
    z = jnp.zeros_like(cos)
    zz = jnp.zeros((pos.shape[0], LANES - rope), F32)
    c = jnp.concatenate([cos, cos, zz], axis=-1)
    sa = jnp.concatenate([-sin, z, zz], axis=-1)
    sb = jnp.concatenate([z, sin, zz], axis=-1)
    return c, sa, sb


def _layer_weights(i, lb, g_attn_norm, w_in, g_q_a, w_uq, g_kv_a, w_uk, w_uv, g_q_nope, g_q_rope, g_k_nope,
                   g_k_rope, g_hg_out, w_o, g_ffn_norm, w_up, conv_w, conv_b, w_down, g_ple_norm, w_ple_gate,
                   w_ple):
    q_lora, kv_lora = g_q_a.shape[1], g_kv_a.shape[1]
    nope, rope = g_q_nope.shape[1], g_q_rope.shape[1]
    heads, v_dim = w_uk.shape[2], w_uv.shape[3]
    hg_width = lb.shape[0]
    d_ff = w_down.shape[1]
    assert nope == LANES and v_dim == LANES and g_hg_out.shape[1] == LANES and 2 * rope == LANES
    scale = float((nope + rope) ** -0.5) * float(np.log2(np.e))
    o3 = q_lora + kv_lora + rope
    w_in_i = w_in[i]
    wd = {"dims": dict(q_lora=q_lora, kv_lora=kv_lora, nope=nope, rope=rope, heads=heads, hg_width=hg_width,
                       d_ff=d_ff)}
    wd["g_attn"], wd["g_ffn"], wd["g_ple"] = g_attn_norm[i], g_ffn_norm[i], g_ple_norm[i]
    wd["w_lat"] = _pad_cols(w_in_i[:, :o3], o3 + LANES - rope).astype(BF16)
    wd["g_q_a"], wd["g_kv_a"] = g_q_a[i], g_kv_a[i]
    wd["g_k_rope_pad"] = jnp.pad(g_k_rope[i], (0, LANES - rope)).reshape(1, LANES)
    wd["w_hgrn"] = w_in_i[:, o3:].astype(BF16)
    wd["lb"] = lb.reshape(1, hg_width)
    wq = w_uq[i].reshape(q_lora, heads, nope + rope)
    wd["w_qn"] = wq[:, :, :nope].reshape(q_lora, heads * nope).astype(BF16)
    half = rope // 2
    slab = lambda a, b: jnp.pad(jnp.concatenate([a, b], axis=-1), [(0, 0)] * (a.ndim - 1) + [(0, LANES - rope)])
    flat = lambda w: w.reshape(q_lora, heads * LANES).astype(BF16)
    wr1, wr2 = wq[:, :, nope:nope + half], wq[:, :, nope + half:]
    wd["w_qr"], wd["w_qr_swap"] = flat(slab(wr1, wr2)), flat(slab(wr2, wr1))
    wd["g_qn"] = (jnp.tile(g_q_nope[i], heads) * scale).reshape(1, heads * nope)
    gr = g_q_rope[i] * scale
    wd["g_qr"] = jnp.tile(slab(gr[:half], gr[half:]), heads).reshape(1, heads * LANES)
    wd["g_qr_swap"] = jnp.tile(slab(gr[half:], gr[:half]), heads).reshape(1, heads * LANES)
    wd["w_uk"] = w_uk[i].reshape(kv_lora, heads * nope).astype(BF16)
    wd["g_kn"] = jnp.tile(g_k_nope[i], heads).reshape(1, heads * nope)
    wd["w_uv_t"] = jnp.transpose(w_uv[i], (1, 2, 0)).reshape(heads * v_dim, kv_lora).astype(BF16)
    wd["w_absorb"] = jnp.transpose(w_uk[i] * g_k_nope[i][None, None, :], (1, 2, 0)).astype(BF16)
    wd["w_uk_t"] = jnp.transpose(w_uk[i], (1, 2, 0)).reshape(heads * nope, kv_lora).astype(BF16)
    wd["w_uv_h"] = jnp.transpose(w_uv[i], (1, 0, 2)).astype(BF16)
    wd["g_hg_out"] = g_hg_out[i]
    mla_width = heads * v_dim
    assert w_o.shape[1] == 2 * mla_width
    wd["w_o"] = w_o[i].astype(BF16)
    wd["w_up"], wd["conv_w"], wd["conv_b"] = w_up[i].astype(BF16), conv_w[i], conv_b[i].reshape(1, 2 * d_ff)
    wd["w_down"] = w_down[i].astype(BF16)
    wd["w_ple_gate"], wd["w_ple"] = w_ple_gate[i].astype(BF16), w_ple[i].astype(BF16)
    return wd


def _mixer_inputs(x, wd, tabs, *, tm):
    d = wd["dims"]
    h = _rmsnorm(x, wd["g_attn"], TILES.norm_rows)
    cq, ckv, ckv_b, kr, kr_b = _latent(h, wd["w_lat"], wd["g_q_a"], wd["g_kv_a"], wd["g_k_rope_pad"], tabs,
                                      q_lora=d["q_lora"], kv_lora=d["kv_lora"], rope=d["rope"],
                                      tm=TILES.latent_rows)
    m = x.shape[0]
    n_tab = tabs[0].shape[0] // min(tm, m)
    tab_spec = pl.BlockSpec((min(tm, m), LANES), lambda i, j: (i % n_tab, 0))

    def qn_body(acc, g):
        return jnp.concatenate(_slab_rms(acc, g, d["nope"]), axis=-1)

    def qr_body(acc, acc_swap, g, g_swap, c, s):
        outs = []
        for a in range(acc.shape[1] // LANES):
            sl = slice(a * LANES, (a + 1) * LANES)
            x, xs = acc[:, sl], acc_swap[:, sl]
            rinv = lax.rsqrt(jnp.sum(x * x, axis=-1, keepdims=True) * (1.0 / d["rope"]) + EPS)
            outs.append((x * rinv * g[:, sl]) * c + (xs * rinv * g_swap[:, sl]) * s)
        return jnp.concatenate(outs, axis=-1)

    c_tab, sa_tab, sb_tab = tabs
    qn = _mm("q_nope", qn_body, [cq], [wd["w_qn"]], tm=tm, tn=TILES.cols_low_rank, out_dtypes=[BF16],
             cols=[wd["g_qn"]])
    qr = _mm("q_rope", qr_body, [cq, cq], [wd["w_qr"], wd["w_qr_swap"]], tm=tm, tn=TILES.cols_rope, out_dtypes=[BF16],
             group_sizes=(1, 1), cols=[wd["g_qr"], wd["g_qr_swap"]],
             extras=[(c_tab, tab_spec), (sa_tab + sb_tab, tab_spec)])

    def f_body(acc, lb):
        s = jax.nn.sigmoid(acc)
        log2_f = jnp.log2(lb + (1.0 - lb) * s)
        return log2_f, (1.0 - lb) * (1.0 - s)

    ident = lambda acc: acc
    hw = d["hg_width"]
    proj = functools.partial(_mm, lhs=[h], rhs=[wd["w_hgrn"]], tm=tm, tn=TILES.cols, n=hw)
    hq = proj("hgrn_q", ident, out_dtypes=[BF16], rhs_col0=0)
    lf, hk = proj("hgrn_f", f_body, out_dtypes=[F32, BF16], cols=[wd["lb"]], rhs_col0=hw)
    hv = proj("hgrn_v", ident, out_dtypes=[BF16], rhs_col0=2 * hw)
    hg = proj("hgrn_g", ident, out_dtypes=[BF16], rhs_col0=3 * hw)
    return (qn, qr, ckv, ckv_b, kr, kr_b), (hq, hk, hv, lf, hg)


def _out_proj(x, o_mla, o_hg, wd, *, tm):
    return _mm("out_proj", lambda acc, res: res + acc, [o_mla, o_hg], [wd["w_o"], wd["w_o"]], rhs_row_blocks=(0, 1),
               tm=tm, tn=TILES.cols, out_dtypes=[F32], rows=[x])


def _down_ple(x, act, p, wd, *, tm):
    x, xg, ssq = _mm("ffn_down", lambda acc, res: res + acc, [act], [wd["w_down"]], tm=min(tm, TILES.down_rows),
                     tn=TILES.down_cols, out_dtypes=[F32], rows=[x], next_norm_gain=wd["g_ple"])
    width = x.shape[1]

    def ple_body(gate, emb, res, ssq_blk):
        return res + emb * jax.nn.sigmoid(gate * _row_rinv(ssq_blk, width))

    return _mm("ple", ple_body, [xg, p.astype(BF16)], [wd["w_ple_gate"], wd["w_ple"]], tm=tm, tn=TILES.cols,
               out_dtypes=[F32], group_sizes=(1, 1), rows=[x], extras=[(ssq, _ssq_spec(min(tm, x.shape[0])))])


def _prompt_layer(x, p, wd, *, batch, seq):
    d = wd["dims"]
    tm = min(TILES.rows, seq)
    assert seq % tm == 0
    tabs = _rope_tables(jnp.arange(seq), d["rope"])
    (qn, qr, ckv, ckv_b, kr, kr_b), (hq, hk, hv, lf, hg) = _mixer_inputs(x, wd, tabs, tm=tm)

    def kn_body(acc, g):
        return jnp.concatenate(_slab_rms(acc, g, d["nope"]), axis=-1)

    kn = _mm("k_nope", kn_body, [ckv_b], [wd["w_uk"]], tm=tm, tn=TILES.cols_low_rank, out_dtypes=[BF16],
             cols=[wd["g_kn"]])
    tq = min(TILES.attn, seq)
    vt = _value_t(ckv_b, wd["w_uv_t"], tk=tq, tn=TILES.cols_low_rank)
    o_mla = _attn_prompt(qn, qr, kn, kr_b, vt, batch=batch, seq=seq, heads=d["heads"], tq=tq, hb=TILES.attn_heads)

    hg_heads = d["hg_width"] // LANES
    s0 = jnp.zeros((batch, hg_heads, LANES, LANES), F32)
    o_hg, s_fin = _hgrn(hq, hk, hv, lf, hg, s0, wd["g_hg_out"], batch=batch, t_len=seq, blk=CHUNK,
                        tt=TILES.hgrn_rows, hb=TILES.hgrn_heads)

    x = _out_proj(x, o_mla, o_hg, wd, tm=tm)
    h = _rmsnorm(x, wd["g_ffn"], TILES.norm_rows)
    tm_up = min(TILES.ff_rows, seq)
    act, tail_g, tail_u = _up_conv_prompt(h, wd["w_up"], wd["conv_w"], wd["conv_b"], seq=seq, tm=tm_up,
                                          cm=TILES.ff_chunk)
    last = lambda t: t.reshape(batch, seq // tm_up, 8, -1)[:, -1, 6:, :]
    conv_state = jnp.concatenate([last(tail_g), last(tail_u)], axis=-1)
    x = _down_ple(x, act, p, wd, tm=tm)
    return x, ckv.reshape(batch, seq, -1), kr.reshape(batch, seq, -1), s_fin, conv_state


def _sample_layer(x, p, cache_ckv, cache_kr, state_hgrn, hist, wd, *, batch, t_new):
    d = wd["dims"]
    m = batch * t_new
    past = cache_ckv.shape[1]
    tabs = _rope_tables(jnp.tile(past + jnp.arange(t_new), batch), d["rope"])
    (qn, qr, ckv, ckv_b, kr, kr_b), (hq, hk, hv, lf, hg) = _mixer_inputs(x, wd, tabs, tm=m)

    qt = _per_head_mm("mla_absorb_q", qn, wd["w_absorb"])
    qr_h = jnp.transpose(qr.reshape(m, d["heads"], LANES), (1, 0, 2))
    o_lat = _attn_sample(qt, qr_h, cache_ckv, cache_kr, ckv_b, kr_b, wd["w_uk_t"], batch=batch, t_new=t_new,
                         heads=d["heads"], nope=d["nope"], rope=d["rope"], tk=TILES.sample_keys)
    o_mla = _value_up(o_lat, wd["w_uv_h"])

    o_hg, s_new = _hgrn(hq, hk, hv, lf, hg, state_hgrn, wd["g_hg_out"], batch=batch, t_len=t_new, blk=t_new,
                        tt=t_new, hb=TILES.hgrn_heads)

    x = _out_proj(x, o_mla, o_hg, wd, tm=m)
    h = _rmsnorm(x, wd["g_ffn"], TILES.norm_rows)
    d_ff = d["d_ff"]
    u = _mm("ffn_up_sample", lambda acc: acc, [h], [wd["w_up"]], tm=m, tn=2 * _ff_tile(d_ff), out_dtypes=[F32])
    u3 = u.reshape(batch, t_new, 2 * d_ff)
    act = _conv_sample(u3, hist, wd["conv_w"], wd["conv_b"])
    taps = wd["conv_w"].shape[0]
    conv_state = jnp.concatenate([hist, u3], axis=1)[:, -(taps - 1):]
    x = _down_ple(x, act.reshape(m, d_ff).astype(BF16), p, wd, tm=m)
    return x, ckv.reshape(batch, t_new, -1), kr.reshape(batch, t_new, -1), s_new, conv_state


def kernel(x_prompt, x_sample, cache_ckv, cache_k_rope, state_hgrn, state_ffn_conv, p_prompt, p_sample,
           g_attn_norm, w_in, g_q_a, w_uq, g_kv_a, w_uk, w_uv, g_q_nope, g_q_rope, g_k_nope, g_k_rope,
           lb_logits, g_hg_out, w_o, g_ffn_norm, w_up, conv_w, conv_b, w_down, g_ple_norm, w_ple_gate, w_ple):
    depth = w_in.shape[0]
    batch, seq, d_model = x_prompt.shape
    dec_batch, t_new, _ = x_sample.shape
    assert t_new >= conv_w.shape[1] - 1
    lb_all = jnp.cumsum(jax.nn.softmax(lb_logits.astype(F32), axis=0), axis=0)
    xp = x_prompt.reshape(batch * seq, d_model)
    xs = x_sample.reshape(dec_batch * t_new, d_model)
    outs = [[] for _ in range(8)]
    for i in range(depth):
        wd = _layer_weights(i, lb_all[i], g_attn_norm, w_in, g_q_a, w_uq, g_kv_a, w_uk, w_uv, g_q_nope, g_q_rope,
                            g_k_nope, g_k_rope, g_hg_out, w_o, g_ffn_norm, w_up, conv_w, conv_b, w_down,
                            g_ple_norm, w_ple_gate, w_ple)
        xp, *new_p = _prompt_layer(xp, p_prompt[i].reshape(batch * seq, -1), wd, batch=batch, seq=seq)
        xs, *new_s = _sample_layer(xs, p_sample[i].reshape(dec_batch * t_new, -1), cache_ckv[i], cache_k_rope[i],
                                   state_hgrn[i], state_ffn_conv[i], wd, batch=dec_batch, t_new=t_new)
        for lst, val in zip(outs, new_p + new_s):
            lst.append(val)
    return (xp.reshape(batch, seq, d_model), xs.reshape(dec_batch, t_new, d_model), *[jnp.stack(o) for o in outs])
```

```python
import functools
from typing import NamedTuple

import numpy as np
import jax
import jax.numpy as jnp
from jax import lax
from jax.experimental import pallas as pl
from jax.experimental.pallas import tpu as pltpu

F32 = jnp.float32
BF16 = jnp.bfloat16

EPS = 1e-6
CHUNK = 64
ROPE_THETA = 10000.0
LANES = 128
NEG_BIG = -1e30
VMEM_LIMIT_BYTES = 56 * 1024 * 1024


class _Tiles(NamedTuple):
    rows: int = 1024
    cols: int = 512
    cols_low_rank: int = 2048
    cols_rope: int = 1024
    down_rows: int = 512
    down_cols: int = 512
    norm_rows: int = 256
    latent_rows: int = 512
    attn: int = 512
    attn_heads: int = 4
    hgrn_rows: int = 512
    hgrn_heads: int = 8
    hgrn_heads_short: int = 16
    ff_rows: int = 2048
    ff_chunk: int = 512
    sample_keys: int = 1024


TILES = _Tiles()


def _params(*sem):
    return pltpu.CompilerParams(dimension_semantics=sem, vmem_limit_bytes=VMEM_LIMIT_BYTES)


def _dot(a, b):
    return jnp.dot(a, b, preferred_element_type=F32)


def _dot_nt(a, b):
    return lax.dot_general(a, b, (((1,), (1,)), ((), ())), preferred_element_type=F32)


def _dot_tn(a, b):
    return lax.dot_general(a, b, (((0,), (0,)), ((), ())), preferred_element_type=F32)


def _rms(x, g):
    ms = jnp.mean(x * x, axis=-1, keepdims=True)
    return x * lax.rsqrt(ms + EPS) * g


def _slab_rms(x, g, valid):
    outs = []
    for a in range(x.shape[1] // LANES):
        blk = x[:, a * LANES:(a + 1) * LANES]
        ms = jnp.sum(blk * blk, axis=-1, keepdims=True) * (1.0 / valid)
        outs.append(blk * lax.rsqrt(ms + EPS) * g[:, a * LANES:(a + 1) * LANES])
    return outs


def _rope_slab(y, c, sa, sb):
    return y * c + pltpu.roll(y, 96, 1) * sa + pltpu.roll(y, 32, 1) * sb


def _rmsnorm_kernel(x_ref, g_ref, o_ref):
    o_ref[...] = _rms(x_ref[...], g_ref[...]).astype(o_ref.dtype)


def _rmsnorm(x, g, tm):
    m, d = x.shape
    tm = min(tm, m)
    assert m % tm == 0
    return pl.pallas_call(
        _rmsnorm_kernel,
        grid=(m // tm,),
        in_specs=[pl.BlockSpec((tm, d), lambda i: (i, 0)), pl.BlockSpec((1, d), lambda i: (0, 0))],
        out_specs=pl.BlockSpec((tm, d), lambda i: (i, 0)),
        out_shape=jax.ShapeDtypeStruct((m, d), BF16),
        compiler_params=_params("parallel"),
        name="rmsnorm",
    )(x, g.reshape(1, d))


def _lane_fold(x):
    out = x[:, :LANES]
    for a in range(1, x.shape[1] // LANES):
        out = out + x[:, a * LANES:(a + 1) * LANES]
    return out


def _row_rinv(ssq, width):
    return lax.rsqrt(jnp.sum(ssq, axis=-1, keepdims=True) * (1.0 / width) + EPS)


def _mm_kernel(*refs, body, n_lhs, group_sizes, n_in, next_norm):
    ins, outs = refs[:n_in], refs[n_in:]
    lhs, rhs, rest = ins[:n_lhs], ins[n_lhs:2 * n_lhs], ins[2 * n_lhs:]
    accs, k = [], 0
    for gs in group_sizes:
        acc = None
        for _ in range(gs):
            d = _dot(lhs[k][...], rhs[k][...])
            acc = d if acc is None else acc + d
            k += 1
        accs.append(acc)
    if next_norm:
        *rest, gain_ref = rest
        *outs, scaled_ref, ssq_ref = outs
    res = body(*accs, *[r[...] for r in rest])
    if not isinstance(res, (tuple, list)):
        res = (res,)
    for o, r in zip(outs, res):
        o[...] = r.astype(o.dtype)
    if next_norm:
        x = res[0]
        scaled_ref[...] = (x * gain_ref[...]).astype(scaled_ref.dtype)
        @pl.when(pl.program_id(1) == 0)
        def _():
            ssq_ref[...] = jnp.zeros(ssq_ref.shape, F32)

        ssq_ref[...] += _lane_fold(x * x)


def _mm(name, body, lhs, rhs, *, tm, tn, out_dtypes, group_sizes=None, cols=(), rows=(), extras=(),
        next_norm_gain=None, n=None, rhs_col0=0, rhs_row_blocks=None):
    m = lhs[0].shape[0]
    n = n or rhs[0].shape[1]
    tm, tn = min(tm, m), min(tn, n)
    assert m % tm == 0 and n % tn == 0 and rhs_col0 % tn == 0, (name, m, n, tm, tn)
    j0 = rhs_col0 // tn
    group_sizes = tuple(group_sizes or (len(lhs),))
    rhs_row_blocks = rhs_row_blocks or (0,) * len(rhs)
    tile = pl.BlockSpec((tm, tn), lambda i, j: (i, j))
    in_specs = [pl.BlockSpec((tm, a.shape[1]), lambda i, j: (i, 0)) for a in lhs]
    in_specs += [pl.BlockSpec((a.shape[1], tn), lambda i, j, rb=rb: (rb, j + j0)) for a, rb in zip(lhs, rhs_row_blocks)]
    in_specs += [pl.BlockSpec((c.shape[0], tn), lambda i, j: (0, j)) for c in cols]
    in_specs += [tile for _ in rows]
    in_specs += [spec for _, spec in extras]
    args = list(lhs) + list(rhs) + list(cols) + list(rows) + [a for a, _ in extras]
    out_specs = [tile for _ in out_dtypes]
    out_shape = [jax.ShapeDtypeStruct((m, n), dt) for dt in out_dtypes]
    if next_norm_gain is not None:
        in_specs.append(pl.BlockSpec((1, tn), lambda i, j: (0, j)))
        args.append(next_norm_gain.reshape(1, n))
        out_specs += [tile, pl.BlockSpec((tm, LANES), lambda i, j: (i, 0))]
        out_shape += [jax.ShapeDtypeStruct((m, n), BF16), jax.ShapeDtypeStruct((m, LANES), F32)]
    kern = functools.partial(_mm_kernel, body=body, n_lhs=len(lhs), group_sizes=group_sizes, n_in=len(args),
                             next_norm=next_norm_gain is not None)
    outs = pl.pallas_call(
        kern,
        grid=(m // tm, n // tn),
        in_specs=in_specs,
        out_specs=out_specs,
        out_shape=out_shape,
        compiler_params=_params("parallel", "arbitrary"),
        name=name,
    )(*args)
    return outs if len(outs) > 1 else outs[0]


def _ssq_spec(tm):
    return pl.BlockSpec((tm, LANES), lambda i, j: (i, 0))


def _latent_kernel(h_ref, w_ref, gq_ref, gkv_ref, gkr_ref, c_ref, sa_ref, sb_ref,
                   cq_ref, ckv_ref, ckvb_ref, kr_ref, krb_ref, *, q_lora, kv_lora, rope):
    z = _dot(h_ref[...], w_ref[...])
    cq_ref[...] = _rms(z[:, :q_lora], gq_ref[...]).astype(cq_ref.dtype)
    ckv = _rms(z[:, q_lora:q_lora + kv_lora], gkv_ref[...])
    ckv_ref[...] = ckv
    ckvb_ref[...] = ckv.astype(ckvb_ref.dtype)
    (kr,) = _slab_rms(z[:, q_lora + kv_lora:], gkr_ref[...], rope)
    kr = _rope_slab(kr, c_ref[...], sa_ref[...], sb_ref[...])
    kr_ref[...] = kr[:, :rope]
    krb_ref[...] = kr.astype(krb_ref.dtype)


def _latent(h, w_lat, g_q_a, g_kv_a, g_k_rope_pad, tabs, *, q_lora, kv_lora, rope, tm):
    m, d = h.shape
    tm = min(tm, m)
    n = w_lat.shape[1]
    n_tab = tabs[0].shape[0] // tm
    row = lambda w: pl.BlockSpec((tm, w), lambda i: (i, 0))
    const = lambda r, w: pl.BlockSpec((r, w), lambda i: (0, 0))
    tab = pl.BlockSpec((tm, LANES), lambda i: (i % n_tab, 0))
    kern = functools.partial(_latent_kernel, q_lora=q_lora, kv_lora=kv_lora, rope=rope)
    return pl.pallas_call(
        kern,
        grid=(m // tm,),
        in_specs=[row(d), const(d, n), const(1, q_lora), const(1, kv_lora), const(1, LANES), tab, tab, tab],
        out_specs=[row(q_lora), row(kv_lora), row(kv_lora), row(rope), row(LANES)],
        out_shape=[jax.ShapeDtypeStruct((m, q_lora), BF16), jax.ShapeDtypeStruct((m, kv_lora), F32),
                   jax.ShapeDtypeStruct((m, kv_lora), BF16), jax.ShapeDtypeStruct((m, rope), F32),
                   jax.ShapeDtypeStruct((m, LANES), BF16)],
        compiler_params=_params("parallel"),
        name="mla_latent",
    )(h, w_lat, g_q_a.reshape(1, -1), g_kv_a.reshape(1, -1), g_k_rope_pad, *tabs)


def _attn_kernel(qn_ref, qr_ref, kn_ref, kr_ref, vt_ref, o_ref, m_scr, l_scr, acc_scr, *, tq, chunk, hb):
    qi = pl.program_id(2)
    lanes = lambda h: slice(h * LANES, (h + 1) * LANES)
    qs = [jnp.concatenate([qn_ref[:, lanes(h)], qr_ref[:, lanes(h)]], axis=-1) for h in range(hb)]
    m_scr[...] = jnp.full(m_scr.shape, NEG_BIG, F32)
    l_scr[...] = jnp.zeros(l_scr.shape, F32)
    acc_scr[...] = jnp.zeros(acc_scr.shape, F32)

    def block(kb, masked):
        off = pl.multiple_of(kb * tq, tq)
        kr = kr_ref[pl.ds(off, tq), :]
        scores = []
        for h in range(hb):
            k = jnp.concatenate([kn_ref[pl.ds(off, tq), lanes(h)], kr], axis=-1)
            s = _dot_nt(k, qs[h])
            if masked:
                kc = lax.broadcasted_iota(jnp.int32, (tq, tq), 0) // chunk
                qc = lax.broadcasted_iota(jnp.int32, (tq, tq), 1) // chunk
                s = jnp.where(kc <= qc, s, NEG_BIG)
            scores.append(s)
        probs = []
        for h in range(hb):
            m_i = m_scr[h]
            m_new = jnp.maximum(m_i, jnp.max(scores[h], axis=0, keepdims=True))
            alpha = jnp.exp2(m_i - m_new)
            p = jnp.exp2(scores[h] - m_new)
            l_scr[h] = alpha * l_scr[h] + jnp.sum(p, axis=0, keepdims=True)
            m_scr[h] = m_new
            probs.append((alpha, p.astype(BF16)))
        for h in range(hb):
            alpha, p = probs[h]
            acc_scr[h] = alpha * acc_scr[h] + _dot(vt_ref[kb, lanes(h), :], p)

    def body(kb, carry):
        block(kb, False)
        return carry

    lax.fori_loop(0, qi, body, 0)
    block(qi, True)
    for h in range(hb):
        o_ref[:, lanes(h)] = (acc_scr[h] / l_scr[h]).T.astype(o_ref.dtype)


def _attn_prompt(qn, qr, kn, krb, vt, *, batch, seq, heads, tq, hb):
    nq = seq // tq
    hb = min(hb, heads)
    assert heads % hb == 0
    qspec = pl.BlockSpec((tq, hb * LANES), lambda b, h, i: (b * nq + i, h))
    kspec = pl.BlockSpec((seq, hb * LANES), lambda b, h, i: (b, h))
    kern = functools.partial(_attn_kernel, tq=tq, chunk=CHUNK, hb=hb)
    return pl.pallas_call(
        kern,
        grid=(batch, heads // hb, nq),
        in_specs=[qspec, qspec, kspec, pl.BlockSpec((seq, LANES), lambda b, h, i: (b, 0)),
                  pl.BlockSpec((nq, hb * LANES, tq), lambda b, h, i: (b, h, 0))],
        out_specs=qspec,
        out_shape=jax.ShapeDtypeStruct(qn.shape, BF16),
        scratch_shapes=[pltpu.VMEM((hb, 1, tq), F32), pltpu.VMEM((hb, 1, tq), F32),
                        pltpu.VMEM((hb, LANES, tq), F32)],
        compiler_params=_params("parallel", "parallel", "arbitrary"),
        name="mla_attn_prompt",
    )(qn, qr, kn, krb, vt)


def _value_t_kernel(w_ref, x_ref, o_ref):
    o_ref[0] = _dot_nt(w_ref[...], x_ref[...]).astype(o_ref.dtype)


def _value_t(x, w_t, *, tk, tn):
    m, c = x.shape
    n = w_t.shape[0]
    tn = min(tn, n)
    return pl.pallas_call(
        _value_t_kernel,
        grid=(m // tk, n // tn),
        in_specs=[pl.BlockSpec((tn, c), lambda i, j: (j, 0)), pl.BlockSpec((tk, c), lambda i, j: (i, 0))],
        out_specs=pl.BlockSpec((1, tn, tk), lambda i, j: (i, j, 0)),
        out_shape=jax.ShapeDtypeStruct((m // tk, n, tk), BF16),
        compiler_params=_params("parallel", "arbitrary"),
        name="v_up_t",
    )(w_t, x)


def _absorb_kernel(x_ref, w_ref, o_ref):
    o_ref[0] = _dot(x_ref[...], w_ref[0]).astype(o_ref.dtype)


def _per_head_mm(name, x, w):
    m = x.shape[0]
    heads, _, n = w.shape
    return pl.pallas_call(
        _absorb_kernel,
        grid=(heads,),
        in_specs=[pl.BlockSpec((m, LANES), lambda h: (0, h)), pl.BlockSpec((1, LANES, n), lambda h: (h, 0, 0))],
        out_specs=pl.BlockSpec((1, m, n), lambda h: (h, 0, 0)),
        out_shape=jax.ShapeDtypeStruct((heads, m, n), BF16),
        compiler_params=_params("parallel"),
        name=name,
    )(x, w)


def _vup_kernel(x_ref, w_ref, o_ref):
    o_ref[...] = _dot(x_ref[0], w_ref[0]).astype(o_ref.dtype)


def _value_up(o_lat, w):
    heads, m, c = o_lat.shape
    return pl.pallas_call(
        _vup_kernel,
        grid=(heads,),
        in_specs=[pl.BlockSpec((1, m, c), lambda h: (h, 0, 0)), pl.BlockSpec((1, c, LANES), lambda h: (h, 0, 0))],
        out_specs=pl.BlockSpec((m, LANES), lambda h: (0, h)),
        out_shape=jax.ShapeDtypeStruct((m, heads * LANES), BF16),
        compiler_params=_params("parallel"),
        name="mla_value_up",
    )(o_lat, w)


def _attn_sample_kernel(qt_ref, qr_ref, cache_ref, kcache_ref, cnew_ref, knew_ref, wkt_ref, o_ref,
                        *, heads, t_new, past, tk, nope, rope):
    hq = heads * t_new
    c_lat = cache_ref.shape[-1]
    qt = qt_ref[...].reshape(hq, c_lat)
    qr = qr_ref[...].reshape(hq, LANES)[:, :rope]
    wkt = wkt_ref[...]

    def scores(c_b, kr_b):
        n = c_b.shape[0]
        kn_t = _dot_nt(wkt, c_b)
        ssq = jnp.sum((kn_t * kn_t).reshape(heads, nope, n), axis=1)
        rinv = lax.rsqrt(ssq * (1.0 / nope) + EPS)
        rinv = jnp.broadcast_to(rinv[:, None, :], (heads, t_new, n)).reshape(hq, n)
        return _dot_nt(qt, c_b) * rinv + _dot_nt(qr, kr_b)

    c_new = cnew_ref[...]
    s_new = scores(c_new, knew_ref[...][:, :rope])
    m_i = jnp.max(s_new, axis=-1, keepdims=True)
    p = jnp.exp2(s_new - m_i)
    l_i = jnp.sum(p, axis=-1, keepdims=True)
    acc = _dot(p.astype(BF16), c_new)
    for kt in range(past // tk):
        c_b = cache_ref[0, kt * tk:(kt + 1) * tk, :].astype(BF16)
        s = scores(c_b, kcache_ref[0, kt * tk:(kt + 1) * tk, :].astype(BF16))
        m_new = jnp.maximum(m_i, jnp.max(s, axis=-1, keepdims=True))
        alpha = jnp.exp2(m_i - m_new)
        p = jnp.exp2(s - m_new)
        l_i = alpha * l_i + jnp.sum(p, axis=-1, keepdims=True)
        acc = alpha * acc + _dot(p.astype(BF16), c_b)
        m_i = m_new
    o_ref[...] = (acc / l_i).reshape(heads, t_new, c_lat).astype(o_ref.dtype)


def _attn_sample(qt, qr_h, cache, kcache, ckv_new, kr_new, wkt, *, batch, t_new, heads, nope, rope, tk):
    past, c_lat = cache.shape[1], cache.shape[2]
    tk = min(tk, past)
    kern = functools.partial(_attn_sample_kernel, heads=heads, t_new=t_new, past=past, tk=tk, nope=nope, rope=rope)
    return pl.pallas_call(
        kern,
        grid=(batch,),
        in_specs=[pl.BlockSpec((heads, t_new, c_lat), lambda b: (0, b, 0)),
                  pl.BlockSpec((heads, t_new, LANES), lambda b: (0, b, 0)),
                  pl.BlockSpec((1, past, c_lat), lambda b: (b, 0, 0)),
                  pl.BlockSpec((1, past, rope), lambda b: (b, 0, 0)),
                  pl.BlockSpec((t_new, c_lat), lambda b: (b, 0)),
                  pl.BlockSpec((t_new, LANES), lambda b: (b, 0)),
                  pl.BlockSpec(wkt.shape, lambda b: (0, 0))],
        out_specs=pl.BlockSpec((heads, t_new, c_lat), lambda b: (0, b, 0)),
        out_shape=jax.ShapeDtypeStruct((heads, batch * t_new, c_lat), BF16),
        compiler_params=_params("parallel"),
        name="mla_attn_sample",
    )(qt, qr_h, cache, kcache, ckv_new, kr_new, wkt)


def _hgrn_tables(blk):
    t = np.arange(blk)
    levels = [blk >> (i + 1) for i in range(int(np.log2(blk)))]
    lvl = np.full((blk, blk), -1, np.int32)
    lvl[t, t] = 0
    for li, c in enumerate(levels, start=1):
        same_pair = (t[:, None] // (2 * c)) == (t[None, :] // (2 * c))
        split = (t[:, None] // c) != (t[None, :] // c)
        lvl[same_pair & split & (t[:, None] > t[None, :])] = li
    prefix = (t[None, :] <= t[:, None]).astype(np.float32)
    rows = [prefix]
    for c in (2, 1):
        mid = (t // (2 * c)) * (2 * c) + c - 1
        rows.append((t[None, :] <= mid[:, None]).astype(np.float32))
    return levels, jnp.asarray(lvl), jnp.asarray(np.concatenate(rows, axis=0), dtype=BF16)


def _hgrn_block(q_ref, k_ref, v_ref, lf_ref, g_ref, o_ref, st_scr, rows, sel, pair_masks, upper_masks, signs,
                g_out, *, blk, levels, hb):
    cols = lambda h: slice(h * LANES, (h + 1) * LANES)
    sums = []
    for h in range(hb):
        lf = lf_ref[rows, cols(h)]
        hi = lf.astype(BF16)
        lo = (lf - hi.astype(F32)).astype(BF16)
        both = _dot(sel, jnp.concatenate([hi, lo], axis=-1))
        sums.append(both[:, :LANES] + both[:, LANES:])
    intra = []
    for h in range(hb):
        b = sums[h][:blk]
        q, k = q_ref[rows, cols(h)], k_ref[rows, cols(h)]
        qf, kf = q.astype(F32), k.astype(F32)
        a = jnp.where(pair_masks[0], _dot_nt(q, k), 0.0)
        for li, c in enumerate(levels, start=1):
            if c == 2:
                ref = sums[h][blk:2 * blk]
            elif c == 1:
                ref = sums[h][2 * blk:]
            else:
                b3 = b.reshape(blk // (2 * c), 2 * c, LANES)
                ref = jnp.broadcast_to(b3[:, c - 1:c, :], b3.shape).reshape(blk, LANES)
            decay = jnp.exp2((b - ref) * signs[li - 1])
            w = (jnp.where(upper_masks[li - 1], qf, kf) * decay).astype(BF16)
            a = jnp.where(pair_masks[li], _dot_nt(w, w), a)
        intra.append(a.astype(BF16))
    for h in range(hb):
        b = sums[h][:blk]
        b_last = b[blk - 1:blk, :]
        qf, kf = q_ref[rows, cols(h)].astype(F32), k_ref[rows, cols(h)].astype(F32)
        v = v_ref[rows, cols(h)]
        s_t = st_scr[h]
        inter = _dot_nt((qf * jnp.exp2(b)).astype(BF16), s_t.astype(BF16))
        o = inter + _dot(intra[h], v)
        k_t = (kf * jnp.exp2(b_last - b)).astype(BF16)
        st_scr[h] = s_t * jnp.exp2(b_last) + _dot_tn(v, k_t)
        gf = g_ref[rows, cols(h)].astype(F32)
        o_ref[rows, cols(h)] = (_rms(o, g_out) * (gf * jax.nn.sigmoid(gf))).astype(o_ref.dtype)


def _hgrn_kernel(q_ref, k_ref, v_ref, lf_ref, g_ref, s0_ref, sel_ref, lvl_ref, gout_ref,
                 o_ref, s_out_ref, st_scr, *, blk, levels, hb, n_blk):
    tb = pl.program_id(2)

    @pl.when(tb == 0)
    def _():
        for h in range(hb):
            st_scr[h] = s0_ref[0, h].T

    sel, lvl, g_out = sel_ref[...], lvl_ref[...], gout_ref[...]
    pair_masks = [lvl == li for li in range(len(levels) + 1)]
    row = lax.broadcasted_iota(jnp.int32, (blk, LANES), 0)
    upper_masks = [(row // c) % 2 == 1 for c in levels]
    signs = [jnp.where(u, 1.0, -1.0) for u in upper_masks]

    def step(c, carry):
        rows = pl.ds(pl.multiple_of(c * blk, blk), blk)
        _hgrn_block(q_ref, k_ref, v_ref, lf_ref, g_ref, o_ref, st_scr, rows, sel, pair_masks, upper_masks, signs,
                    g_out, blk=blk, levels=levels, hb=hb)
        return carry

    lax.fori_loop(0, n_blk, step, 0)

    @pl.when(tb == pl.num_programs(2) - 1)
    def _():
        for h in range(hb):
            s_out_ref[0, h] = st_scr[h].T


def _hgrn(q, k, v, lf, gate, s0, g_out, *, batch, t_len, blk, tt, hb):
    heads = q.shape[1] // LANES
    tt, hb = min(tt, t_len), min(hb, heads)
    levels, lvl, sel = _hgrn_tables(blk)
    nt = t_len // tt
    xspec = pl.BlockSpec((tt, hb * LANES), lambda b, h, t: (b * nt + t, h))
    sspec = pl.BlockSpec((1, hb, LANES, LANES), lambda b, h, t: (b, h, 0, 0))
    const = lambda a: pl.BlockSpec(a.shape, lambda b, h, t: (0, 0))
    kern = functools.partial(_hgrn_kernel, blk=blk, levels=levels, hb=hb, n_blk=tt // blk)
    g_out = g_out.reshape(1, LANES)
    return pl.pallas_call(
        kern,
        grid=(batch, heads // hb, nt),
        in_specs=[xspec, xspec, xspec, xspec, xspec, sspec, const(sel), const(lvl), const(g_out)],
        out_specs=[xspec, sspec],
        out_shape=[jax.ShapeDtypeStruct(q.shape, BF16), jax.ShapeDtypeStruct(s0.shape, F32)],
        scratch_shapes=[pltpu.VMEM((hb, LANES, LANES), F32)],
        compiler_params=_params("parallel", "parallel", "arbitrary"),
        name="hgrn2",
    )(q, k, v, lf, gate, s0, sel, lvl, g_out)


def _silu(x):
    h = 0.5 * x
    return h + h * jnp.tanh(h)


def _ff_tile(d_ff):
    assert d_ff % LANES == 0
    return 2 * LANES if d_ff % (2 * LANES) == 0 else LANES


def _up_conv_kernel(h_ref, wg_ref, wu_ref, cwg_ref, cwu_ref, cbg_ref, cbu_ref,
                    act_ref, tailg_ref, tailu_ref, ug_scr, uu_scr, carry_scr, *, tm, cm, tiles_per_seq):
    i, j = pl.program_id(0), pl.program_id(1)
    seq_start = (i % tiles_per_seq) == 0
    halves = ((wg_ref, cwg_ref, cbg_ref, tailg_ref, ug_scr), (wu_ref, cwu_ref, cbu_ref, tailu_ref, uu_scr))
    for half, (_, _, _, _, scr) in enumerate(halves):
        @pl.when(seq_start)
        def _(scr=scr):
            scr[0:8, :] = jnp.zeros((8, scr.shape[1]), F32)

        @pl.when(jnp.logical_not(seq_start))
        def _(scr=scr, half=half):
            scr[0:8, :] = carry_scr[half, j]

    for r in range(tm // cm):
        h = h_ref[r * cm:(r + 1) * cm, :]
        conv = []
        for w_ref, cw_ref, cb_ref, _, scr in halves:
            u = _dot(h, w_ref[...])
            scr[8 + r * cm:8 + (r + 1) * cm, :] = u
            cw = cw_ref[...]
            conv.append(cb_ref[...] + cw[2:3] * u + cw[1:2] * scr[7 + r * cm:7 + (r + 1) * cm, :]
                        + cw[0:1] * scr[6 + r * cm:6 + (r + 1) * cm, :])
        act_ref[r * cm:(r + 1) * cm, :] = (_silu(conv[0]) * conv[1]).astype(act_ref.dtype)

    for half, (_, _, _, tail_ref, scr) in enumerate(halves):
        tail = scr[tm:tm + 8, :]
        carry_scr[half, j] = tail
        tail_ref[...] = tail


def _up_conv_prompt(h, w_up, conv_w, conv_b, *, seq, tm, cm):
    m, d = h.shape
    n = w_up.shape[1] // 2
    tn = _ff_tile(n)
    tm = min(tm, seq)
    cm = min(cm, tm)
    assert seq % tm == 0 and tm % cm == 0 and conv_w.shape[0] == 3
    nj = n // tn
    gate = lambda r: pl.BlockSpec((r, tn), lambda i, j: (0, j))
    up = lambda r: pl.BlockSpec((r, tn), lambda i, j: (0, j + nj))
    tail = pl.BlockSpec((8, tn), lambda i, j: (i, j))
    kern = functools.partial(_up_conv_kernel, tm=tm, cm=cm, tiles_per_seq=seq // tm)
    return pl.pallas_call(
        kern,
        grid=(m // tm, nj),
        in_specs=[pl.BlockSpec((tm, d), lambda i, j: (i, 0)), gate(d), up(d), gate(3), up(3), gate(1), up(1)],
        out_specs=[pl.BlockSpec((tm, tn), lambda i, j: (i, j)), tail, tail],
        out_shape=[jax.ShapeDtypeStruct((m, n), BF16), jax.ShapeDtypeStruct((m // tm * 8, n), F32),
                   jax.ShapeDtypeStruct((m // tm * 8, n), F32)],
        scratch_shapes=[pltpu.VMEM((tm + 8, tn), F32), pltpu.VMEM((tm + 8, tn), F32),
                        pltpu.VMEM((2, nj, 8, tn), F32)],
        compiler_params=_params("arbitrary", "arbitrary"),
        name="ffn_up_conv",
    )(h, w_up, w_up, conv_w, conv_w, conv_b, conv_b)


def _conv_sample_kernel(ug_ref, uu_ref, hg_ref, hu_ref, cwg_ref, cwu_ref, cbg_ref, cbu_ref, act_ref, *, t_len, taps):
    conv = [[None] * t_len, [None] * t_len]
    for half, (u_ref, hist_ref, cw_ref, cb_ref) in enumerate(
            ((ug_ref, hg_ref, cwg_ref, cbg_ref), (uu_ref, hu_ref, cwu_ref, cbu_ref))):
        cw, cb = cw_ref[...], cb_ref[...]
        rows = [hist_ref[:, r, :] for r in range(taps - 1)] + [u_ref[:, t, :] for t in range(t_len)]
        for t in range(t_len):
            c = cb
            for tap in range(taps):
                c = c + cw[tap:tap + 1] * rows[t + tap]
            conv[half][t] = c
    for t in range(t_len):
        act_ref[:, t, :] = _silu(conv[0][t]) * conv[1][t]


def _conv_sample(u, hist, conv_w, conv_b):
    bsz, t_len, n2 = u.shape
    n = n2 // 2
    tn = _ff_tile(n)
    nj = n // tn
    taps = conv_w.shape[0]
    gate = lambda *lead: pl.BlockSpec((*lead, tn), lambda j: (*([0] * len(lead)), j))
    up = lambda *lead: pl.BlockSpec((*lead, tn), lambda j: (*([0] * len(lead)), j + nj))
    kern = functools.partial(_conv_sample_kernel, t_len=t_len, taps=taps)
    return pl.pallas_call(
        kern,
        grid=(nj,),
        in_specs=[gate(bsz, t_len), up(bsz, t_len), gate(bsz, taps - 1), up(bsz, taps - 1),
                  gate(taps), up(taps), gate(1), up(1)],
        out_specs=gate(bsz, t_len),
        out_shape=jax.ShapeDtypeStruct((bsz, t_len, n), F32),
        compiler_params=_params("parallel"),
        name="ffn_conv_sample",
    )(u, u, hist, hist, conv_w, conv_w, conv_b, conv_b)


def _pad_cols(w, n):
    return jnp.pad(w, ((0, 0), (0, n - w.shape[1])))


def _rope_tables(pos, rope):
    half = rope // 2
    inv = 1.0 / (ROPE_THETA ** (jnp.arange(half, dtype=F32) / half))
    ang = pos.astype(F32)[:, None] * inv[None, :]
    cos, sin = jnp.cos(ang), jnp.sin(ang)
    z = jnp.zeros_like(cos)
    zz = jnp.zeros((pos.shape[0], LANES - rope), F32)
    c = jnp.concatenate([cos, cos, zz], axis=-1)
    sa = jnp.concatenate([-sin, z, zz], axis=-1)
    sb = jnp.concatenate([z, sin, zz], axis=-1)
    return c, sa, sb


def _layer_weights(i, lb, g_attn_norm, w_in, g_q_a, w_uq, g_kv_a, w_uk, w_uv, g_q_nope, g_q_rope, g_k_nope,
                   g_k_rope, g_hg_out, w_o, g_ffn_norm, w_up, conv_w, conv_b, w_down, g_ple_norm, w_ple_gate,
                   w_ple):
    q_lora, kv_lora = g_q_a.shape[1], g_kv_a.shape[1]
    nope, rope = g_q_nope.shape[1], g_q_rope.shape[1]
    heads, v_dim = w_uk.shape[2], w_uv.shape[3]
    hg_width = lb.shape[0]
    d_ff = w_down.shape[1]
    assert nope == LANES and v_dim == LANES and g_hg_out.shape[1] == LANES and 2 * rope == LANES
    scale = float((nope + rope) ** -0.5) * float(np.log2(np.e))
    o3 = q_lora + kv_lora + rope
    w_in_i = w_in[i]
    wd = {"dims": dict(q_lora=q_lora, kv_lora=kv_lora, nope=nope, rope=rope, heads=heads, hg_width=hg_width,
                       d_ff=d_ff)}
    wd["g_attn"], wd["g_ffn"], wd["g_ple"] = g_attn_norm[i], g_ffn_norm[i], g_ple_norm[i]
    wd["w_lat"] = _pad_cols(w_in_i[:, :o3], o3 + LANES - rope).astype(BF16)
    wd["g_q_a"], wd["g_kv_a"] = g_q_a[i], g_kv_a[i]
    wd["g_k_rope_pad"] = jnp.pad(g_k_rope[i], (0, LANES - rope)).reshape(1, LANES)
    wd["w_hgrn"] = w_in_i[:, o3:].astype(BF16)
    wd["lb"] = lb.reshape(1, hg_width)
    wq = w_uq[i].reshape(q_lora, heads, nope + rope)
    wd["w_qn"] = wq[:, :, :nope].reshape(q_lora, heads * nope).astype(BF16)
    half = rope // 2
    slab = lambda a, b: jnp.pad(jnp.concatenate([a, b], axis=-1), [(0, 0)] * (a.ndim - 1) + [(0, LANES - rope)])
    flat = lambda w: w.reshape(q_lora, heads * LANES).astype(BF16)
    wr1, wr2 = wq[:, :, nope:nope + half], wq[:, :, nope + half:]
    wd["w_qr"], wd["w_qr_swap"] = flat(slab(wr1, wr2)), flat(slab(wr2, wr1))
    wd["g_qn"] = (jnp.tile(g_q_nope[i], heads) * scale).reshape(1, heads * nope)
    gr = g_q_rope[i] * scale
    wd["g_qr"] = jnp.tile(slab(gr[:half], gr[half:]), heads).reshape(1, heads * LANES)
    wd["g_qr_swap"] = jnp.tile(slab(gr[half:], gr[:half]), heads).reshape(1, heads * LANES)
    wd["w_uk"] = w_uk[i].reshape(kv_lora, heads * nope).astype(BF16)
    wd["g_kn"] = jnp.tile(g_k_nope[i], heads).reshape(1, heads * nope)
    wd["w_uv_t"] = jnp.transpose(w_uv[i], (1, 2, 0)).reshape(heads * v_dim, kv_lora).astype(BF16)
    wd["w_absorb"] = jnp.transpose(w_uk[i] * g_k_nope[i][None, None, :], (1, 2, 0)).astype(BF16)
    wd["w_uk_t"] = jnp.transpose(w_uk[i], (1, 2, 0)).reshape(heads * nope, kv_lora).astype(BF16)
    wd["w_uv_h"] = jnp.transpose(w_uv[i], (1, 0, 2)).astype(BF16)
    wd["g_hg_out"] = g_hg_out[i]
    mla_width = heads * v_dim
    assert w_o.shape[1] == 2 * mla_width
    wd["w_o"] = w_o[i].astype(BF16)
    wd["w_up"], wd["conv_w"], wd["conv_b"] = w_up[i].astype(BF16), conv_w[i], conv_b[i].reshape(1, 2 * d_ff)
    wd["w_down"] = w_down[i].astype(BF16)
    wd["w_ple_gate"], wd["w_ple"] = w_ple_gate[i].astype(BF16), w_ple[i].astype(BF16)
    return wd


def _mixer_inputs(x, wd, tabs, *, tm):
    d = wd["dims"]
    h = _rmsnorm(x, wd["g_attn"], TILES.norm_rows)
    cq, ckv, ckv_b, kr, kr_b = _latent(h, wd["w_lat"], wd["g_q_a"], wd["g_kv_a"], wd["g_k_rope_pad"], tabs,
                                      q_lora=d["q_lora"], kv_lora=d["kv_lora"], rope=d["rope"],
                                      tm=TILES.latent_rows)
    m = x.shape[0]
    n_tab = tabs[0].shape[0] // min(tm, m)
    tab_spec = pl.BlockSpec((min(tm, m), LANES), lambda i, j: (i % n_tab, 0))

    def qn_body(acc, g):
        return jnp.concatenate(_slab_rms(acc, g, d["nope"]), axis=-1)

    def qr_body(acc, acc_swap, g, g_swap, c, s):
        outs = []
        for a in range(acc.shape[1] // LANES):
            sl = slice(a * LANES, (a + 1) * LANES)
            x, xs = acc[:, sl], acc_swap[:, sl]
            rinv = lax.rsqrt(jnp.sum(x * x, axis=-1, keepdims=True) * (1.0 / d["rope"]) + EPS)
            outs.append((x * rinv * g[:, sl]) * c + (xs * rinv * g_swap[:, sl]) * s)
        return jnp.concatenate(outs, axis=-1)

    c_tab, sa_tab, sb_tab = tabs
    qn = _mm("q_nope", qn_body, [cq], [wd["w_qn"]], tm=tm, tn=TILES.cols_low_rank, out_dtypes=[BF16],
             cols=[wd["g_qn"]])
    qr = _mm("q_rope", qr_body, [cq, cq], [wd["w_qr"], wd["w_qr_swap"]], tm=tm, tn=TILES.cols_rope, out_dtypes=[BF16],
             group_sizes=(1, 1), cols=[wd["g_qr"], wd["g_qr_swap"]],
             extras=[(c_tab, tab_spec), (sa_tab + sb_tab, tab_spec)])

    def f_body(acc, lb):
        s = jax.nn.sigmoid(acc)
        log2_f = jnp.log2(lb + (1.0 - lb) * s)
        return log2_f, (1.0 - lb) * (1.0 - s)

    ident = lambda acc: acc
    hw = d["hg_width"]
    proj = functools.partial(_mm, lhs=[h], rhs=[wd["w_hgrn"]], tm=tm, tn=TILES.cols, n=hw)
    hq = proj("hgrn_q", ident, out_dtypes=[BF16], rhs_col0=0)
    lf, hk = proj("hgrn_f", f_body, out_dtypes=[F32, BF16], cols=[wd["lb"]], rhs_col0=hw)
    hv = proj("hgrn_v", ident, out_dtypes=[BF16], rhs_col0=2 * hw)
    hg = proj("hgrn_g", ident, out_dtypes=[BF16], rhs_col0=3 * hw)
    return (qn, qr, ckv, ckv_b, kr, kr_b), (hq, hk, hv, lf, hg)


def _out_proj(x, o_mla, o_hg, wd, *, tm):
    return _mm("out_proj", lambda acc, res: res + acc, [o_mla, o_hg], [wd["w_o"], wd["w_o"]], rhs_row_blocks=(0, 1),
               tm=tm, tn=TILES.cols, out_dtypes=[F32], rows=[x])


def _down_ple(x, act, p, wd, *, tm):
    x, xg, ssq = _mm("ffn_down", lambda acc, res: res + acc, [act], [wd["w_down"]], tm=min(tm, TILES.down_rows),
                     tn=TILES.down_cols, out_dtypes=[F32], rows=[x], next_norm_gain=wd["g_ple"])
    width = x.shape[1]

    def ple_body(gate, emb, res, ssq_blk):
        return res + emb * jax.nn.sigmoid(gate * _row_rinv(ssq_blk, width))

    return _mm("ple", ple_body, [xg, p.astype(BF16)], [wd["w_ple_gate"], wd["w_ple"]], tm=tm, tn=TILES.cols,
               out_dtypes=[F32], group_sizes=(1, 1), rows=[x], extras=[(ssq, _ssq_spec(min(tm, x.shape[0])))])


def _prompt_layer(x, p, wd, *, batch, seq):
    d = wd["dims"]
    tm = min(TILES.rows, seq)
    assert seq % tm == 0
    tabs = _rope_tables(jnp.arange(seq), d["rope"])
    (qn, qr, ckv, ckv_b, kr, kr_b), (hq, hk, hv, lf, hg) = _mixer_inputs(x, wd, tabs, tm=tm)

    def kn_body(acc, g):
        return jnp.concatenate(_slab_rms(acc, g, d["nope"]), axis=-1)

    kn = _mm("k_nope", kn_body, [ckv_b], [wd["w_uk"]], tm=tm, tn=TILES.cols_low_rank, out_dtypes=[BF16],
             cols=[wd["g_kn"]])
    tq = min(TILES.attn, seq)
    vt = _value_t(ckv_b, wd["w_uv_t"], tk=tq, tn=TILES.cols_low_rank)
    o_mla = _attn_prompt(qn, qr, kn, kr_b, vt, batch=batch, seq=seq, heads=d["heads"], tq=tq, hb=TILES.attn_heads)

    hg_heads = d["hg_width"] // LANES
    s0 = jnp.zeros((batch, hg_heads, LANES, LANES), F32)
    o_hg, s_fin = _hgrn(hq, hk, hv, lf, hg, s0, wd["g_hg_out"], batch=batch, t_len=seq, blk=CHUNK,
                        tt=TILES.hgrn_rows, hb=TILES.hgrn_heads)

    x = _out_proj(x, o_mla, o_hg, wd, tm=tm)
    h = _rmsnorm(x, wd["g_ffn"], TILES.norm_rows)
    tm_up = min(TILES.ff_rows, seq)
    act, tail_g, tail_u = _up_conv_prompt(h, wd["w_up"], wd["conv_w"], wd["conv_b"], seq=seq, tm=tm_up,
                                          cm=TILES.ff_chunk)
    last = lambda t: t.reshape(batch, seq // tm_up, 8, -1)[:, -1, 6:, :]
    conv_state = jnp.concatenate([last(tail_g), last(tail_u)], axis=-1)
    x = _down_ple(x, act, p, wd, tm=tm)
    return x, ckv.reshape(batch, seq, -1), kr.reshape(batch, seq, -1), s_fin, conv_state


def _sample_layer(x, p, cache_ckv, cache_kr, state_hgrn, hist, wd, *, batch, t_new):
    d = wd["dims"]
    m = batch * t_new
    past = cache_ckv.shape[1]
    tabs = _rope_tables(jnp.tile(past + jnp.arange(t_new), batch), d["rope"])
    (qn, qr, ckv, ckv_b, kr, kr_b), (hq, hk, hv, lf, hg) = _mixer_inputs(x, wd, tabs, tm=m)

    qt = _per_head_mm("mla_absorb_q", qn, wd["w_absorb"])
    qr_h = jnp.transpose(qr.reshape(m, d["heads"], LANES), (1, 0, 2))
    o_lat = _attn_sample(qt, qr_h, cache_ckv, cache_kr, ckv_b, kr_b, wd["w_uk_t"], batch=batch, t_new=t_new,
                         heads=d["heads"], nope=d["nope"], rope=d["rope"], tk=TILES.sample_keys)
    o_mla = _value_up(o_lat, wd["w_uv_h"])

    o_hg, s_new = _hgrn(hq, hk, hv, lf, hg, state_hgrn, wd["g_hg_out"], batch=batch, t_len=t_new, blk=t_new,
                        tt=t_new, hb=TILES.hgrn_heads_short)

    x = _out_proj(x, o_mla, o_hg, wd, tm=m)
    h = _rmsnorm(x, wd["g_ffn"], TILES.norm_rows)
    d_ff = d["d_ff"]
    u = _mm("ffn_up_sample", lambda acc: acc, [h], [wd["w_up"]], tm=m, tn=2 * _ff_tile(d_ff), out_dtypes=[F32])
    u3 = u.reshape(batch, t_new, 2 * d_ff)
    act = _conv_sample(u3, hist, wd["conv_w"], wd["conv_b"])
    taps = wd["conv_w"].shape[0]
    conv_state = jnp.concatenate([hist, u3], axis=1)[:, -(taps - 1):]
    x = _down_ple(x, act.reshape(m, d_ff).astype(BF16), p, wd, tm=m)
    return x, ckv.reshape(batch, t_new, -1), kr.reshape(batch, t_new, -1), s_new, conv_state


def kernel(x_prompt, x_sample, cache_ckv, cache_k_rope, state_hgrn, state_ffn_conv, p_prompt, p_sample,
           g_attn_norm, w_in, g_q_a, w_uq, g_kv_a, w_uk, w_uv, g_q_nope, g_q_rope, g_k_nope, g_k_rope,
           lb_logits, g_hg_out, w_o, g_ffn_norm, w_up, conv_w, conv_b, w_down, g_ple_norm, w_ple_gate, w_ple):
    depth = w_in.shape[0]
    batch, seq, d_model = x_prompt.shape
    dec_batch, t_new, _ = x_sample.shape
    assert t_new >= conv_w.shape[1] - 1
    lb_all = jnp.cumsum(jax.nn.softmax(lb_logits.astype(F32), axis=0), axis=0)
    xp = x_prompt.reshape(batch * seq, d_model)
    xs = x_sample.reshape(dec_batch * t_new, d_model)
    outs = [[] for _ in range(8)]
    for i in range(depth):
        wd = _layer_weights(i, lb_all[i], g_attn_norm, w_in, g_q_a, w_uq, g_kv_a, w_uk, w_uv, g_q_nope, g_q_rope,
                            g_k_nope, g_k_rope, g_hg_out, w_o, g_ffn_norm, w_up, conv_w, conv_b, w_down,
                            g_ple_norm, w_ple_gate, w_ple)
        xp, *new_p = _prompt_layer(xp, p_prompt[i].reshape(batch * seq, -1), wd, batch=batch, seq=seq)
        xs, *new_s = _sample_layer(xs, p_sample[i].reshape(dec_batch * t_new, -1), cache_ckv[i], cache_k_rope[i],
                                   state_hgrn[i], state_ffn_conv[i], wd, batch=dec_batch, t_new=t_new)
        for lst, val in zip(outs, new_p + new_s):
            lst.append(val)
    return (xp.reshape(batch, seq, d_model), xs.reshape(dec_batch, t_new, d_model), *[jnp.stack(o) for o in outs])
```

```python
import functools
from typing import NamedTuple

import numpy as np
import jax
import jax.numpy as jnp
from jax import lax
from jax.experimental import pallas as pl
from jax.experimental.pallas import tpu as pltpu

F32 = jnp.float32
BF16 = jnp.bfloat16

EPS = 1e-6
CHUNK = 64
ROPE_THETA = 10000.0
LANES = 128
NEG_BIG = -1e30
VMEM_LIMIT_BYTES = 56 * 1024 * 1024


class _Tiles(NamedTuple):
    rows: int = 1024
    cols: int = 512
    cols_wide: int = 1024
    cols_low_rank: int = 2048
    cols_rope: int = 1024
    down_rows: int = 512
    down_cols: int = 512
    norm_rows: int = 256
    latent_rows: int = 512
    attn: int = 512
    attn_heads: int = 4
    hgrn_rows: int = 512
    hgrn_heads: int = 8
    hgrn_heads_short: int = 16
    ff_rows: int = 2048
    ff_chunk: int = 512
    sample_keys: int = 1024


TILES = _Tiles()


def _params(*sem):
    return pltpu.CompilerParams(dimension_semantics=sem, vmem_limit_bytes=VMEM_LIMIT_BYTES)


def _dot(a, b):
    return jnp.dot(a, b, preferred_element_type=F32)


def _dot_nt(a, b):
    return lax.dot_general(a, b, (((1,), (1,)), ((), ())), preferred_element_type=F32)


def _dot_tn(a, b):
    return lax.dot_general(a, b, (((0,), (0,)), ((), ())), preferred_element_type=F32)


def _rms(x, g):
    ms = jnp.mean(x * x, axis=-1, keepdims=True)
    return x * lax.rsqrt(ms + EPS) * g


def _slab_rms(x, g, valid):
    outs = []
    for a in range(x.shape[1] // LANES):
        blk = x[:, a * LANES:(a + 1) * LANES]
        ms = jnp.sum(blk * blk, axis=-1, keepdims=True) * (1.0 / valid)
        outs.append(blk * lax.rsqrt(ms + EPS) * g[:, a * LANES:(a + 1) * LANES])
    return outs


def _rope_slab(y, c, sa, sb):
    return y * c + pltpu.roll(y, 96, 1) * sa + pltpu.roll(y, 32, 1) * sb


def _rmsnorm_kernel(x_ref, g_ref, o_ref):
    o_ref[...] = _rms(x_ref[...], g_ref[...]).astype(o_ref.dtype)


def _rmsnorm(x, g, tm):
    m, d = x.shape
    tm = min(tm, m)
    assert m % tm == 0
    return pl.pallas_call(
        _rmsnorm_kernel,
        grid=(m // tm,),
        in_specs=[pl.BlockSpec((tm, d), lambda i: (i, 0)), pl.BlockSpec((1, d), lambda i: (0, 0))],
        out_specs=pl.BlockSpec((tm, d), lambda i: (i, 0)),
        out_shape=jax.ShapeDtypeStruct((m, d), BF16),
        compiler_params=_params("parallel"),
        name="rmsnorm",
    )(x, g.reshape(1, d))


def _lane_fold(x):
    out = x[:, :LANES]
    for a in range(1, x.shape[1] // LANES):
        out = out + x[:, a * LANES:(a + 1) * LANES]
    return out


def _row_rinv(ssq, width):
    return lax.rsqrt(jnp.sum(ssq, axis=-1, keepdims=True) * (1.0 / width) + EPS)


def _mm_kernel(*refs, body, n_lhs, group_sizes, n_in, next_norm):
    ins, outs = refs[:n_in], refs[n_in:]
    lhs, rhs, rest = ins[:n_lhs], ins[n_lhs:2 * n_lhs], ins[2 * n_lhs:]
    accs, k = [], 0
    for gs in group_sizes:
        acc = None
        for _ in range(gs):
            d = _dot(lhs[k][...], rhs[k][...])
            acc = d if acc is None else acc + d
            k += 1
        accs.append(acc)
    if next_norm:
        *rest, gain_ref = rest
        *outs, scaled_ref, ssq_ref = outs
    res = body(*accs, *[r[...] for r in rest])
    if not isinstance(res, (tuple, list)):
        res = (res,)
    for o, r in zip(outs, res):
        o[...] = r.astype(o.dtype)
    if next_norm:
        x = res[0]
        scaled_ref[...] = (x * gain_ref[...]).astype(scaled_ref.dtype)
        @pl.when(pl.program_id(1) == 0)
        def _():
            ssq_ref[...] = jnp.zeros(ssq_ref.shape, F32)

        ssq_ref[...] += _lane_fold(x * x)


def _mm(name, body, lhs, rhs, *, tm, tn, out_dtypes, group_sizes=None, cols=(), rows=(), extras=(),
        next_norm_gain=None, n=None, rhs_col0=0, rhs_row_blocks=None):
    m = lhs[0].shape[0]
    n = n or rhs[0].shape[1]
    tm, tn = min(tm, m), min(tn, n)
    assert m % tm == 0 and n % tn == 0 and rhs_col0 % tn == 0, (name, m, n, tm, tn)
    j0 = rhs_col0 // tn
    group_sizes = tuple(group_sizes or (len(lhs),))
    rhs_row_blocks = rhs_row_blocks or (0,) * len(rhs)
    tile = pl.BlockSpec((tm, tn), lambda i, j: (i, j))
    in_specs = [pl.BlockSpec((tm, a.shape[1]), lambda i, j: (i, 0)) for a in lhs]
    in_specs += [pl.BlockSpec((a.shape[1], tn), lambda i, j, rb=rb: (rb, j + j0)) for a, rb in zip(lhs, rhs_row_blocks)]
    in_specs += [pl.BlockSpec((c.shape[0], tn), lambda i, j: (0, j)) for c in cols]
    in_specs += [tile for _ in rows]
    in_specs += [spec for _, spec in extras]
    args = list(lhs) + list(rhs) + list(cols) + list(rows) + [a for a, _ in extras]
    out_specs = [tile for _ in out_dtypes]
    out_shape = [jax.ShapeDtypeStruct((m, n), dt) for dt in out_dtypes]
    if next_norm_gain is not None:
        in_specs.append(pl.BlockSpec((1, tn), lambda i, j: (0, j)))
        args.append(next_norm_gain.reshape(1, n))
        out_specs += [tile, pl.BlockSpec((tm, LANES), lambda i, j: (i, 0))]
        out_shape += [jax.ShapeDtypeStruct((m, n), BF16), jax.ShapeDtypeStruct((m, LANES), F32)]
    kern = functools.partial(_mm_kernel, body=body, n_lhs=len(lhs), group_sizes=group_sizes, n_in=len(args),
                             next_norm=next_norm_gain is not None)
    outs = pl.pallas_call(
        kern,
        grid=(m // tm, n // tn),
        in_specs=in_specs,
        out_specs=out_specs,
        out_shape=out_shape,
        compiler_params=_params("parallel", "arbitrary"),
        name=name,
    )(*args)
    return outs if len(outs) > 1 else outs[0]


def _ssq_spec(tm):
    return pl.BlockSpec((tm, LANES), lambda i, j: (i, 0))


def _latent_kernel(h_ref, w_ref, gq_ref, gkv_ref, gkr_ref, c_ref, sa_ref, sb_ref,
                   cq_ref, ckv_ref, ckvb_ref, kr_ref, krb_ref, *, q_lora, kv_lora, rope):
    z = _dot(h_ref[...], w_ref[...])
    cq_ref[...] = _rms(z[:, :q_lora], gq_ref[...]).astype(cq_ref.dtype)
    ckv = _rms(z[:, q_lora:q_lora + kv_lora], gkv_ref[...])
    ckv_ref[...] = ckv
    ckvb_ref[...] = ckv.astype(ckvb_ref.dtype)
    (kr,) = _slab_rms(z[:, q_lora + kv_lora:], gkr_ref[...], rope)
    kr = _rope_slab(kr, c_ref[...], sa_ref[...], sb_ref[...])
    kr_ref[...] = kr[:, :rope]
    krb_ref[...] = kr.astype(krb_ref.dtype)


def _latent(h, w_lat, g_q_a, g_kv_a, g_k_rope_pad, tabs, *, q_lora, kv_lora, rope, tm):
    m, d = h.shape
    tm = min(tm, m)
    n = w_lat.shape[1]
    n_tab = tabs[0].shape[0] // tm
    row = lambda w: pl.BlockSpec((tm, w), lambda i: (i, 0))
    const = lambda r, w: pl.BlockSpec((r, w), lambda i: (0, 0))
    tab = pl.BlockSpec((tm, LANES), lambda i: (i % n_tab, 0))
    kern = functools.partial(_latent_kernel, q_lora=q_lora, kv_lora=kv_lora, rope=rope)
    return pl.pallas_call(
        kern,
        grid=(m // tm,),
        in_specs=[row(d), const(d, n), const(1, q_lora), const(1, kv_lora), const(1, LANES), tab, tab, tab],
        out_specs=[row(q_lora), row(kv_lora), row(kv_lora), row(rope), row(LANES)],
        out_shape=[jax.ShapeDtypeStruct((m, q_lora), BF16), jax.ShapeDtypeStruct((m, kv_lora), F32),
                   jax.ShapeDtypeStruct((m, kv_lora), BF16), jax.ShapeDtypeStruct((m, rope), F32),
                   jax.ShapeDtypeStruct((m, LANES), BF16)],
        compiler_params=_params("parallel"),
        name="mla_latent",
    )(h, w_lat, g_q_a.reshape(1, -1), g_kv_a.reshape(1, -1), g_k_rope_pad, *tabs)


def _attn_kernel(qn_ref, qr_ref, kn_ref, kr_ref, vt_ref, o_ref, m_scr, l_scr, acc_scr, *, tq, chunk, hb):
    qi = pl.program_id(2)
    lanes = lambda h: slice(h * LANES, (h + 1) * LANES)
    qs = [jnp.concatenate([qn_ref[:, lanes(h)], qr_ref[:, lanes(h)]], axis=-1) for h in range(hb)]
    m_scr[...] = jnp.full(m_scr.shape, NEG_BIG, F32)
    l_scr[...] = jnp.zeros(l_scr.shape, F32)
    acc_scr[...] = jnp.zeros(acc_scr.shape, F32)

    def block(kb, masked):
        off = pl.multiple_of(kb * tq, tq)
        kr = kr_ref[pl.ds(off, tq), :]
        scores = []
        for h in range(hb):
            k = jnp.concatenate([kn_ref[pl.ds(off, tq), lanes(h)], kr], axis=-1)
            s = _dot_nt(k, qs[h])
            if masked:
                kc = lax.broadcasted_iota(jnp.int32, (tq, tq), 0) // chunk
                qc = lax.broadcasted_iota(jnp.int32, (tq, tq), 1) // chunk
                s = jnp.where(kc <= qc, s, NEG_BIG)
            scores.append(s)
        probs = []
        for h in range(hb):
            m_i = m_scr[h]
            m_new = jnp.maximum(m_i, jnp.max(scores[h], axis=0, keepdims=True))
            alpha = jnp.exp2(m_i - m_new)
            p = jnp.exp2(scores[h] - m_new)
            l_scr[h] = alpha * l_scr[h] + jnp.sum(p, axis=0, keepdims=True)
            m_scr[h] = m_new
            probs.append((alpha, p.astype(BF16)))
        for h in range(hb):
            alpha, p = probs[h]
            acc_scr[h] = alpha * acc_scr[h] + _dot(vt_ref[kb, lanes(h), :], p)

    def body(kb, carry):
        block(kb, False)
        return carry

    lax.fori_loop(0, qi, body, 0)
    block(qi, True)
    for h in range(hb):
        o_ref[:, lanes(h)] = (acc_scr[h] / l_scr[h]).T.astype(o_ref.dtype)


def _attn_prompt(qn, qr, kn, krb, vt, *, batch, seq, heads, tq, hb):
    nq = seq // tq
    hb = min(hb, heads)
    assert heads % hb == 0
    qspec = pl.BlockSpec((tq, hb * LANES), lambda b, h, i: (b * nq + i, h))
    kspec = pl.BlockSpec((seq, hb * LANES), lambda b, h, i: (b, h))
    kern = functools.partial(_attn_kernel, tq=tq, chunk=CHUNK, hb=hb)
    return pl.pallas_call(
        kern,
        grid=(batch, heads // hb, nq),
        in_specs=[qspec, qspec, kspec, pl.BlockSpec((seq, LANES), lambda b, h, i: (b, 0)),
                  pl.BlockSpec((nq, hb * LANES, tq), lambda b, h, i: (b, h, 0))],
        out_specs=qspec,
        out_shape=jax.ShapeDtypeStruct(qn.shape, BF16),
        scratch_shapes=[pltpu.VMEM((hb, 1, tq), F32), pltpu.VMEM((hb, 1, tq), F32),
                        pltpu.VMEM((hb, LANES, tq), F32)],
        compiler_params=_params("parallel", "parallel", "arbitrary"),
        name="mla_attn_prompt",
    )(qn, qr, kn, krb, vt)


def _value_t_kernel(w_ref, x_ref, o_ref):
    o_ref[0] = _dot_nt(w_ref[...], x_ref[...]).astype(o_ref.dtype)


def _value_t(x, w_t, *, tk, tn):
    m, c = x.shape
    n = w_t.shape[0]
    tn = min(tn, n)
    return pl.pallas_call(
        _value_t_kernel,
        grid=(m // tk, n // tn),
        in_specs=[pl.BlockSpec((tn, c), lambda i, j: (j, 0)), pl.BlockSpec((tk, c), lambda i, j: (i, 0))],
        out_specs=pl.BlockSpec((1, tn, tk), lambda i, j: (i, j, 0)),
        out_shape=jax.ShapeDtypeStruct((m // tk, n, tk), BF16),
        compiler_params=_params("parallel", "arbitrary"),
        name="v_up_t",
    )(w_t, x)


def _absorb_kernel(x_ref, w_ref, o_ref):
    o_ref[0] = _dot(x_ref[...], w_ref[0]).astype(o_ref.dtype)


def _per_head_mm(name, x, w):
    m = x.shape[0]
    heads, _, n = w.shape
    return pl.pallas_call(
        _absorb_kernel,
        grid=(heads,),
        in_specs=[pl.BlockSpec((m, LANES), lambda h: (0, h)), pl.BlockSpec((1, LANES, n), lambda h: (h, 0, 0))],
        out_specs=pl.BlockSpec((1, m, n), lambda h: (h, 0, 0)),
        out_shape=jax.ShapeDtypeStruct((heads, m, n), BF16),
        compiler_params=_params("parallel"),
        name=name,
    )(x, w)


def _vup_kernel(x_ref, w_ref, o_ref):
    o_ref[...] = _dot(x_ref[0], w_ref[0]).astype(o_ref.dtype)


def _value_up(o_lat, w):
    heads, m, c = o_lat.shape
    return pl.pallas_call(
        _vup_kernel,
        grid=(heads,),
        in_specs=[pl.BlockSpec((1, m, c), lambda h: (h, 0, 0)), pl.BlockSpec((1, c, LANES), lambda h: (h, 0, 0))],
        out_specs=pl.BlockSpec((m, LANES), lambda h: (0, h)),
        out_shape=jax.ShapeDtypeStruct((m, heads * LANES), BF16),
        compiler_params=_params("parallel"),
        name="mla_value_up",
    )(o_lat, w)


def _attn_sample_kernel(qt_ref, qr_ref, cache_ref, kcache_ref, cnew_ref, knew_ref, wkt_ref, o_ref,
                        *, heads, t_new, past, tk, nope, rope):
    hq = heads * t_new
    c_lat = cache_ref.shape[-1]
    qt = qt_ref[...].reshape(hq, c_lat)
    qr = qr_ref[...].reshape(hq, LANES)[:, :rope]
    wkt = wkt_ref[...]

    def scores(c_b, kr_b):
        n = c_b.shape[0]
        kn_t = _dot_nt(wkt, c_b)
        ssq = jnp.sum((kn_t * kn_t).reshape(heads, nope, n), axis=1)
        rinv = lax.rsqrt(ssq * (1.0 / nope) + EPS)
        rinv = jnp.broadcast_to(rinv[:, None, :], (heads, t_new, n)).reshape(hq, n)
        return _dot_nt(qt, c_b) * rinv + _dot_nt(qr, kr_b)

    c_new = cnew_ref[...]
    s_new = scores(c_new, knew_ref[...][:, :rope])
    m_i = jnp.max(s_new, axis=-1, keepdims=True)
    p = jnp.exp2(s_new - m_i)
    l_i = jnp.sum(p, axis=-1, keepdims=True)
    acc = _dot(p.astype(BF16), c_new)
    for kt in range(past // tk):
        c_b = cache_ref[0, kt * tk:(kt + 1) * tk, :].astype(BF16)
        s = scores(c_b, kcache_ref[0, kt * tk:(kt + 1) * tk, :].astype(BF16))
        m_new = jnp.maximum(m_i, jnp.max(s, axis=-1, keepdims=True))
        alpha = jnp.exp2(m_i - m_new)
        p = jnp.exp2(s - m_new)
        l_i = alpha * l_i + jnp.sum(p, axis=-1, keepdims=True)
        acc = alpha * acc + _dot(p.astype(BF16), c_b)
        m_i = m_new
    o_ref[...] = (acc / l_i).reshape(heads, t_new, c_lat).astype(o_ref.dtype)


def _attn_sample(qt, qr_h, cache, kcache, ckv_new, kr_new, wkt, *, batch, t_new, heads, nope, rope, tk):
    past, c_lat = cache.shape[1], cache.shape[2]
    tk = min(tk, past)
    kern = functools.partial(_attn_sample_kernel, heads=heads, t_new=t_new, past=past, tk=tk, nope=nope, rope=rope)
    return pl.pallas_call(
        kern,
        grid=(batch,),
        in_specs=[pl.BlockSpec((heads, t_new, c_lat), lambda b: (0, b, 0)),
                  pl.BlockSpec((heads, t_new, LANES), lambda b: (0, b, 0)),
                  pl.BlockSpec((1, past, c_lat), lambda b: (b, 0, 0)),
                  pl.BlockSpec((1, past, rope), lambda b: (b, 0, 0)),
                  pl.BlockSpec((t_new, c_lat), lambda b: (b, 0)),
                  pl.BlockSpec((t_new, LANES), lambda b: (b, 0)),
                  pl.BlockSpec(wkt.shape, lambda b: (0, 0))],
        out_specs=pl.BlockSpec((heads, t_new, c_lat), lambda b: (0, b, 0)),
        out_shape=jax.ShapeDtypeStruct((heads, batch * t_new, c_lat), BF16),
        compiler_params=_params("parallel"),
        name="mla_attn_sample",
    )(qt, qr_h, cache, kcache, ckv_new, kr_new, wkt)


def _hgrn_tables(blk):
    t = np.arange(blk)
    levels = [blk >> (i + 1) for i in range(int(np.log2(blk)))]
    lvl = np.full((blk, blk), -1, np.int32)
    lvl[t, t] = 0
    for li, c in enumerate(levels, start=1):
        same_pair = (t[:, None] // (2 * c)) == (t[None, :] // (2 * c))
        split = (t[:, None] // c) != (t[None, :] // c)
        lvl[same_pair & split & (t[:, None] > t[None, :])] = li
    prefix = (t[None, :] <= t[:, None]).astype(np.float32)
    rows = [prefix]
    for c in (2, 1):
        mid = (t // (2 * c)) * (2 * c) + c - 1
        rows.append((t[None, :] <= mid[:, None]).astype(np.float32))
    return levels, jnp.asarray(lvl), jnp.asarray(np.concatenate(rows, axis=0), dtype=BF16)


def _hgrn_block(q_ref, k_ref, v_ref, lf_ref, g_ref, o_ref, st_scr, rows, sel, pair_masks, upper_masks, signs,
                g_out, *, blk, levels, hb):
    cols = lambda h: slice(h * LANES, (h + 1) * LANES)
    sums = []
    for h in range(hb):
        lf = lf_ref[rows, cols(h)]
        hi = lf.astype(BF16)
        lo = (lf - hi.astype(F32)).astype(BF16)
        both = _dot(sel, jnp.concatenate([hi, lo], axis=-1))
        sums.append(both[:, :LANES] + both[:, LANES:])
    intra = []
    for h in range(hb):
        b = sums[h][:blk]
        q, k = q_ref[rows, cols(h)], k_ref[rows, cols(h)]
        qf, kf = q.astype(F32), k.astype(F32)
        a = jnp.where(pair_masks[0], _dot_nt(q, k), 0.0)
        for li, c in enumerate(levels, start=1):
            if c == 2:
                ref = sums[h][blk:2 * blk]
            elif c == 1:
                ref = sums[h][2 * blk:]
            else:
                b3 = b.reshape(blk // (2 * c), 2 * c, LANES)
                ref = jnp.broadcast_to(b3[:, c - 1:c, :], b3.shape).reshape(blk, LANES)
            decay = jnp.exp2((b - ref) * signs[li - 1])
            w = (jnp.where(upper_masks[li - 1], qf, kf) * decay).astype(BF16)
            a = jnp.where(pair_masks[li], _dot_nt(w, w), a)
        intra.append(a.astype(BF16))
    for h in range(hb):
        b = sums[h][:blk]
        b_last = b[blk - 1:blk, :]
        qf, kf = q_ref[rows, cols(h)].astype(F32), k_ref[rows, cols(h)].astype(F32)
        v = v_ref[rows, cols(h)]
        s_t = st_scr[h]
        inter = _dot_nt((qf * jnp.exp2(b)).astype(BF16), s_t.astype(BF16))
        o = inter + _dot(intra[h], v)
        k_t = (kf * jnp.exp2(b_last - b)).astype(BF16)
        st_scr[h] = s_t * jnp.exp2(b_last) + _dot_tn(v, k_t)
        gf = g_ref[rows, cols(h)].astype(F32)
        o_ref[rows, cols(h)] = (_rms(o, g_out) * (gf * jax.nn.sigmoid(gf))).astype(o_ref.dtype)


def _hgrn_kernel(q_ref, k_ref, v_ref, lf_ref, g_ref, s0_ref, sel_ref, lvl_ref, gout_ref,
                 o_ref, s_out_ref, st_scr, *, blk, levels, hb, n_blk):
    tb = pl.program_id(2)

    @pl.when(tb == 0)
    def _():
        for h in range(hb):
            st_scr[h] = s0_ref[0, h].T

    sel, lvl, g_out = sel_ref[...], lvl_ref[...], gout_ref[...]
    pair_masks = [lvl == li for li in range(len(levels) + 1)]
    row = lax.broadcasted_iota(jnp.int32, (blk, LANES), 0)
    upper_masks = [(row // c) % 2 == 1 for c in levels]
    signs = [jnp.where(u, 1.0, -1.0) for u in upper_masks]

    def step(c, carry):
        rows = pl.ds(pl.multiple_of(c * blk, blk), blk)
        _hgrn_block(q_ref, k_ref, v_ref, lf_ref, g_ref, o_ref, st_scr, rows, sel, pair_masks, upper_masks, signs,
                    g_out, blk=blk, levels=levels, hb=hb)
        return carry

    lax.fori_loop(0, n_blk, step, 0)

    @pl.when(tb == pl.num_programs(2) - 1)
    def _():
        for h in range(hb):
            s_out_ref[0, h] = st_scr[h].T


def _hgrn(q, k, v, lf, gate, s0, g_out, *, batch, t_len, blk, tt, hb):
    heads = q.shape[1] // LANES
    tt, hb = min(tt, t_len), min(hb, heads)
    levels, lvl, sel = _hgrn_tables(blk)
    nt = t_len // tt
    xspec = pl.BlockSpec((tt, hb * LANES), lambda b, h, t: (b * nt + t, h))
    sspec = pl.BlockSpec((1, hb, LANES, LANES), lambda b, h, t: (b, h, 0, 0))
    const = lambda a: pl.BlockSpec(a.shape, lambda b, h, t: (0, 0))
    kern = functools.partial(_hgrn_kernel, blk=blk, levels=levels, hb=hb, n_blk=tt // blk)
    g_out = g_out.reshape(1, LANES)
    return pl.pallas_call(
        kern,
        grid=(batch, heads // hb, nt),
        in_specs=[xspec, xspec, xspec, xspec, xspec, sspec, const(sel), const(lvl), const(g_out)],
        out_specs=[xspec, sspec],
        out_shape=[jax.ShapeDtypeStruct(q.shape, BF16), jax.ShapeDtypeStruct(s0.shape, F32)],
        scratch_shapes=[pltpu.VMEM((hb, LANES, LANES), F32)],
        compiler_params=_params("parallel", "parallel", "arbitrary"),
        name="hgrn2",
    )(q, k, v, lf, gate, s0, sel, lvl, g_out)


def _silu(x):
    h = 0.5 * x
    return h + h * jnp.tanh(h)


def _ff_tile(d_ff):
    assert d_ff % LANES == 0
    return 2 * LANES if d_ff % (2 * LANES) == 0 else LANES


def _up_conv_kernel(h_ref, wg_ref, wu_ref, cwg_ref, cwu_ref, cbg_ref, cbu_ref,
                    act_ref, tailg_ref, tailu_ref, ug_scr, uu_scr, carry_scr, *, tm, cm, tiles_per_seq):
    i, j = pl.program_id(0), pl.program_id(1)
    seq_start = (i % tiles_per_seq) == 0
    halves = ((wg_ref, cwg_ref, cbg_ref, tailg_ref, ug_scr), (wu_ref, cwu_ref, cbu_ref, tailu_ref, uu_scr))
    for half, (_, _, _, _, scr) in enumerate(halves):
        @pl.when(seq_start)
        def _(scr=scr):
            scr[0:8, :] = jnp.zeros((8, scr.shape[1]), F32)

        @pl.when(jnp.logical_not(seq_start))
        def _(scr=scr, half=half):
            scr[0:8, :] = carry_scr[half, j]

    for r in range(tm // cm):
        h = h_ref[r * cm:(r + 1) * cm, :]
        conv = []
        for w_ref, cw_ref, cb_ref, _, scr in halves:
            u = _dot(h, w_ref[...])
            scr[8 + r * cm:8 + (r + 1) * cm, :] = u
            cw = cw_ref[...]
            conv.append(cb_ref[...] + cw[2:3] * u + cw[1:2] * scr[7 + r * cm:7 + (r + 1) * cm, :]
                        + cw[0:1] * scr[6 + r * cm:6 + (r + 1) * cm, :])
        act_ref[r * cm:(r + 1) * cm, :] = (_silu(conv[0]) * conv[1]).astype(act_ref.dtype)

    for half, (_, _, _, tail_ref, scr) in enumerate(halves):
        tail = scr[tm:tm + 8, :]
        carry_scr[half, j] = tail
        tail_ref[...] = tail


def _up_conv_prompt(h, w_up, conv_w, conv_b, *, seq, tm, cm):
    m, d = h.shape
    n = w_up.shape[1] // 2
    tn = _ff_tile(n)
    tm = min(tm, seq)
    cm = min(cm, tm)
    assert seq % tm == 0 and tm % cm == 0 and conv_w.shape[0] == 3
    nj = n // tn
    gate = lambda r: pl.BlockSpec((r, tn), lambda i, j: (0, j))
    up = lambda r: pl.BlockSpec((r, tn), lambda i, j: (0, j + nj))
    tail = pl.BlockSpec((8, tn), lambda i, j: (i, j))
    kern = functools.partial(_up_conv_kernel, tm=tm, cm=cm, tiles_per_seq=seq // tm)
    return pl.pallas_call(
        kern,
        grid=(m // tm, nj),
        in_specs=[pl.BlockSpec((tm, d), lambda i, j: (i, 0)), gate(d), up(d), gate(3), up(3), gate(1), up(1)],
        out_specs=[pl.BlockSpec((tm, tn), lambda i, j: (i, j)), tail, tail],
        out_shape=[jax.ShapeDtypeStruct((m, n), BF16), jax.ShapeDtypeStruct((m // tm * 8, n), F32),
                   jax.ShapeDtypeStruct((m // tm * 8, n), F32)],
        scratch_shapes=[pltpu.VMEM((tm + 8, tn), F32), pltpu.VMEM((tm + 8, tn), F32),
                        pltpu.VMEM((2, nj, 8, tn), F32)],
        compiler_params=_params("arbitrary", "arbitrary"),
        name="ffn_up_conv",
    )(h, w_up, w_up, conv_w, conv_w, conv_b, conv_b)


def _conv_sample_kernel(ug_ref, uu_ref, hg_ref, hu_ref, cwg_ref, cwu_ref, cbg_ref, cbu_ref, act_ref, *, t_len, taps):
    conv = [[None] * t_len, [None] * t_len]
    for half, (u_ref, hist_ref, cw_ref, cb_ref) in enumerate(
            ((ug_ref, hg_ref, cwg_ref, cbg_ref), (uu_ref, hu_ref, cwu_ref, cbu_ref))):
        cw, cb = cw_ref[...], cb_ref[...]
        rows = [hist_ref[:, r, :] for r in range(taps - 1)] + [u_ref[:, t, :] for t in range(t_len)]
        for t in range(t_len):
            c = cb
            for tap in range(taps):
                c = c + cw[tap:tap + 1] * rows[t + tap]
            conv[half][t] = c
    for t in range(t_len):
        act_ref[:, t, :] = _silu(conv[0][t]) * conv[1][t]


def _conv_sample(u, hist, conv_w, conv_b):
    bsz, t_len, n2 = u.shape
    n = n2 // 2
    tn = _ff_tile(n)
    nj = n // tn
    taps = conv_w.shape[0]
    gate = lambda *lead: pl.BlockSpec((*lead, tn), lambda j: (*([0] * len(lead)), j))
    up = lambda *lead: pl.BlockSpec((*lead, tn), lambda j: (*([0] * len(lead)), j + nj))
    kern = functools.partial(_conv_sample_kernel, t_len=t_len, taps=taps)
    return pl.pallas_call(
        kern,
        grid=(nj,),
        in_specs=[gate(bsz, t_len), up(bsz, t_len), gate(bsz, taps - 1), up(bsz, taps - 1),
                  gate(taps), up(taps), gate(1), up(1)],
        out_specs=gate(bsz, t_len),
        out_shape=jax.ShapeDtypeStruct((bsz, t_len, n), F32),
        compiler_params=_params("parallel"),
        name="ffn_conv_sample",
    )(u, u, hist, hist, conv_w, conv_w, conv_b, conv_b)


def _pad_cols(w, n):
    return jnp.pad(w, ((0, 0), (0, n - w.shape[1])))


def _rope_tables(pos, rope):
    half = rope // 2
    inv = 1.0 / (ROPE_THETA ** (jnp.arange(half, dtype=F32) / half))
    ang = pos.astype(F32)[:, None] * inv[None, :]
    cos, sin = jnp.cos(ang), jnp.sin(ang)
    z = jnp.zeros_like(cos)
    zz = jnp.zeros((pos.shape[0], LANES - rope), F32)
    c = jnp.concatenate([cos, cos, zz], axis=-1)
    sa = jnp.concatenate([-sin, z, zz], axis=-1)
    sb = jnp.concatenate([z, sin, zz], axis=-1)
    return c, sa, sb


def _layer_weights(i, lb, g_attn_norm, w_in, g_q_a, w_uq, g_kv_a, w_uk, w_uv, g_q_nope, g_q_rope, g_k_nope,
                   g_k_rope, g_hg_out, w_o, g_ffn_norm, w_up, conv_w, conv_b, w_down, g_ple_norm, w_ple_gate,
                   w_ple):
    q_lora, kv_lora = g_q_a.shape[1], g_kv_a.shape[1]
    nope, rope = g_q_nope.shape[1], g_q_rope.shape[1]
    heads, v_dim = w_uk.shape[2], w_uv.shape[3]
    hg_width = lb.shape[0]
    d_ff = w_down.shape[1]
    assert nope == LANES and v_dim == LANES and g_hg_out.shape[1] == LANES and 2 * rope == LANES
    scale = float((nope + rope) ** -0.5) * float(np.log2(np.e))
    o3 = q_lora + kv_lora + rope
    w_in_i = w_in[i]
    wd = {"dims": dict(q_lora=q_lora, kv_lora=kv_lora, nope=nope, rope=rope, heads=heads, hg_width=hg_width,
                       d_ff=d_ff)}
    wd["g_attn"], wd["g_ffn"], wd["g_ple"] = g_attn_norm[i], g_ffn_norm[i], g_ple_norm[i]
    wd["w_lat"] = _pad_cols(w_in_i[:, :o3], o3 + LANES - rope).astype(BF16)
    wd["g_q_a"], wd["g_kv_a"] = g_q_a[i], g_kv_a[i]
    wd["g_k_rope_pad"] = jnp.pad(g_k_rope[i], (0, LANES - rope)).reshape(1, LANES)
    wd["w_hgrn"] = w_in_i[:, o3:].astype(BF16)
    wd["lb"] = lb.reshape(1, hg_width)
    wq = w_uq[i].reshape(q_lora, heads, nope + rope)
    wd["w_qn"] = wq[:, :, :nope].reshape(q_lora, heads * nope).astype(BF16)
    half = rope // 2
    slab = lambda a, b: jnp.pad(jnp.concatenate([a, b], axis=-1), [(0, 0)] * (a.ndim - 1) + [(0, LANES - rope)])
    flat = lambda w: w.reshape(q_lora, heads * LANES).astype(BF16)
    wr1, wr2 = wq[:, :, nope:nope + half], wq[:, :, nope + half:]
    wd["w_qr"], wd["w_qr_swap"] = flat(slab(wr1, wr2)), flat(slab(wr2, wr1))
    wd["g_qn"] = (jnp.tile(g_q_nope[i], heads) * scale).reshape(1, heads * nope)
    gr = g_q_rope[i] * scale
    wd["g_qr"] = jnp.tile(slab(gr[:half], gr[half:]), heads).reshape(1, heads * LANES)
    wd["g_qr_swap"] = jnp.tile(slab(gr[half:], gr[:half]), heads).reshape(1, heads * LANES)
    wd["w_uk"] = w_uk[i].reshape(kv_lora, heads * nope).astype(BF16)
    wd["g_kn"] = jnp.tile(g_k_nope[i], heads).reshape(1, heads * nope)
    wd["w_uv_t"] = jnp.transpose(w_uv[i], (1, 2, 0)).reshape(heads * v_dim, kv_lora).astype(BF16)
    wd["w_absorb"] = jnp.transpose(w_uk[i] * g_k_nope[i][None, None, :], (1, 2, 0)).astype(BF16)
    wd["w_uk_t"] = jnp.transpose(w_uk[i], (1, 2, 0)).reshape(heads * nope, kv_lora).astype(BF16)
    wd["w_uv_h"] = jnp.transpose(w_uv[i], (1, 0, 2)).astype(BF16)
    wd["g_hg_out"] = g_hg_out[i]
    mla_width = heads * v_dim
    assert w_o.shape[1] == 2 * mla_width
    wd["w_o"] = w_o[i].astype(BF16)
    wd["w_up"], wd["conv_w"], wd["conv_b"] = w_up[i].astype(BF16), conv_w[i], conv_b[i].reshape(1, 2 * d_ff)
    wd["w_down"] = w_down[i].astype(BF16)
    wd["w_ple_gate"], wd["w_ple"] = w_ple_gate[i].astype(BF16), w_ple[i].astype(BF16)
    return wd


def _mixer_inputs(x, wd, tabs, *, tm):
    d = wd["dims"]
    h = _rmsnorm(x, wd["g_attn"], TILES.norm_rows)
    cq, ckv, ckv_b, kr, kr_b = _latent(h, wd["w_lat"], wd["g_q_a"], wd["g_kv_a"], wd["g_k_rope_pad"], tabs,
                                      q_lora=d["q_lora"], kv_lora=d["kv_lora"], rope=d["rope"],
                                      tm=TILES.latent_rows)
    m = x.shape[0]
    n_tab = tabs[0].shape[0] // min(tm, m)
    tab_spec = pl.BlockSpec((min(tm, m), LANES), lambda i, j: (i % n_tab, 0))

    def qn_body(acc, g):
        return jnp.concatenate(_slab_rms(acc, g, d["nope"]), axis=-1)

    def qr_body(acc, acc_swap, g, g_swap, c, s):
        outs = []
        for a in range(acc.shape[1] // LANES):
            sl = slice(a * LANES, (a + 1) * LANES)
            x, xs = acc[:, sl], acc_swap[:, sl]
            rinv = lax.rsqrt(jnp.sum(x * x, axis=-1, keepdims=True) * (1.0 / d["rope"]) + EPS)
            outs.append((x * rinv * g[:, sl]) * c + (xs * rinv * g_swap[:, sl]) * s)
        return jnp.concatenate(outs, axis=-1)

    c_tab, sa_tab, sb_tab = tabs
    qn = _mm("q_nope", qn_body, [cq], [wd["w_qn"]], tm=tm, tn=TILES.cols_low_rank, out_dtypes=[BF16],
             cols=[wd["g_qn"]])
    qr = _mm("q_rope", qr_body, [cq, cq], [wd["w_qr"], wd["w_qr_swap"]], tm=tm, tn=TILES.cols_rope, out_dtypes=[BF16],
             group_sizes=(1, 1), cols=[wd["g_qr"], wd["g_qr_swap"]],
             extras=[(c_tab, tab_spec), (sa_tab + sb_tab, tab_spec)])

    def f_body(acc, lb):
        s = jax.nn.sigmoid(acc)
        log2_f = jnp.log2(lb + (1.0 - lb) * s)
        return log2_f, (1.0 - lb) * (1.0 - s)

    ident = lambda acc: acc
    hw = d["hg_width"]
    proj = functools.partial(_mm, lhs=[h], rhs=[wd["w_hgrn"]], tm=tm, tn=TILES.cols_wide, n=hw)
    hq = proj("hgrn_q", ident, out_dtypes=[BF16], rhs_col0=0)
    lf, hk = proj("hgrn_f", f_body, out_dtypes=[F32, BF16], cols=[wd["lb"]], rhs_col0=hw)
    hv = proj("hgrn_v", ident, out_dtypes=[BF16], rhs_col0=2 * hw)
    hg = proj("hgrn_g", ident, out_dtypes=[BF16], rhs_col0=3 * hw)
    return (qn, qr, ckv, ckv_b, kr, kr_b), (hq, hk, hv, lf, hg)


def _out_proj(x, o_mla, o_hg, wd, *, tm):
    return _mm("out_proj", lambda acc, res: res + acc, [o_mla, o_hg], [wd["w_o"], wd["w_o"]], rhs_row_blocks=(0, 1),
               tm=tm, tn=TILES.cols_wide, out_dtypes=[F32], rows=[x])


def _down_ple(x, act, p, wd, *, tm):
    x, xg, ssq = _mm("ffn_down", lambda acc, res: res + acc, [act], [wd["w_down"]], tm=min(tm, TILES.down_rows),
                     tn=TILES.down_cols, out_dtypes=[F32], rows=[x], next_norm_gain=wd["g_ple"])
    width = x.shape[1]

    def ple_body(gate, emb, res, ssq_blk):
        return res + emb * jax.nn.sigmoid(gate * _row_rinv(ssq_blk, width))

    return _mm("ple", ple_body, [xg, p.astype(BF16)], [wd["w_ple_gate"], wd["w_ple"]], tm=tm, tn=TILES.cols,
               out_dtypes=[F32], group_sizes=(1, 1), rows=[x], extras=[(ssq, _ssq_spec(min(tm, x.shape[0])))])


def _prompt_layer(x, p, wd, *, batch, seq):
    d = wd["dims"]
    tm = min(TILES.rows, seq)
    assert seq % tm == 0
    tabs = _rope_tables(jnp.arange(seq), d["rope"])
    (qn, qr, ckv, ckv_b, kr, kr_b), (hq, hk, hv, lf, hg) = _mixer_inputs(x, wd, tabs, tm=tm)

    def kn_body(acc, g):
        return jnp.concatenate(_slab_rms(acc, g, d["nope"]), axis=-1)

    kn = _mm("k_nope", kn_body, [ckv_b], [wd["w_uk"]], tm=tm, tn=TILES.cols_low_rank, out_dtypes=[BF16],
             cols=[wd["g_kn"]])
    tq = min(TILES.attn, seq)
    vt = _value_t(ckv_b, wd["w_uv_t"], tk=tq, tn=TILES.cols_low_rank)
    o_mla = _attn_prompt(qn, qr, kn, kr_b, vt, batch=batch, seq=seq, heads=d["heads"], tq=tq, hb=TILES.attn_heads)

    hg_heads = d["hg_width"] // LANES
    s0 = jnp.zeros((batch, hg_heads, LANES, LANES), F32)
    o_hg, s_fin = _hgrn(hq, hk, hv, lf, hg, s0, wd["g_hg_out"], batch=batch, t_len=seq, blk=CHUNK,
                        tt=TILES.hgrn_rows, hb=TILES.hgrn_heads)

    x = _out_proj(x, o_mla, o_hg, wd, tm=tm)
    h = _rmsnorm(x, wd["g_ffn"], TILES.norm_rows)
    tm_up = min(TILES.ff_rows, seq)
    act, tail_g, tail_u = _up_conv_prompt(h, wd["w_up"], wd["conv_w"], wd["conv_b"], seq=seq, tm=tm_up,
                                          cm=TILES.ff_chunk)
    last = lambda t: t.reshape(batch, seq // tm_up, 8, -1)[:, -1, 6:, :]
    conv_state = jnp.concatenate([last(tail_g), last(tail_u)], axis=-1)
    x = _down_ple(x, act, p, wd, tm=tm)
    return x, ckv.reshape(batch, seq, -1), kr.reshape(batch, seq, -1), s_fin, conv_state


def _sample_layer(x, p, cache_ckv, cache_kr, state_hgrn, hist, wd, *, batch, t_new):
    d = wd["dims"]
    m = batch * t_new
    past = cache_ckv.shape[1]
    tabs = _rope_tables(jnp.tile(past + jnp.arange(t_new), batch), d["rope"])
    (qn, qr, ckv, ckv_b, kr, kr_b), (hq, hk, hv, lf, hg) = _mixer_inputs(x, wd, tabs, tm=m)

    qt = _per_head_mm("mla_absorb_q", qn, wd["w_absorb"])
    qr_h = jnp.transpose(qr.reshape(m, d["heads"], LANES), (1, 0, 2))
    o_lat = _attn_sample(qt, qr_h, cache_ckv, cache_kr, ckv_b, kr_b, wd["w_uk_t"], batch=batch, t_new=t_new,
                         heads=d["heads"], nope=d["nope"], rope=d["rope"], tk=TILES.sample_keys)
    o_mla = _value_up(o_lat, wd["w_uv_h"])

    o_hg, s_new = _hgrn(hq, hk, hv, lf, hg, state_hgrn, wd["g_hg_out"], batch=batch, t_len=t_new, blk=t_new,
                        tt=t_new, hb=TILES.hgrn_heads_short)

    x = _out_proj(x, o_mla, o_hg, wd, tm=m)
    h = _rmsnorm(x, wd["g_ffn"], TILES.norm_rows)
    d_ff = d["d_ff"]
    u = _mm("ffn_up_sample", lambda acc: acc, [h], [wd["w_up"]], tm=m, tn=2 * _ff_tile(d_ff), out_dtypes=[F32])
    u3 = u.reshape(batch, t_new, 2 * d_ff)
    act = _conv_sample(u3, hist, wd["conv_w"], wd["conv_b"])
    taps = wd["conv_w"].shape[0]
    conv_state = jnp.concatenate([hist, u3], axis=1)[:, -(taps - 1):]
    x = _down_ple(x, act.reshape(m, d_ff).astype(BF16), p, wd, tm=m)
    return x, ckv.reshape(batch, t_new, -1), kr.reshape(batch, t_new, -1), s_new, conv_state


def kernel(x_prompt, x_sample, cache_ckv, cache_k_rope, state_hgrn, state_ffn_conv, p_prompt, p_sample,
           g_attn_norm, w_in, g_q_a, w_uq, g_kv_a, w_uk, w_uv, g_q_nope, g_q_rope, g_k_nope, g_k_rope,
           lb_logits, g_hg_out, w_o, g_ffn_norm, w_up, conv_w, conv_b, w_down, g_ple_norm, w_ple_gate, w_ple):
    depth = w_in.shape[0]
    batch, seq, d_model = x_prompt.shape
    dec_batch, t_new, _ = x_sample.shape
    assert t_new >= conv_w.shape[1] - 1
    lb_all = jnp.cumsum(jax.nn.softmax(lb_logits.astype(F32), axis=0), axis=0)
    xp = x_prompt.reshape(batch * seq, d_model)
    xs = x_sample.reshape(dec_batch * t_new, d_model)
    outs = [[] for _ in range(8)]
    for i in range(depth):
        wd = _layer_weights(i, lb_all[i], g_attn_norm, w_in, g_q_a, w_uq, g_kv_a, w_uk, w_uv, g_q_nope, g_q_rope,
                            g_k_nope, g_k_rope, g_hg_out, w_o, g_ffn_norm, w_up, conv_w, conv_b, w_down,
                            g_ple_norm, w_ple_gate, w_ple)
        xp, *new_p = _prompt_layer(xp, p_prompt[i].reshape(batch * seq, -1), wd, batch=batch, seq=seq)
        xs, *new_s = _sample_layer(xs, p_sample[i].reshape(dec_batch * t_new, -1), cache_ckv[i], cache_k_rope[i],
                                   state_hgrn[i], state_ffn_conv[i], wd, batch=dec_batch, t_new=t_new)
        for lst, val in zip(outs, new_p + new_s):
            lst.append(val)
    return (xp.reshape(batch, seq, d_model), xs.reshape(dec_batch, t_new, d_model), *[jnp.stack(o) for o in outs])
```

```python
import functools
from typing import NamedTuple

import numpy as np
import jax
import jax.numpy as jnp
from jax import lax
from jax.experimental import pallas as pl
from jax.experimental.pallas import tpu as pltpu

F32 = jnp.float32
BF16 = jnp.bfloat16

EPS = 1e-6
CHUNK = 64
ROPE_THETA = 10000.0
LANES = 128
NEG_BIG = -1e30
VMEM_LIMIT_BYTES = 56 * 1024 * 1024


class _Tiles(NamedTuple):
    rows: int = 1024
    cols: int = 512
    cols_wide: int = 1024
    cols_low_rank: int = 2048
    cols_rope: int = 2048
    down_rows: int = 512
    down_cols: int = 512
    norm_rows: int = 512
    latent_rows: int = 512
    attn: int = 512
    attn_heads: int = 4
    hgrn_rows: int = 512
    hgrn_heads: int = 8
    hgrn_heads_short: int = 16
    ff_rows: int = 2048
    ff_chunk: int = 512
    sample_keys: int = 1024


TILES = _Tiles()


def _params(*sem):
    return pltpu.CompilerParams(dimension_semantics=sem, vmem_limit_bytes=VMEM_LIMIT_BYTES)


def _dot(a, b):
    return jnp.dot(a, b, preferred_element_type=F32)


def _dot_nt(a, b):
    return lax.dot_general(a, b, (((1,), (1,)), ((), ())), preferred_element_type=F32)


def _dot_tn(a, b):
    return lax.dot_general(a, b, (((0,), (0,)), ((), ())), preferred_element_type=F32)


def _rms(x, g):
    ms = jnp.mean(x * x, axis=-1, keepdims=True)
    return x * lax.rsqrt(ms + EPS) * g


def _slab_rms(x, g, valid):
    outs = []
    for a in range(x.shape[1] // LANES):
        blk = x[:, a * LANES:(a + 1) * LANES]
        ms = jnp.sum(blk * blk, axis=-1, keepdims=True) * (1.0 / valid)
        outs.append(blk * lax.rsqrt(ms + EPS) * g[:, a * LANES:(a + 1) * LANES])
    return outs


def _rope_slab(y, c, sa, sb):
    return y * c + pltpu.roll(y, 96, 1) * sa + pltpu.roll(y, 32, 1) * sb


def _rmsnorm_kernel(x_ref, g_ref, o_ref):
    o_ref[...] = _rms(x_ref[...], g_ref[...]).astype(o_ref.dtype)


def _rmsnorm(x, g, tm):
    m, d = x.shape
    tm = min(tm, m)
    assert m % tm == 0
    return pl.pallas_call(
        _rmsnorm_kernel,
        grid=(m // tm,),
        in_specs=[pl.BlockSpec((tm, d), lambda i: (i, 0)), pl.BlockSpec((1, d), lambda i: (0, 0))],
        out_specs=pl.BlockSpec((tm, d), lambda i: (i, 0)),
        out_shape=jax.ShapeDtypeStruct((m, d), BF16),
        compiler_params=_params("parallel"),
        name="rmsnorm",
    )(x, g.reshape(1, d))


def _lane_fold(x):
    out = x[:, :LANES]
    for a in range(1, x.shape[1] // LANES):
        out = out + x[:, a * LANES:(a + 1) * LANES]
    return out


def _row_rinv(ssq, width):
    return lax.rsqrt(jnp.sum(ssq, axis=-1, keepdims=True) * (1.0 / width) + EPS)


def _mm_kernel(*refs, body, n_lhs, group_sizes, n_in, next_norm):
    ins, outs = refs[:n_in], refs[n_in:]
    lhs, rhs, rest = ins[:n_lhs], ins[n_lhs:2 * n_lhs], ins[2 * n_lhs:]
    accs, k = [], 0
    for gs in group_sizes:
        acc = None
        for _ in range(gs):
            d = _dot(lhs[k][...], rhs[k][...])
            acc = d if acc is None else acc + d
            k += 1
        accs.append(acc)
    if next_norm:
        *rest, gain_ref = rest
        *outs, scaled_ref, ssq_ref = outs
    res = body(*accs, *[r[...] for r in rest])
    if not isinstance(res, (tuple, list)):
        res = (res,)
    for o, r in zip(outs, res):
        o[...] = r.astype(o.dtype)
    if next_norm:
        x = res[0]
        scaled_ref[...] = (x * gain_ref[...]).astype(scaled_ref.dtype)
        @pl.when(pl.program_id(1) == 0)
        def _():
            ssq_ref[...] = jnp.zeros(ssq_ref.shape, F32)

        ssq_ref[...] += _lane_fold(x * x)


def _mm(name, body, lhs, rhs, *, tm, tn, out_dtypes, group_sizes=None, cols=(), rows=(), extras=(),
        next_norm_gain=None, n=None, rhs_col0=0, rhs_row_blocks=None):
    m = lhs[0].shape[0]
    n = n or rhs[0].shape[1]
    tm, tn = min(tm, m), min(tn, n)
    assert m % tm == 0 and n % tn == 0 and rhs_col0 % tn == 0, (name, m, n, tm, tn)
    j0 = rhs_col0 // tn
    group_sizes = tuple(group_sizes or (len(lhs),))
    rhs_row_blocks = rhs_row_blocks or (0,) * len(rhs)
    tile = pl.BlockSpec((tm, tn), lambda i, j: (i, j))
    in_specs = [pl.BlockSpec((tm, a.shape[1]), lambda i, j: (i, 0)) for a in lhs]
    in_specs += [pl.BlockSpec((a.shape[1], tn), lambda i, j, rb=rb: (rb, j + j0)) for a, rb in zip(lhs, rhs_row_blocks)]
    in_specs += [pl.BlockSpec((c.shape[0], tn), lambda i, j: (0, j)) for c in cols]
    in_specs += [tile for _ in rows]
    in_specs += [spec for _, spec in extras]
    args = list(lhs) + list(rhs) + list(cols) + list(rows) + [a for a, _ in extras]
    out_specs = [tile for _ in out_dtypes]
    out_shape = [jax.ShapeDtypeStruct((m, n), dt) for dt in out_dtypes]
    if next_norm_gain is not None:
        in_specs.append(pl.BlockSpec((1, tn), lambda i, j: (0, j)))
        args.append(next_norm_gain.reshape(1, n))
        out_specs += [tile, pl.BlockSpec((tm, LANES), lambda i, j: (i, 0))]
        out_shape += [jax.ShapeDtypeStruct((m, n), BF16), jax.ShapeDtypeStruct((m, LANES), F32)]
    kern = functools.partial(_mm_kernel, body=body, n_lhs=len(lhs), group_sizes=group_sizes, n_in=len(args),
                             next_norm=next_norm_gain is not None)
    outs = pl.pallas_call(
        kern,
        grid=(m // tm, n // tn),
        in_specs=in_specs,
        out_specs=out_specs,
        out_shape=out_shape,
        compiler_params=_params("parallel", "arbitrary"),
        name=name,
    )(*args)
    return outs if len(outs) > 1 else outs[0]


def _ssq_spec(tm):
    return pl.BlockSpec((tm, LANES), lambda i, j: (i, 0))


def _latent_kernel(h_ref, w_ref, gq_ref, gkv_ref, gkr_ref, c_ref, sa_ref, sb_ref,
                   cq_ref, ckv_ref, ckvb_ref, kr_ref, krb_ref, *, q_lora, kv_lora, rope):
    z = _dot(h_ref[...], w_ref[...])
    cq_ref[...] = _rms(z[:, :q_lora], gq_ref[...]).astype(cq_ref.dtype)
    ckv = _rms(z[:, q_lora:q_lora + kv_lora], gkv_ref[...])
    ckv_ref[...] = ckv
    ckvb_ref[...] = ckv.astype(ckvb_ref.dtype)
    (kr,) = _slab_rms(z[:, q_lora + kv_lora:], gkr_ref[...], rope)
    kr = _rope_slab(kr, c_ref[...], sa_ref[...], sb_ref[...])
    kr_ref[...] = kr[:, :rope]
    krb_ref[...] = kr.astype(krb_ref.dtype)


def _latent(h, w_lat, g_q_a, g_kv_a, g_k_rope_pad, tabs, *, q_lora, kv_lora, rope, tm):
    m, d = h.shape
    tm = min(tm, m)
    n = w_lat.shape[1]
    n_tab = tabs[0].shape[0] // tm
    row = lambda w: pl.BlockSpec((tm, w), lambda i: (i, 0))
    const = lambda r, w: pl.BlockSpec((r, w), lambda i: (0, 0))
    tab = pl.BlockSpec((tm, LANES), lambda i: (i % n_tab, 0))
    kern = functools.partial(_latent_kernel, q_lora=q_lora, kv_lora=kv_lora, rope=rope)
    return pl.pallas_call(
        kern,
        grid=(m // tm,),
        in_specs=[row(d), const(d, n), const(1, q_lora), const(1, kv_lora), const(1, LANES), tab, tab, tab],
        out_specs=[row(q_lora), row(kv_lora), row(kv_lora), row(rope), row(LANES)],
        out_shape=[jax.ShapeDtypeStruct((m, q_lora), BF16), jax.ShapeDtypeStruct((m, kv_lora), F32),
                   jax.ShapeDtypeStruct((m, kv_lora), BF16), jax.ShapeDtypeStruct((m, rope), F32),
                   jax.ShapeDtypeStruct((m, LANES), BF16)],
        compiler_params=_params("parallel"),
        name="mla_latent",
    )(h, w_lat, g_q_a.reshape(1, -1), g_kv_a.reshape(1, -1), g_k_rope_pad, *tabs)


def _attn_kernel(qn_ref, qr_ref, kn_ref, kr_ref, vt_ref, o_ref, m_scr, l_scr, acc_scr, *, tq, chunk, hb):
    qi = pl.program_id(2)
    lanes = lambda h: slice(h * LANES, (h + 1) * LANES)
    qs = [jnp.concatenate([qn_ref[:, lanes(h)], qr_ref[:, lanes(h)]], axis=-1) for h in range(hb)]
    m_scr[...] = jnp.full(m_scr.shape, NEG_BIG, F32)
    l_scr[...] = jnp.zeros(l_scr.shape, F32)
    acc_scr[...] = jnp.zeros(acc_scr.shape, F32)

    def block(kb, masked):
        off = pl.multiple_of(kb * tq, tq)
        kr = kr_ref[pl.ds(off, tq), :]
        scores = []
        for h in range(hb):
            k = jnp.concatenate([kn_ref[pl.ds(off, tq), lanes(h)], kr], axis=-1)
            s = _dot_nt(k, qs[h])
            if masked:
                kc = lax.broadcasted_iota(jnp.int32, (tq, tq), 0) // chunk
                qc = lax.broadcasted_iota(jnp.int32, (tq, tq), 1) // chunk
                s = jnp.where(kc <= qc, s, NEG_BIG)
            scores.append(s)
        probs = []
        for h in range(hb):
            m_i = m_scr[h]
            m_new = jnp.maximum(m_i, jnp.max(scores[h], axis=0, keepdims=True))
            alpha = jnp.exp2(m_i - m_new)
            p = jnp.exp2(scores[h] - m_new)
            l_scr[h] = alpha * l_scr[h] + jnp.sum(p, axis=0, keepdims=True)
            m_scr[h] = m_new
            probs.append((alpha, p.astype(BF16)))
        for h in range(hb):
            alpha, p = probs[h]
            acc_scr[h] = alpha * acc_scr[h] + _dot(vt_ref[kb, lanes(h), :], p)

    def body(kb, carry):
        block(kb, False)
        return carry

    lax.fori_loop(0, qi, body, 0)
    block(qi, True)
    for h in range(hb):
        o_ref[:, lanes(h)] = (acc_scr[h] / l_scr[h]).T.astype(o_ref.dtype)


def _attn_prompt(qn, qr, kn, krb, vt, *, batch, seq, heads, tq, hb):
    nq = seq // tq
    hb = min(hb, heads)
    assert heads % hb == 0
    qspec = pl.BlockSpec((tq, hb * LANES), lambda b, h, i: (b * nq + i, h))
    kspec = pl.BlockSpec((seq, hb * LANES), lambda b, h, i: (b, h))
    kern = functools.partial(_attn_kernel, tq=tq, chunk=CHUNK, hb=hb)
    return pl.pallas_call(
        kern,
        grid=(batch, heads // hb, nq),
        in_specs=[qspec, qspec, kspec, pl.BlockSpec((seq, LANES), lambda b, h, i: (b, 0)),
                  pl.BlockSpec((nq, hb * LANES, tq), lambda b, h, i: (b, h, 0))],
        out_specs=qspec,
        out_shape=jax.ShapeDtypeStruct(qn.shape, BF16),
        scratch_shapes=[pltpu.VMEM((hb, 1, tq), F32), pltpu.VMEM((hb, 1, tq), F32),
                        pltpu.VMEM((hb, LANES, tq), F32)],
        compiler_params=_params("parallel", "parallel", "arbitrary"),
        name="mla_attn_prompt",
    )(qn, qr, kn, krb, vt)


def _value_t_kernel(w_ref, x_ref, o_ref):
    o_ref[0] = _dot_nt(w_ref[...], x_ref[...]).astype(o_ref.dtype)


def _value_t(x, w_t, *, tk, tn):
    m, c = x.shape
    n = w_t.shape[0]
    tn = min(tn, n)
    return pl.pallas_call(
        _value_t_kernel,
        grid=(m // tk, n // tn),
        in_specs=[pl.BlockSpec((tn, c), lambda i, j: (j, 0)), pl.BlockSpec((tk, c), lambda i, j: (i, 0))],
        out_specs=pl.BlockSpec((1, tn, tk), lambda i, j: (i, j, 0)),
        out_shape=jax.ShapeDtypeStruct((m // tk, n, tk), BF16),
        compiler_params=_params("parallel", "arbitrary"),
        name="v_up_t",
    )(w_t, x)


def _absorb_kernel(x_ref, w_ref, o_ref):
    o_ref[0] = _dot(x_ref[...], w_ref[0]).astype(o_ref.dtype)


def _per_head_mm(name, x, w):
    m = x.shape[0]
    heads, _, n = w.shape
    return pl.pallas_call(
        _absorb_kernel,
        grid=(heads,),
        in_specs=[pl.BlockSpec((m, LANES), lambda h: (0, h)), pl.BlockSpec((1, LANES, n), lambda h: (h, 0, 0))],
        out_specs=pl.BlockSpec((1, m, n), lambda h: (h, 0, 0)),
        out_shape=jax.ShapeDtypeStruct((heads, m, n), BF16),
        compiler_params=_params("parallel"),
        name=name,
    )(x, w)


def _vup_kernel(x_ref, w_ref, o_ref):
    o_ref[...] = _dot(x_ref[0], w_ref[0]).astype(o_ref.dtype)


def _value_up(o_lat, w):
    heads, m, c = o_lat.shape
    return pl.pallas_call(
        _vup_kernel,
        grid=(heads,),
        in_specs=[pl.BlockSpec((1, m, c), lambda h: (h, 0, 0)), pl.BlockSpec((1, c, LANES), lambda h: (h, 0, 0))],
        out_specs=pl.BlockSpec((m, LANES), lambda h: (0, h)),
        out_shape=jax.ShapeDtypeStruct((m, heads * LANES), BF16),
        compiler_params=_params("parallel"),
        name="mla_value_up",
    )(o_lat, w)


def _attn_sample_kernel(qt_ref, qr_ref, cache_ref, kcache_ref, cnew_ref, knew_ref, wkt_ref, o_ref,
                        *, heads, t_new, past, tk, nope, rope):
    hq = heads * t_new
    c_lat = cache_ref.shape[-1]
    qt = qt_ref[...].reshape(hq, c_lat)
    qr = qr_ref[...].reshape(hq, LANES)[:, :rope]
    wkt = wkt_ref[...]

    def scores(c_b, kr_b):
        n = c_b.shape[0]
        kn_t = _dot_nt(wkt, c_b)
        ssq = jnp.sum((kn_t * kn_t).reshape(heads, nope, n), axis=1)
        rinv = lax.rsqrt(ssq * (1.0 / nope) + EPS)
        rinv = jnp.broadcast_to(rinv[:, None, :], (heads, t_new, n)).reshape(hq, n)
        return _dot_nt(qt, c_b) * rinv + _dot_nt(qr, kr_b)

    c_new = cnew_ref[...]
    s_new = scores(c_new, knew_ref[...][:, :rope])
    m_i = jnp.max(s_new, axis=-1, keepdims=True)
    p = jnp.exp2(s_new - m_i)
    l_i = jnp.sum(p, axis=-1, keepdims=True)
    acc = _dot(p.astype(BF16), c_new)
    for kt in range(past // tk):
        c_b = cache_ref[0, kt * tk:(kt + 1) * tk, :].astype(BF16)
        s = scores(c_b, kcache_ref[0, kt * tk:(kt + 1) * tk, :].astype(BF16))
        m_new = jnp.maximum(m_i, jnp.max(s, axis=-1, keepdims=True))
        alpha = jnp.exp2(m_i - m_new)
        p = jnp.exp2(s - m_new)
        l_i = alpha * l_i + jnp.sum(p, axis=-1, keepdims=True)
        acc = alpha * acc + _dot(p.astype(BF16), c_b)
        m_i = m_new
    o_ref[...] = (acc / l_i).reshape(heads, t_new, c_lat).astype(o_ref.dtype)


def _attn_sample(qt, qr_h, cache, kcache, ckv_new, kr_new, wkt, *, batch, t_new, heads, nope, rope, tk):
    past, c_lat = cache.shape[1], cache.shape[2]
    tk = min(tk, past)
    kern = functools.partial(_attn_sample_kernel, heads=heads, t_new=t_new, past=past, tk=tk, nope=nope, rope=rope)
    return pl.pallas_call(
        kern,
        grid=(batch,),
        in_specs=[pl.BlockSpec((heads, t_new, c_lat), lambda b: (0, b, 0)),
                  pl.BlockSpec((heads, t_new, LANES), lambda b: (0, b, 0)),
                  pl.BlockSpec((1, past, c_lat), lambda b: (b, 0, 0)),
                  pl.BlockSpec((1, past, rope), lambda b: (b, 0, 0)),
                  pl.BlockSpec((t_new, c_lat), lambda b: (b, 0)),
                  pl.BlockSpec((t_new, LANES), lambda b: (b, 0)),
                  pl.BlockSpec(wkt.shape, lambda b: (0, 0))],
        out_specs=pl.BlockSpec((heads, t_new, c_lat), lambda b: (0, b, 0)),
        out_shape=jax.ShapeDtypeStruct((heads, batch * t_new, c_lat), BF16),
        compiler_params=_params("parallel"),
        name="mla_attn_sample",
    )(qt, qr_h, cache, kcache, ckv_new, kr_new, wkt)


def _hgrn_tables(blk):
    t = np.arange(blk)
    levels = [blk >> (i + 1) for i in range(int(np.log2(blk)))]
    lvl = np.full((blk, blk), -1, np.int32)
    lvl[t, t] = 0
    for li, c in enumerate(levels, start=1):
        same_pair = (t[:, None] // (2 * c)) == (t[None, :] // (2 * c))
        split = (t[:, None] // c) != (t[None, :] // c)
        lvl[same_pair & split & (t[:, None] > t[None, :])] = li
    prefix = (t[None, :] <= t[:, None]).astype(np.float32)
    rows = [prefix]
    for c in (2, 1):
        mid = (t // (2 * c)) * (2 * c) + c - 1
        rows.append((t[None, :] <= mid[:, None]).astype(np.float32))
    return levels, jnp.asarray(lvl), jnp.asarray(np.concatenate(rows, axis=0), dtype=BF16)


def _hgrn_block(q_ref, k_ref, v_ref, lf_ref, g_ref, o_ref, st_scr, rows, sel, pair_masks, upper_masks, signs,
                g_out, *, blk, levels, hb):
    cols = lambda h: slice(h * LANES, (h + 1) * LANES)
    sums = []
    for h in range(hb):
        lf = lf_ref[rows, cols(h)]
        hi = lf.astype(BF16)
        lo = (lf - hi.astype(F32)).astype(BF16)
        both = _dot(sel, jnp.concatenate([hi, lo], axis=-1))
        sums.append(both[:, :LANES] + both[:, LANES:])
    intra = []
    for h in range(hb):
        b = sums[h][:blk]
        q, k = q_ref[rows, cols(h)], k_ref[rows, cols(h)]
        qf, kf = q.astype(F32), k.astype(F32)
        a = jnp.where(pair_masks[0], _dot_nt(q, k), 0.0)
        for li, c in enumerate(levels, start=1):
            if c == 2:
                ref = sums[h][blk:2 * blk]
            elif c == 1:
                ref = sums[h][2 * blk:]
            else:
                b3 = b.reshape(blk // (2 * c), 2 * c, LANES)
                ref = jnp.broadcast_to(b3[:, c - 1:c, :], b3.shape).reshape(blk, LANES)
            decay = jnp.exp2((b - ref) * signs[li - 1])
            w = (jnp.where(upper_masks[li - 1], qf, kf) * decay).astype(BF16)
            a = jnp.where(pair_masks[li], _dot_nt(w, w), a)
        intra.append(a.astype(BF16))
    for h in range(hb):
        b = sums[h][:blk]
        b_last = b[blk - 1:blk, :]
        qf, kf = q_ref[rows, cols(h)].astype(F32), k_ref[rows, cols(h)].astype(F32)
        v = v_ref[rows, cols(h)]
        s_t = st_scr[h]
        inter = _dot_nt((qf * jnp.exp2(b)).astype(BF16), s_t.astype(BF16))
        o = inter + _dot(intra[h], v)
        k_t = (kf * jnp.exp2(b_last - b)).astype(BF16)
        st_scr[h] = s_t * jnp.exp2(b_last) + _dot_tn(v, k_t)
        gf = g_ref[rows, cols(h)].astype(F32)
        o_ref[rows, cols(h)] = (_rms(o, g_out) * (gf * jax.nn.sigmoid(gf))).astype(o_ref.dtype)


def _hgrn_kernel(q_ref, k_ref, v_ref, lf_ref, g_ref, s0_ref, sel_ref, lvl_ref, gout_ref,
                 o_ref, s_out_ref, st_scr, *, blk, levels, hb, n_blk):
    tb = pl.program_id(2)

    @pl.when(tb == 0)
    def _():
        for h in range(hb):
            st_scr[h] = s0_ref[0, h].T

    sel, lvl, g_out = sel_ref[...], lvl_ref[...], gout_ref[...]
    pair_masks = [lvl == li for li in range(len(levels) + 1)]
    row = lax.broadcasted_iota(jnp.int32, (blk, LANES), 0)
    upper_masks = [(row // c) % 2 == 1 for c in levels]
    signs = [jnp.where(u, 1.0, -1.0) for u in upper_masks]

    def step(c, carry):
        rows = pl.ds(pl.multiple_of(c * blk, blk), blk)
        _hgrn_block(q_ref, k_ref, v_ref, lf_ref, g_ref, o_ref, st_scr, rows, sel, pair_masks, upper_masks, signs,
                    g_out, blk=blk, levels=levels, hb=hb)
        return carry

    lax.fori_loop(0, n_blk, step, 0)

    @pl.when(tb == pl.num_programs(2) - 1)
    def _():
        for h in range(hb):
            s_out_ref[0, h] = st_scr[h].T


def _hgrn(q, k, v, lf, gate, s0, g_out, *, batch, t_len, blk, tt, hb):
    heads = q.shape[1] // LANES
    tt, hb = min(tt, t_len), min(hb, heads)
    levels, lvl, sel = _hgrn_tables(blk)
    nt = t_len // tt
    xspec = pl.BlockSpec((tt, hb * LANES), lambda b, h, t: (b * nt + t, h))
    sspec = pl.BlockSpec((1, hb, LANES, LANES), lambda b, h, t: (b, h, 0, 0))
    const = lambda a: pl.BlockSpec(a.shape, lambda b, h, t: (0, 0))
    kern = functools.partial(_hgrn_kernel, blk=blk, levels=levels, hb=hb, n_blk=tt // blk)
    g_out = g_out.reshape(1, LANES)
    return pl.pallas_call(
        kern,
        grid=(batch, heads // hb, nt),
        in_specs=[xspec, xspec, xspec, xspec, xspec, sspec, const(sel), const(lvl), const(g_out)],
        out_specs=[xspec, sspec],
        out_shape=[jax.ShapeDtypeStruct(q.shape, BF16), jax.ShapeDtypeStruct(s0.shape, F32)],
        scratch_shapes=[pltpu.VMEM((hb, LANES, LANES), F32)],
        compiler_params=_params("parallel", "parallel", "arbitrary"),
        name="hgrn2",
    )(q, k, v, lf, gate, s0, sel, lvl, g_out)


def _silu(x):
    h = 0.5 * x
    return h + h * jnp.tanh(h)


def _ff_tile(d_ff):
    assert d_ff % LANES == 0
    return 2 * LANES if d_ff % (2 * LANES) == 0 else LANES


def _up_conv_kernel(h_ref, wg_ref, wu_ref, cwg_ref, cwu_ref, cbg_ref, cbu_ref,
                    act_ref, tailg_ref, tailu_ref, ug_scr, uu_scr, carry_scr, *, tm, cm, tiles_per_seq):
    i, j = pl.program_id(0), pl.program_id(1)
    seq_start = (i % tiles_per_seq) == 0
    halves = ((wg_ref, cwg_ref, cbg_ref, tailg_ref, ug_scr), (wu_ref, cwu_ref, cbu_ref, tailu_ref, uu_scr))
    for half, (_, _, _, _, scr) in enumerate(halves):
        @pl.when(seq_start)
        def _(scr=scr):
            scr[0:8, :] = jnp.zeros((8, scr.shape[1]), F32)

        @pl.when(jnp.logical_not(seq_start))
        def _(scr=scr, half=half):
            scr[0:8, :] = carry_scr[half, j]

    for r in range(tm // cm):
        h = h_ref[r * cm:(r + 1) * cm, :]
        conv = []
        for w_ref, cw_ref, cb_ref, _, scr in halves:
            u = _dot(h, w_ref[...])
            scr[8 + r * cm:8 + (r + 1) * cm, :] = u
            cw = cw_ref[...]
            conv.append(cb_ref[...] + cw[2:3] * u + cw[1:2] * scr[7 + r * cm:7 + (r + 1) * cm, :]
                        + cw[0:1] * scr[6 + r * cm:6 + (r + 1) * cm, :])
        act_ref[r * cm:(r + 1) * cm, :] = (_silu(conv[0]) * conv[1]).astype(act_ref.dtype)

    for half, (_, _, _, tail_ref, scr) in enumerate(halves):
        tail = scr[tm:tm + 8, :]
        carry_scr[half, j] = tail
        tail_ref[...] = tail


def _up_conv_prompt(h, w_up, conv_w, conv_b, *, seq, tm, cm):
    m, d = h.shape
    n = w_up.shape[1] // 2
    tn = _ff_tile(n)
    tm = min(tm, seq)
    cm = min(cm, tm)
    assert seq % tm == 0 and tm % cm == 0 and conv_w.shape[0] == 3
    nj = n // tn
    gate = lambda r: pl.BlockSpec((r, tn), lambda i, j: (0, j))
    up = lambda r: pl.BlockSpec((r, tn), lambda i, j: (0, j + nj))
    tail = pl.BlockSpec((8, tn), lambda i, j: (i, j))
    kern = functools.partial(_up_conv_kernel, tm=tm, cm=cm, tiles_per_seq=seq // tm)
    return pl.pallas_call(
        kern,
        grid=(m // tm, nj),
        in_specs=[pl.BlockSpec((tm, d), lambda i, j: (i, 0)), gate(d), up(d), gate(3), up(3), gate(1), up(1)],
        out_specs=[pl.BlockSpec((tm, tn), lambda i, j: (i, j)), tail, tail],
        out_shape=[jax.ShapeDtypeStruct((m, n), BF16), jax.ShapeDtypeStruct((m // tm * 8, n), F32),
                   jax.ShapeDtypeStruct((m // tm * 8, n), F32)],
        scratch_shapes=[pltpu.VMEM((tm + 8, tn), F32), pltpu.VMEM((tm + 8, tn), F32),
                        pltpu.VMEM((2, nj, 8, tn), F32)],
        compiler_params=_params("arbitrary", "arbitrary"),
        name="ffn_up_conv",
    )(h, w_up, w_up, conv_w, conv_w, conv_b, conv_b)


def _conv_sample_kernel(ug_ref, uu_ref, hg_ref, hu_ref, cwg_ref, cwu_ref, cbg_ref, cbu_ref, act_ref, *, t_len, taps):
    conv = [[None] * t_len, [None] * t_len]
    for half, (u_ref, hist_ref, cw_ref, cb_ref) in enumerate(
            ((ug_ref, hg_ref, cwg_ref, cbg_ref), (uu_ref, hu_ref, cwu_ref, cbu_ref))):
        cw, cb = cw_ref[...], cb_ref[...]
        rows = [hist_ref[:, r, :] for r in range(taps - 1)] + [u_ref[:, t, :] for t in range(t_len)]
        for t in range(t_len):
            c = cb
            for tap in range(taps):
                c = c + cw[tap:tap + 1] * rows[t + tap]
            conv[half][t] = c
    for t in range(t_len):
        act_ref[:, t, :] = _silu(conv[0][t]) * conv[1][t]


def _conv_sample(u, hist, conv_w, conv_b):
    bsz, t_len, n2 = u.shape
    n = n2 // 2
    tn = _ff_tile(n)
    nj = n // tn
    taps = conv_w.shape[0]
    gate = lambda *lead: pl.BlockSpec((*lead, tn), lambda j: (*([0] * len(lead)), j))
    up = lambda *lead: pl.BlockSpec((*lead, tn), lambda j: (*([0] * len(lead)), j + nj))
    kern = functools.partial(_conv_sample_kernel, t_len=t_len, taps=taps)
    return pl.pallas_call(
        kern,
        grid=(nj,),
        in_specs=[gate(bsz, t_len), up(bsz, t_len), gate(bsz, taps - 1), up(bsz, taps - 1),
                  gate(taps), up(taps), gate(1), up(1)],
        out_specs=gate(bsz, t_len),
        out_shape=jax.ShapeDtypeStruct((bsz, t_len, n), F32),
        compiler_params=_params("parallel"),
        name="ffn_conv_sample",
    )(u, u, hist, hist, conv_w, conv_w, conv_b, conv_b)


def _pad_cols(w, n):
    return jnp.pad(w, ((0, 0), (0, n - w.shape[1])))


def _rope_tables(pos, rope):
    half = rope // 2
    inv = 1.0 / (ROPE_THETA ** (jnp.arange(half, dtype=F32) / half))
    ang = pos.astype(F32)[:, None] * inv[None, :]
    cos, sin = jnp.cos(ang), jnp.sin(ang)
    z = jnp.zeros_like(cos)
    zz = jnp.zeros((pos.shape[0], LANES - rope), F32)
    c = jnp.concatenate([cos, cos, zz], axis=-1)
    sa = jnp.concatenate([-sin, z, zz], axis=-1)
    sb = jnp.concatenate([z, sin, zz], axis=-1)
    return c, sa, sb


def _layer_weights(i, lb, g_attn_norm, w_in, g_q_a, w_uq, g_kv_a, w_uk, w_uv, g_q_nope, g_q_rope, g_k_nope,
                   g_k_rope, g_hg_out, w_o, g_ffn_norm, w_up, conv_w, conv_b, w_down, g_ple_norm, w_ple_gate,
                   w_ple):
    q_lora, kv_lora = g_q_a.shape[1], g_kv_a.shape[1]
    nope, rope = g_q_nope.shape[1], g_q_rope.shape[1]
    heads, v_dim = w_uk.shape[2], w_uv.shape[3]
    hg_width = lb.shape[0]
    d_ff = w_down.shape[1]
    assert nope == LANES and v_dim == LANES and g_hg_out.shape[1] == LANES and 2 * rope == LANES
    scale = float((nope + rope) ** -0.5) * float(np.log2(np.e))
    o3 = q_lora + kv_lora + rope
    w_in_i = w_in[i]
    wd = {"dims": dict(q_lora=q_lora, kv_lora=kv_lora, nope=nope, rope=rope, heads=heads, hg_width=hg_width,
                       d_ff=d_ff)}
    wd["g_attn"], wd["g_ffn"], wd["g_ple"] = g_attn_norm[i], g_ffn_norm[i], g_ple_norm[i]
    wd["w_lat"] = _pad_cols(w_in_i[:, :o3], o3 + LANES - rope).astype(BF16)
    wd["g_q_a"], wd["g_kv_a"] = g_q_a[i], g_kv_a[i]
    wd["g_k_rope_pad"] = jnp.pad(g_k_rope[i], (0, LANES - rope)).reshape(1, LANES)
    wd["w_hgrn"] = w_in_i[:, o3:].astype(BF16)
    wd["lb"] = lb.reshape(1, hg_width)
    wq = w_uq[i].reshape(q_lora, heads, nope + rope)
    wd["w_qn"] = wq[:, :, :nope].reshape(q_lora, heads * nope).astype(BF16)
    half = rope // 2
    slab = lambda a, b: jnp.pad(jnp.concatenate([a, b], axis=-1), [(0, 0)] * (a.ndim - 1) + [(0, LANES - rope)])
    flat = lambda w: w.reshape(q_lora, heads * LANES).astype(BF16)
    wr1, wr2 = wq[:, :, nope:nope + half], wq[:, :, nope + half:]
    wd["w_qr"], wd["w_qr_swap"] = flat(slab(wr1, wr2)), flat(slab(wr2, wr1))
    wd["g_qn"] = (jnp.tile(g_q_nope[i], heads) * scale).reshape(1, heads * nope)
    gr = g_q_rope[i] * scale
    wd["g_qr"] = jnp.tile(slab(gr[:half], gr[half:]), heads).reshape(1, heads * LANES)
    wd["g_qr_swap"] = jnp.tile(slab(gr[half:], gr[:half]), heads).reshape(1, heads * LANES)
    wd["w_uk"] = w_uk[i].reshape(kv_lora, heads * nope).astype(BF16)
    wd["g_kn"] = jnp.tile(g_k_nope[i], heads).reshape(1, heads * nope)
    wd["w_uv_t"] = jnp.transpose(w_uv[i], (1, 2, 0)).reshape(heads * v_dim, kv_lora).astype(BF16)
    wd["w_absorb"] = jnp.transpose(w_uk[i] * g_k_nope[i][None, None, :], (1, 2, 0)).astype(BF16)
    wd["w_uk_t"] = jnp.transpose(w_uk[i], (1, 2, 0)).reshape(heads * nope, kv_lora).astype(BF16)
    wd["w_uv_h"] = jnp.transpose(w_uv[i], (1, 0, 2)).astype(BF16)
    wd["g_hg_out"] = g_hg_out[i]
    mla_width = heads * v_dim
    assert w_o.shape[1] == 2 * mla_width
    wd["w_o"] = w_o[i].astype(BF16)
    wd["w_up"], wd["conv_w"], wd["conv_b"] = w_up[i].astype(BF16), conv_w[i], conv_b[i].reshape(1, 2 * d_ff)
    wd["w_down"] = w_down[i].astype(BF16)
    wd["w_ple_gate"], wd["w_ple"] = w_ple_gate[i].astype(BF16), w_ple[i].astype(BF16)
    return wd


def _mixer_inputs(x, wd, tabs, *, tm):
    d = wd["dims"]
    h = _rmsnorm(x, wd["g_attn"], TILES.norm_rows)
    cq, ckv, ckv_b, kr, kr_b = _latent(h, wd["w_lat"], wd["g_q_a"], wd["g_kv_a"], wd["g_k_rope_pad"], tabs,
                                      q_lora=d["q_lora"], kv_lora=d["kv_lora"], rope=d["rope"],
                                      tm=TILES.latent_rows)
    m = x.shape[0]
    n_tab = tabs[0].shape[0] // min(tm, m)
    tab_spec = pl.BlockSpec((min(tm, m), LANES), lambda i, j: (i % n_tab, 0))

    def qn_body(acc, g):
        return jnp.concatenate(_slab_rms(acc, g, d["nope"]), axis=-1)

    def qr_body(acc, acc_swap, g, g_swap, c, s):
        outs = []
        for a in range(acc.shape[1] // LANES):
            sl = slice(a * LANES, (a + 1) * LANES)
            x, xs = acc[:, sl], acc_swap[:, sl]
            rinv = lax.rsqrt(jnp.sum(x * x, axis=-1, keepdims=True) * (1.0 / d["rope"]) + EPS)
            outs.append((x * rinv * g[:, sl]) * c + (xs * rinv * g_swap[:, sl]) * s)
        return jnp.concatenate(outs, axis=-1)

    c_tab, sa_tab, sb_tab = tabs
    qn = _mm("q_nope", qn_body, [cq], [wd["w_qn"]], tm=tm, tn=TILES.cols_low_rank, out_dtypes=[BF16],
             cols=[wd["g_qn"]])
    qr = _mm("q_rope", qr_body, [cq, cq], [wd["w_qr"], wd["w_qr_swap"]], tm=tm, tn=TILES.cols_rope, out_dtypes=[BF16],
             group_sizes=(1, 1), cols=[wd["g_qr"], wd["g_qr_swap"]],
             extras=[(c_tab, tab_spec), (sa_tab + sb_tab, tab_spec)])

    def f_body(acc, lb):
        s = jax.nn.sigmoid(acc)
        log2_f = jnp.log2(lb + (1.0 - lb) * s)
        return log2_f, (1.0 - lb) * (1.0 - s)

    ident = lambda acc: acc
    hw = d["hg_width"]
    proj = functools.partial(_mm, lhs=[h], rhs=[wd["w_hgrn"]], tm=tm, tn=TILES.cols_wide, n=hw)
    hq = proj("hgrn_q", ident, out_dtypes=[BF16], rhs_col0=0)
    lf, hk = proj("hgrn_f", f_body, out_dtypes=[F32, BF16], cols=[wd["lb"]], rhs_col0=hw)
    hv = proj("hgrn_v", ident, out_dtypes=[BF16], rhs_col0=2 * hw)
    hg = proj("hgrn_g", ident, out_dtypes=[BF16], rhs_col0=3 * hw)
    return (qn, qr, ckv, ckv_b, kr, kr_b), (hq, hk, hv, lf, hg)


def _out_proj(x, o_mla, o_hg, wd, *, tm):
    return _mm("out_proj", lambda acc, res: res + acc, [o_mla, o_hg], [wd["w_o"], wd["w_o"]], rhs_row_blocks=(0, 1),
               tm=tm, tn=TILES.cols_wide, out_dtypes=[F32], rows=[x])


def _down_ple(x, act, p, wd, *, tm):
    x, xg, ssq = _mm("ffn_down", lambda acc, res: res + acc, [act], [wd["w_down"]], tm=min(tm, TILES.down_rows),
                     tn=TILES.down_cols, out_dtypes=[F32], rows=[x], next_norm_gain=wd["g_ple"])
    width = x.shape[1]

    def ple_body(gate, emb, res, ssq_blk):
        return res + emb * jax.nn.sigmoid(gate * _row_rinv(ssq_blk, width))

    return _mm("ple", ple_body, [xg, p.astype(BF16)], [wd["w_ple_gate"], wd["w_ple"]], tm=tm, tn=TILES.cols,
               out_dtypes=[F32], group_sizes=(1, 1), rows=[x], extras=[(ssq, _ssq_spec(min(tm, x.shape[0])))])


def _prompt_layer(x, p, wd, *, batch, seq):
    d = wd["dims"]
    tm = min(TILES.rows, seq)
    assert seq % tm == 0
    tabs = _rope_tables(jnp.arange(seq), d["rope"])
    (qn, qr, ckv, ckv_b, kr, kr_b), (hq, hk, hv, lf, hg) = _mixer_inputs(x, wd, tabs, tm=tm)

    def kn_body(acc, g):
        return jnp.concatenate(_slab_rms(acc, g, d["nope"]), axis=-1)

    kn = _mm("k_nope", kn_body, [ckv_b], [wd["w_uk"]], tm=tm, tn=TILES.cols_low_rank, out_dtypes=[BF16],
             cols=[wd["g_kn"]])
    tq = min(TILES.attn, seq)
    vt = _value_t(ckv_b, wd["w_uv_t"], tk=tq, tn=TILES.cols_low_rank)
    o_mla = _attn_prompt(qn, qr, kn, kr_b, vt, batch=batch, seq=seq, heads=d["heads"], tq=tq, hb=TILES.attn_heads)

    hg_heads = d["hg_width"] // LANES
    s0 = jnp.zeros((batch, hg_heads, LANES, LANES), F32)
    o_hg, s_fin = _hgrn(hq, hk, hv, lf, hg, s0, wd["g_hg_out"], batch=batch, t_len=seq, blk=CHUNK,
                        tt=TILES.hgrn_rows, hb=TILES.hgrn_heads)

    x = _out_proj(x, o_mla, o_hg, wd, tm=tm)
    h = _rmsnorm(x, wd["g_ffn"], TILES.norm_rows)
    tm_up = min(TILES.ff_rows, seq)
    act, tail_g, tail_u = _up_conv_prompt(h, wd["w_up"], wd["conv_w"], wd["conv_b"], seq=seq, tm=tm_up,
                                          cm=TILES.ff_chunk)
    last = lambda t: t.reshape(batch, seq // tm_up, 8, -1)[:, -1, 6:, :]
    conv_state = jnp.concatenate([last(tail_g), last(tail_u)], axis=-1)
    x = _down_ple(x, act, p, wd, tm=tm)
    return x, ckv.reshape(batch, seq, -1), kr.reshape(batch, seq, -1), s_fin, conv_state


def _sample_layer(x, p, cache_ckv, cache_kr, state_hgrn, hist, wd, *, batch, t_new):
    d = wd["dims"]
    m = batch * t_new
    past = cache_ckv.shape[1]
    tabs = _rope_tables(jnp.tile(past + jnp.arange(t_new), batch), d["rope"])
    (qn, qr, ckv, ckv_b, kr, kr_b), (hq, hk, hv, lf, hg) = _mixer_inputs(x, wd, tabs, tm=m)

    qt = _per_head_mm("mla_absorb_q", qn, wd["w_absorb"])
    qr_h = jnp.transpose(qr.reshape(m, d["heads"], LANES), (1, 0, 2))
    o_lat = _attn_sample(qt, qr_h, cache_ckv, cache_kr, ckv_b, kr_b, wd["w_uk_t"], batch=batch, t_new=t_new,
                         heads=d["heads"], nope=d["nope"], rope=d["rope"], tk=TILES.sample_keys)
    o_mla = _value_up(o_lat, wd["w_uv_h"])

    o_hg, s_new = _hgrn(hq, hk, hv, lf, hg, state_hgrn, wd["g_hg_out"], batch=batch, t_len=t_new, blk=t_new,
                        tt=t_new, hb=TILES.hgrn_heads_short)

    x = _out_proj(x, o_mla, o_hg, wd, tm=m)
    h = _rmsnorm(x, wd["g_ffn"], TILES.norm_rows)
    d_ff = d["d_ff"]
    u = _mm("ffn_up_sample", lambda acc: acc, [h], [wd["w_up"]], tm=m, tn=2 * _ff_tile(d_ff), out_dtypes=[F32])
    u3 = u.reshape(batch, t_new, 2 * d_ff)
    act = _conv_sample(u3, hist, wd["conv_w"], wd["conv_b"])
    taps = wd["conv_w"].shape[0]
    conv_state = jnp.concatenate([hist, u3], axis=1)[:, -(taps - 1):]
    x = _down_ple(x, act.reshape(m, d_ff).astype(BF16), p, wd, tm=m)
    return x, ckv.reshape(batch, t_new, -1), kr.reshape(batch, t_new, -1), s_new, conv_state


def kernel(x_prompt, x_sample, cache_ckv, cache_k_rope, state_hgrn, state_ffn_conv, p_prompt, p_sample,
           g_attn_norm, w_in, g_q_a, w_uq, g_kv_a, w_uk, w_uv, g_q_nope, g_q_rope, g_k_nope, g_k_rope,
           lb_logits, g_hg_out, w_o, g_ffn_norm, w_up, conv_w, conv_b, w_down, g_ple_norm, w_ple_gate, w_ple):
    depth = w_in.shape[0]
    batch, seq, d_model = x_prompt.shape
    dec_batch, t_new, _ = x_sample.shape
    assert t_new >= conv_w.shape[1] - 1
    lb_all = jnp.cumsum(jax.nn.softmax(lb_logits.astype(F32), axis=0), axis=0)
    xp = x_prompt.reshape(batch * seq, d_model)
    xs = x_sample.reshape(dec_batch * t_new, d_model)
    outs = [[] for _ in range(8)]
    for i in range(depth):
        wd = _layer_weights(i, lb_all[i], g_attn_norm, w_in, g_q_a, w_uq, g_kv_a, w_uk, w_uv, g_q_nope, g_q_rope,
                            g_k_nope, g_k_rope, g_hg_out, w_o, g_ffn_norm, w_up, conv_w, conv_b, w_down,
                            g_ple_norm, w_ple_gate, w_ple)
        xp, *new_p = _prompt_layer(xp, p_prompt[i].reshape(batch * seq, -1), wd, batch=batch, seq=seq)
        xs, *new_s = _sample_layer(xs, p_sample[i].reshape(dec_batch * t_new, -1), cache_ckv[i], cache_k_rope[i],
                                   state_hgrn[i], state_ffn_conv[i], wd, batch=dec_batch, t_new=t_new)
        for lst, val in zip(outs, new_p + new_s):
            lst.append(val)
    return (xp.reshape(batch, seq, d_model), xs.reshape(dec_batch, t_new, d_model), *[jnp.stack(o) for o in outs])
```

```python
import functools
from typing import NamedTuple

import numpy as np
import jax
import jax.numpy as jnp
from jax import lax
from jax.experimental import pallas as pl
from jax.experimental.pallas import tpu as pltpu

F32 = jnp.float32
BF16 = jnp.bfloat16

EPS = 1e-6
CHUNK = 64
ROPE_THETA = 10000.0
LANES = 128
NEG_BIG = -1e30
VMEM_LIMIT_BYTES = 56 * 1024 * 1024


class _Tiles(NamedTuple):
    rows: int = 1024
    cols: int = 512
    cols_wide: int = 1024
    cols_low_rank: int = 2048
    cols_rope: int = 2048
    down_rows: int = 512
    down_cols: int = 512
    norm_rows: int = 512
    latent_rows: int = 512
    attn: int = 512
    attn_heads: int = 4
    hgrn_rows: int = 512
    hgrn_heads: int = 8
    hgrn_heads_short: int = 16
    ff_rows: int = 4096
    ff_chunk: int = 512
    sample_keys: int = 1024


TILES = _Tiles()


def _params(*sem):
    return pltpu.CompilerParams(dimension_semantics=sem, vmem_limit_bytes=VMEM_LIMIT_BYTES)


def _dot(a, b):
    return jnp.dot(a, b, preferred_element_type=F32)


def _dot_nt(a, b):
    return lax.dot_general(a, b, (((1,), (1,)), ((), ())), preferred_element_type=F32)


def _dot_tn(a, b):
    return lax.dot_general(a, b, (((0,), (0,)), ((), ())), preferred_element_type=F32)


def _rms(x, g):
    ms = jnp.mean(x * x, axis=-1, keepdims=True)
    return x * lax.rsqrt(ms + EPS) * g


def _slab_rms(x, g, valid):
    outs = []
    for a in range(x.shape[1] // LANES):
        blk = x[:, a * LANES:(a + 1) * LANES]
        ms = jnp.sum(blk * blk, axis=-1, keepdims=True) * (1.0 / valid)
        outs.append(blk * lax.rsqrt(ms + EPS) * g[:, a * LANES:(a + 1) * LANES])
    return outs


def _rope_slab(y, c, sa, sb):
    return y * c + pltpu.roll(y, 96, 1) * sa + pltpu.roll(y, 32, 1) * sb


def _rmsnorm_kernel(x_ref, g_ref, o_ref):
    o_ref[...] = _rms(x_ref[...], g_ref[...]).astype(o_ref.dtype)


def _rmsnorm(x, g, tm):
    m, d = x.shape
    tm = min(tm, m)
    assert m % tm == 0
    return pl.pallas_call(
        _rmsnorm_kernel,
        grid=(m // tm,),
        in_specs=[pl.BlockSpec((tm, d), lambda i: (i, 0)), pl.BlockSpec((1, d), lambda i: (0, 0))],
        out_specs=pl.BlockSpec((tm, d), lambda i: (i, 0)),
        out_shape=jax.ShapeDtypeStruct((m, d), BF16),
        compiler_params=_params("parallel"),
        name="rmsnorm",
    )(x, g.reshape(1, d))


def _lane_fold(x):
    out = x[:, :LANES]
    for a in range(1, x.shape[1] // LANES):
        out = out + x[:, a * LANES:(a + 1) * LANES]
    return out


def _row_rinv(ssq, width):
    return lax.rsqrt(jnp.sum(ssq, axis=-1, keepdims=True) * (1.0 / width) + EPS)


def _mm_kernel(*refs, body, n_lhs, group_sizes, n_in, next_norm):
    ins, outs = refs[:n_in], refs[n_in:]
    lhs, rhs, rest = ins[:n_lhs], ins[n_lhs:2 * n_lhs], ins[2 * n_lhs:]
    accs, k = [], 0
    for gs in group_sizes:
        acc = None
        for _ in range(gs):
            d = _dot(lhs[k][...], rhs[k][...])
            acc = d if acc is None else acc + d
            k += 1
        accs.append(acc)
    if next_norm:
        *rest, gain_ref = rest
        *outs, scaled_ref, ssq_ref = outs
    res = body(*accs, *[r[...] for r in rest])
    if not isinstance(res, (tuple, list)):
        res = (res,)
    for o, r in zip(outs, res):
        o[...] = r.astype(o.dtype)
    if next_norm:
        x = res[0]
        scaled_ref[...] = (x * gain_ref[...]).astype(scaled_ref.dtype)
        @pl.when(pl.program_id(1) == 0)
        def _():
            ssq_ref[...] = jnp.zeros(ssq_ref.shape, F32)

        ssq_ref[...] += _lane_fold(x * x)


def _mm(name, body, lhs, rhs, *, tm, tn, out_dtypes, group_sizes=None, cols=(), rows=(), extras=(),
        next_norm_gain=None, n=None, rhs_col0=0, rhs_row_blocks=None):
    m = lhs[0].shape[0]
    n = n or rhs[0].shape[1]
    tm, tn = min(tm, m), min(tn, n)
    assert m % tm == 0 and n % tn == 0 and rhs_col0 % tn == 0, (name, m, n, tm, tn)
    j0 = rhs_col0 // tn
    group_sizes = tuple(group_sizes or (len(lhs),))
    rhs_row_blocks = rhs_row_blocks or (0,) * len(rhs)
    tile = pl.BlockSpec((tm, tn), lambda i, j: (i, j))
    in_specs = [pl.BlockSpec((tm, a.shape[1]), lambda i, j: (i, 0)) for a in lhs]
    in_specs += [pl.BlockSpec((a.shape[1], tn), lambda i, j, rb=rb: (rb, j + j0)) for a, rb in zip(lhs, rhs_row_blocks)]
    in_specs += [pl.BlockSpec((c.shape[0], tn), lambda i, j: (0, j)) for c in cols]
    in_specs += [tile for _ in rows]
    in_specs += [spec for _, spec in extras]
    args = list(lhs) + list(rhs) + list(cols) + list(rows) + [a for a, _ in extras]
    out_specs = [tile for _ in out_dtypes]
    out_shape = [jax.ShapeDtypeStruct((m, n), dt) for dt in out_dtypes]
    if next_norm_gain is not None:
        in_specs.append(pl.BlockSpec((1, tn), lambda i, j: (0, j)))
        args.append(next_norm_gain.reshape(1, n))
        out_specs += [tile, pl.BlockSpec((tm, LANES), lambda i, j: (i, 0))]
        out_shape += [jax.ShapeDtypeStruct((m, n), BF16), jax.ShapeDtypeStruct((m, LANES), F32)]
    kern = functools.partial(_mm_kernel, body=body, n_lhs=len(lhs), group_sizes=group_sizes, n_in=len(args),
                             next_norm=next_norm_gain is not None)
    outs = pl.pallas_call(
        kern,
        grid=(m // tm, n // tn),
        in_specs=in_specs,
        out_specs=out_specs,
        out_shape=out_shape,
        compiler_params=_params("parallel", "arbitrary"),
        name=name,
    )(*args)
    return outs if len(outs) > 1 else outs[0]


def _ssq_spec(tm):
    return pl.BlockSpec((tm, LANES), lambda i, j: (i, 0))


def _latent_kernel(h_ref, w_ref, gq_ref, gkv_ref, gkr_ref, c_ref, sa_ref, sb_ref,
                   cq_ref, ckv_ref, ckvb_ref, kr_ref, krb_ref, *, q_lora, kv_lora, rope):
    z = _dot(h_ref[...], w_ref[...])
    cq_ref[...] = _rms(z[:, :q_lora], gq_ref[...]).astype(cq_ref.dtype)
    ckv = _rms(z[:, q_lora:q_lora + kv_lora], gkv_ref[...])
    ckv_ref[...] = ckv
    ckvb_ref[...] = ckv.astype(ckvb_ref.dtype)
    (kr,) = _slab_rms(z[:, q_lora + kv_lora:], gkr_ref[...], rope)
    kr = _rope_slab(kr, c_ref[...], sa_ref[...], sb_ref[...])
    kr_ref[...] = kr[:, :rope]
    krb_ref[...] = kr.astype(krb_ref.dtype)


def _latent(h, w_lat, g_q_a, g_kv_a, g_k_rope_pad, tabs, *, q_lora, kv_lora, rope, tm):
    m, d = h.shape
    tm = min(tm, m)
    n = w_lat.shape[1]
    n_tab = tabs[0].shape[0] // tm
    row = lambda w: pl.BlockSpec((tm, w), lambda i: (i, 0))
    const = lambda r, w: pl.BlockSpec((r, w), lambda i: (0, 0))
    tab = pl.BlockSpec((tm, LANES), lambda i: (i % n_tab, 0))
    kern = functools.partial(_latent_kernel, q_lora=q_lora, kv_lora=kv_lora, rope=rope)
    return pl.pallas_call(
        kern,
        grid=(m // tm,),
        in_specs=[row(d), const(d, n), const(1, q_lora), const(1, kv_lora), const(1, LANES), tab, tab, tab],
        out_specs=[row(q_lora), row(kv_lora), row(kv_lora), row(rope), row(LANES)],
        out_shape=[jax.ShapeDtypeStruct((m, q_lora), BF16), jax.ShapeDtypeStruct((m, kv_lora), F32),
                   jax.ShapeDtypeStruct((m, kv_lora), BF16), jax.ShapeDtypeStruct((m, rope), F32),
                   jax.ShapeDtypeStruct((m, LANES), BF16)],
        compiler_params=_params("parallel"),
        name="mla_latent",
    )(h, w_lat, g_q_a.reshape(1, -1), g_kv_a.reshape(1, -1), g_k_rope_pad, *tabs)


def _attn_kernel(qn_ref, qr_ref, kn_ref, kr_ref, vt_ref, o_ref, m_scr, l_scr, acc_scr, *, tq, chunk, hb):
    qi = pl.program_id(2)
    lanes = lambda h: slice(h * LANES, (h + 1) * LANES)
    qs = [jnp.concatenate([qn_ref[:, lanes(h)], qr_ref[:, lanes(h)]], axis=-1) for h in range(hb)]
    m_scr[...] = jnp.full(m_scr.shape, NEG_BIG, F32)
    l_scr[...] = jnp.zeros(l_scr.shape, F32)
    acc_scr[...] = jnp.zeros(acc_scr.shape, F32)

    def block(kb, masked):
        off = pl.multiple_of(kb * tq, tq)
        kr = kr_ref[pl.ds(off, tq), :]
        scores = []
        for h in range(hb):
            k = jnp.concatenate([kn_ref[pl.ds(off, tq), lanes(h)], kr], axis=-1)
            s = _dot_nt(k, qs[h])
            if masked:
                kc = lax.broadcasted_iota(jnp.int32, (tq, tq), 0) // chunk
                qc = lax.broadcasted_iota(jnp.int32, (tq, tq), 1) // chunk
                s = jnp.where(kc <= qc, s, NEG_BIG)
            scores.append(s)
        probs = []
        for h in range(hb):
            m_i = m_scr[h]
            m_new = jnp.maximum(m_i, jnp.max(scores[h], axis=0, keepdims=True))
            alpha = jnp.exp2(m_i - m_new)
            p = jnp.exp2(scores[h] - m_new)
            l_scr[h] = alpha * l_scr[h] + jnp.sum(p, axis=0, keepdims=True)
            m_scr[h] = m_new
            probs.append((alpha, p.astype(BF16)))
        for h in range(hb):
            alpha, p = probs[h]
            acc_scr[h] = alpha * acc_scr[h] + _dot(vt_ref[kb, lanes(h), :], p)

    def body(kb, carry):
        block(kb, False)
        return carry

    lax.fori_loop(0, qi, body, 0)
    block(qi, True)
    for h in range(hb):
        o_ref[:, lanes(h)] = (acc_scr[h] / l_scr[h]).T.astype(o_ref.dtype)


def _attn_prompt(qn, qr, kn, krb, vt, *, batch, seq, heads, tq, hb):
    nq = seq // tq
    hb = min(hb, heads)
    assert heads % hb == 0
    qspec = pl.BlockSpec((tq, hb * LANES), lambda b, h, i: (b * nq + i, h))
    kspec = pl.BlockSpec((seq, hb * LANES), lambda b, h, i: (b, h))
    kern = functools.partial(_attn_kernel, tq=tq, chunk=CHUNK, hb=hb)
    return pl.pallas_call(
        kern,
        grid=(batch, heads // hb, nq),
        in_specs=[qspec, qspec, kspec, pl.BlockSpec((seq, LANES), lambda b, h, i: (b, 0)),
                  pl.BlockSpec((nq, hb * LANES, tq), lambda b, h, i: (b, h, 0))],
        out_specs=qspec,
        out_shape=jax.ShapeDtypeStruct(qn.shape, BF16),
        scratch_shapes=[pltpu.VMEM((hb, 1, tq), F32), pltpu.VMEM((hb, 1, tq), F32),
                        pltpu.VMEM((hb, LANES, tq), F32)],
        compiler_params=_params("parallel", "parallel", "arbitrary"),
        name="mla_attn_prompt",
    )(qn, qr, kn, krb, vt)


def _value_t_kernel(w_ref, x_ref, o_ref):
    o_ref[0] = _dot_nt(w_ref[...], x_ref[...]).astype(o_ref.dtype)


def _value_t(x, w_t, *, tk, tn):
    m, c = x.shape
    n = w_t.shape[0]
    tn = min(tn, n)
    return pl.pallas_call(
        _value_t_kernel,
        grid=(m // tk, n // tn),
        in_specs=[pl.BlockSpec((tn, c), lambda i, j: (j, 0)), pl.BlockSpec((tk, c), lambda i, j: (i, 0))],
        out_specs=pl.BlockSpec((1, tn, tk), lambda i, j: (i, j, 0)),
        out_shape=jax.ShapeDtypeStruct((m // tk, n, tk), BF16),
        compiler_params=_params("parallel", "arbitrary"),
        name="v_up_t",
    )(w_t, x)


def _absorb_kernel(x_ref, w_ref, o_ref):
    o_ref[0] = _dot(x_ref[...], w_ref[0]).astype(o_ref.dtype)


def _per_head_mm(name, x, w):
    m = x.shape[0]
    heads, _, n = w.shape
    return pl.pallas_call(
        _absorb_kernel,
        grid=(heads,),
        in_specs=[pl.BlockSpec((m, LANES), lambda h: (0, h)), pl.BlockSpec((1, LANES, n), lambda h: (h, 0, 0))],
        out_specs=pl.BlockSpec((1, m, n), lambda h: (h, 0, 0)),
        out_shape=jax.ShapeDtypeStruct((heads, m, n), BF16),
        compiler_params=_params("parallel"),
        name=name,
    )(x, w)


def _vup_kernel(x_ref, w_ref, o_ref):
    o_ref[...] = _dot(x_ref[0], w_ref[0]).astype(o_ref.dtype)


def _value_up(o_lat, w):
    heads, m, c = o_lat.shape
    return pl.pallas_call(
        _vup_kernel,
        grid=(heads,),
        in_specs=[pl.BlockSpec((1, m, c), lambda h: (h, 0, 0)), pl.BlockSpec((1, c, LANES), lambda h: (h, 0, 0))],
        out_specs=pl.BlockSpec((m, LANES), lambda h: (0, h)),
        out_shape=jax.ShapeDtypeStruct((m, heads * LANES), BF16),
        compiler_params=_params("parallel"),
        name="mla_value_up",
    )(o_lat, w)


def _attn_sample_kernel(qt_ref, qr_ref, cache_ref, kcache_ref, cnew_ref, knew_ref, wkt_ref, o_ref,
                        *, heads, t_new, past, tk, nope, rope):
    hq = heads * t_new
    c_lat = cache_ref.shape[-1]
    qt = qt_ref[...].reshape(hq, c_lat)
    qr = qr_ref[...].reshape(hq, LANES)[:, :rope]
    wkt = wkt_ref[...]

    def scores(c_b, kr_b):
        n = c_b.shape[0]
        kn_t = _dot_nt(wkt, c_b)
        ssq = jnp.sum((kn_t * kn_t).reshape(heads, nope, n), axis=1)
        rinv = lax.rsqrt(ssq * (1.0 / nope) + EPS)
        rinv = jnp.broadcast_to(rinv[:, None, :], (heads, t_new, n)).reshape(hq, n)
        return _dot_nt(qt, c_b) * rinv + _dot_nt(qr, kr_b)

    c_new = cnew_ref[...]
    s_new = scores(c_new, knew_ref[...][:, :rope])
    m_i = jnp.max(s_new, axis=-1, keepdims=True)
    p = jnp.exp2(s_new - m_i)
    l_i = jnp.sum(p, axis=-1, keepdims=True)
    acc = _dot(p.astype(BF16), c_new)
    for kt in range(past // tk):
        c_b = cache_ref[0, kt * tk:(kt + 1) * tk, :].astype(BF16)
        s = scores(c_b, kcache_ref[0, kt * tk:(kt + 1) * tk, :].astype(BF16))
        m_new = jnp.maximum(m_i, jnp.max(s, axis=-1, keepdims=True))
        alpha = jnp.exp2(m_i - m_new)
        p = jnp.exp2(s - m_new)
        l_i = alpha * l_i + jnp.sum(p, axis=-1, keepdims=True)
        acc = alpha * acc + _dot(p.astype(BF16), c_b)
        m_i = m_new
    o_ref[...] = (acc / l_i).reshape(heads, t_new, c_lat).astype(o_ref.dtype)


def _attn_sample(qt, qr_h, cache, kcache, ckv_new, kr_new, wkt, *, batch, t_new, heads, nope, rope, tk):
    past, c_lat = cache.shape[1], cache.shape[2]
    tk = min(tk, past)
    kern = functools.partial(_attn_sample_kernel, heads=heads, t_new=t_new, past=past, tk=tk, nope=nope, rope=rope)
    return pl.pallas_call(
        kern,
        grid=(batch,),
        in_specs=[pl.BlockSpec((heads, t_new, c_lat), lambda b: (0, b, 0)),
                  pl.BlockSpec((heads, t_new, LANES), lambda b: (0, b, 0)),
                  pl.BlockSpec((1, past, c_lat), lambda b: (b, 0, 0)),
                  pl.BlockSpec((1, past, rope), lambda b: (b, 0, 0)),
                  pl.BlockSpec((t_new, c_lat), lambda b: (b, 0)),
                  pl.BlockSpec((t_new, LANES), lambda b: (b, 0)),
                  pl.BlockSpec(wkt.shape, lambda b: (0, 0))],
        out_specs=pl.BlockSpec((heads, t_new, c_lat), lambda b: (0, b, 0)),
        out_shape=jax.ShapeDtypeStruct((heads, batch * t_new, c_lat), BF16),
        compiler_params=_params("parallel"),
        name="mla_attn_sample",
    )(qt, qr_h, cache, kcache, ckv_new, kr_new, wkt)


def _hgrn_tables(blk):
    t = np.arange(blk)
    levels = [blk >> (i + 1) for i in range(int(np.log2(blk)))]
    lvl = np.full((blk, blk), -1, np.int32)
    lvl[t, t] = 0
    for li, c in enumerate(levels, start=1):
        same_pair = (t[:, None] // (2 * c)) == (t[None, :] // (2 * c))
        split = (t[:, None] // c) != (t[None, :] // c)
        lvl[same_pair & split & (t[:, None] > t[None, :])] = li
    prefix = (t[None, :] <= t[:, None]).astype(np.float32)
    rows = [prefix]
    for c in (2, 1):
        mid = (t // (2 * c)) * (2 * c) + c - 1
        rows.append((t[None, :] <= mid[:, None]).astype(np.float32))
    return levels, jnp.asarray(lvl), jnp.asarray(np.concatenate(rows, axis=0), dtype=BF16)


def _hgrn_block(q_ref, k_ref, v_ref, lf_ref, g_ref, o_ref, st_scr, rows, sel, pair_masks, upper_masks, signs,
                g_out, *, blk, levels, hb):
    cols = lambda h: slice(h * LANES, (h + 1) * LANES)
    sums = []
    for h in range(hb):
        lf = lf_ref[rows, cols(h)]
        hi = lf.astype(BF16)
        lo = (lf - hi.astype(F32)).astype(BF16)
        both = _dot(sel, jnp.concatenate([hi, lo], axis=-1))
        sums.append(both[:, :LANES] + both[:, LANES:])
    intra = []
    for h in range(hb):
        b = sums[h][:blk]
        q, k = q_ref[rows, cols(h)], k_ref[rows, cols(h)]
        qf, kf = q.astype(F32), k.astype(F32)
        a = jnp.where(pair_masks[0], _dot_nt(q, k), 0.0)
        for li, c in enumerate(levels, start=1):
            if c == 2:
                ref = sums[h][blk:2 * blk]
            elif c == 1:
                ref = sums[h][2 * blk:]
            else:
                b3 = b.reshape(blk // (2 * c), 2 * c, LANES)
                ref = jnp.broadcast_to(b3[:, c - 1:c, :], b3.shape).reshape(blk, LANES)
            decay = jnp.exp2((b - ref) * signs[li - 1])
            w = (jnp.where(upper_masks[li - 1], qf, kf) * decay).astype(BF16)
            a = jnp.where(pair_masks[li], _dot_nt(w, w), a)
        intra.append(a.astype(BF16))
    for h in range(hb):
        b = sums[h][:blk]
        b_last = b[blk - 1:blk, :]
        qf, kf = q_ref[rows, cols(h)].astype(F32), k_ref[rows, cols(h)].astype(F32)
        v = v_ref[rows, cols(h)]
        s_t = st_scr[h]
        inter = _dot_nt((qf * jnp.exp2(b)).astype(BF16), s_t.astype(BF16))
        o = inter + _dot(intra[h], v)
        k_t = (kf * jnp.exp2(b_last - b)).astype(BF16)
        st_scr[h] = s_t * jnp.exp2(b_last) + _dot_tn(v, k_t)
        gf = g_ref[rows, cols(h)].astype(F32)
        o_ref[rows, cols(h)] = (_rms(o, g_out) * (gf * jax.nn.sigmoid(gf))).astype(o_ref.dtype)


def _hgrn_kernel(q_ref, k_ref, v_ref, lf_ref, g_ref, s0_ref, sel_ref, lvl_ref, gout_ref,
                 o_ref, s_out_ref, st_scr, *, blk, levels, hb, n_blk):
    tb = pl.program_id(2)

    @pl.when(tb == 0)
    def _():
        for h in range(hb):
            st_scr[h] = s0_ref[0, h].T

    sel, lvl, g_out = sel_ref[...], lvl_ref[...], gout_ref[...]
    pair_masks = [lvl == li for li in range(len(levels) + 1)]
    row = lax.broadcasted_iota(jnp.int32, (blk, LANES), 0)
    upper_masks = [(row // c) % 2 == 1 for c in levels]
    signs = [jnp.where(u, 1.0, -1.0) for u in upper_masks]

    def step(c, carry):
        rows = pl.ds(pl.multiple_of(c * blk, blk), blk)
        _hgrn_block(q_ref, k_ref, v_ref, lf_ref, g_ref, o_ref, st_scr, rows, sel, pair_masks, upper_masks, signs,
                    g_out, blk=blk, levels=levels, hb=hb)
        return carry

    lax.fori_loop(0, n_blk, step, 0)

    @pl.when(tb == pl.num_programs(2) - 1)
    def _():
        for h in range(hb):
            s_out_ref[0, h] = st_scr[h].T


def _hgrn(q, k, v, lf, gate, s0, g_out, *, batch, t_len, blk, tt, hb):
    heads = q.shape[1] // LANES
    tt, hb = min(tt, t_len), min(hb, heads)
    levels, lvl, sel = _hgrn_tables(blk)
    nt = t_len // tt
    xspec = pl.BlockSpec((tt, hb * LANES), lambda b, h, t: (b * nt + t, h))
    sspec = pl.BlockSpec((1, hb, LANES, LANES), lambda b, h, t: (b, h, 0, 0))
    const = lambda a: pl.BlockSpec(a.shape, lambda b, h, t: (0, 0))
    kern = functools.partial(_hgrn_kernel, blk=blk, levels=levels, hb=hb, n_blk=tt // blk)
    g_out = g_out.reshape(1, LANES)
    return pl.pallas_call(
        kern,
        grid=(batch, heads // hb, nt),
        in_specs=[xspec, xspec, xspec, xspec, xspec, sspec, const(sel), const(lvl), const(g_out)],
        out_specs=[xspec, sspec],
        out_shape=[jax.ShapeDtypeStruct(q.shape, BF16), jax.ShapeDtypeStruct(s0.shape, F32)],
        scratch_shapes=[pltpu.VMEM((hb, LANES, LANES), F32)],
        compiler_params=_params("parallel", "parallel", "arbitrary"),
        name="hgrn2",
    )(q, k, v, lf, gate, s0, sel, lvl, g_out)


def _silu(x):
    h = 0.5 * x
    return h + h * jnp.tanh(h)


def _ff_tile(d_ff):
    assert d_ff % LANES == 0
    return 2 * LANES if d_ff % (2 * LANES) == 0 else LANES


def _up_conv_kernel(h_ref, wg_ref, wu_ref, cwg_ref, cwu_ref, cbg_ref, cbu_ref,
                    act_ref, tailg_ref, tailu_ref, ug_scr, uu_scr, carry_scr, *, tm, cm, tiles_per_seq):
    i, j = pl.program_id(0), pl.program_id(1)
    seq_start = (i % tiles_per_seq) == 0
    halves = ((wg_ref, cwg_ref, cbg_ref, tailg_ref, ug_scr), (wu_ref, cwu_ref, cbu_ref, tailu_ref, uu_scr))
    for half, (_, _, _, _, scr) in enumerate(halves):
        @pl.when(seq_start)
        def _(scr=scr):
            scr[0:8, :] = jnp.zeros((8, scr.shape[1]), F32)

        @pl.when(jnp.logical_not(seq_start))
        def _(scr=scr, half=half):
            scr[0:8, :] = carry_scr[half, j]

    for r in range(tm // cm):
        h = h_ref[r * cm:(r + 1) * cm, :]
        conv = []
        for w_ref, cw_ref, cb_ref, _, scr in halves:
            u = _dot(h, w_ref[...])
            scr[8 + r * cm:8 + (r + 1) * cm, :] = u
            cw = cw_ref[...]
            conv.append(cb_ref[...] + cw[2:3] * u + cw[1:2] * scr[7 + r * cm:7 + (r + 1) * cm, :]
                        + cw[0:1] * scr[6 + r * cm:6 + (r + 1) * cm, :])
        act_ref[r * cm:(r + 1) * cm, :] = (_silu(conv[0]) * conv[1]).astype(act_ref.dtype)

    for half, (_, _, _, tail_ref, scr) in enumerate(halves):
        tail = scr[tm:tm + 8, :]
        carry_scr[half, j] = tail
        tail_ref[...] = tail


def _up_conv_prompt(h, w_up, conv_w, conv_b, *, seq, tm, cm):
    m, d = h.shape
    n = w_up.shape[1] // 2
    tn = _ff_tile(n)
    tm = min(tm, seq)
    cm = min(cm, tm)
    assert seq % tm == 0 and tm % cm == 0 and conv_w.shape[0] == 3
    nj = n // tn
    gate = lambda r: pl.BlockSpec((r, tn), lambda i, j: (0, j))
    up = lambda r: pl.BlockSpec((r, tn), lambda i, j: (0, j + nj))
    tail = pl.BlockSpec((8, tn), lambda i, j: (i, j))
    kern = functools.partial(_up_conv_kernel, tm=tm, cm=cm, tiles_per_seq=seq // tm)
    return pl.pallas_call(
        kern,
        grid=(m // tm, nj),
        in_specs=[pl.BlockSpec((tm, d), lambda i, j: (i, 0), pipeline_mode=pl.Buffered(1)),
                  gate(d), up(d), gate(3), up(3), gate(1), up(1)],
        out_specs=[pl.BlockSpec((tm, tn), lambda i, j: (i, j)), tail, tail],
        out_shape=[jax.ShapeDtypeStruct((m, n), BF16), jax.ShapeDtypeStruct((m // tm * 8, n), F32),
                   jax.ShapeDtypeStruct((m // tm * 8, n), F32)],
        scratch_shapes=[pltpu.VMEM((tm + 8, tn), F32), pltpu.VMEM((tm + 8, tn), F32),
                        pltpu.VMEM((2, nj, 8, tn), F32)],
        compiler_params=_params("arbitrary", "arbitrary"),
        name="ffn_up_conv",
    )(h, w_up, w_up, conv_w, conv_w, conv_b, conv_b)


def _conv_sample_kernel(ug_ref, uu_ref, hg_ref, hu_ref, cwg_ref, cwu_ref, cbg_ref, cbu_ref, act_ref, *, t_len, taps):
    conv = [[None] * t_len, [None] * t_len]
    for half, (u_ref, hist_ref, cw_ref, cb_ref) in enumerate(
            ((ug_ref, hg_ref, cwg_ref, cbg_ref), (uu_ref, hu_ref, cwu_ref, cbu_ref))):
        cw, cb = cw_ref[...], cb_ref[...]
        rows = [hist_ref[:, r, :] for r in range(taps - 1)] + [u_ref[:, t, :] for t in range(t_len)]
        for t in range(t_len):
            c = cb
            for tap in range(taps):
                c = c + cw[tap:tap + 1] * rows[t + tap]
            conv[half][t] = c
    for t in range(t_len):
        act_ref[:, t, :] = _silu(conv[0][t]) * conv[1][t]


def _conv_sample(u, hist, conv_w, conv_b):
    bsz, t_len, n2 = u.shape
    n = n2 // 2
    tn = _ff_tile(n)
    nj = n // tn
    taps = conv_w.shape[0]
    gate = lambda *lead: pl.BlockSpec((*lead, tn), lambda j: (*([0] * len(lead)), j))
    up = lambda *lead: pl.BlockSpec((*lead, tn), lambda j: (*([0] * len(lead)), j + nj))
    kern = functools.partial(_conv_sample_kernel, t_len=t_len, taps=taps)
    return pl.pallas_call(
        kern,
        grid=(nj,),
        in_specs=[gate(bsz, t_len), up(bsz, t_len), gate(bsz, taps - 1), up(bsz, taps - 1),
                  gate(taps), up(taps), gate(1), up(1)],
        out_specs=gate(bsz, t_len),
        out_shape=jax.ShapeDtypeStruct((bsz, t_len, n), F32),
        compiler_params=_params("parallel"),
        name="ffn_conv_sample",
    )(u, u, hist, hist, conv_w, conv_w, conv_b, conv_b)


def _pad_cols(w, n):
    return jnp.pad(w, ((0, 0), (0, n - w.shape[1])))


def _rope_tables(pos, rope):
    half = rope // 2
    inv = 1.0 / (ROPE_THETA ** (jnp.arange(half, dtype=F32) / half))
    ang = pos.astype(F32)[:, None] * inv[None, :]
    cos, sin = jnp.cos(ang), jnp.sin(ang)
    z = jnp.zeros_like(cos)
    zz = jnp.zeros((pos.shape[0], LANES - rope), F32)
    c = jnp.concatenate([cos, cos, zz], axis=-1)
    sa = jnp.concatenate([-sin, z, zz], axis=-1)
    sb = jnp.concatenate([z, sin, zz], axis=-1)
    return c, sa, sb


def _layer_weights(i, lb, g_attn_norm, w_in, g_q_a, w_uq, g_kv_a, w_uk, w_uv, g_q_nope, g_q_rope, g_k_nope,
                   g_k_rope, g_hg_out, w_o, g_ffn_norm, w_up, conv_w, conv_b, w_down, g_ple_norm, w_ple_gate,
                   w_ple):
    q_lora, kv_lora = g_q_a.shape[1], g_kv_a.shape[1]
    nope, rope = g_q_nope.shape[1], g_q_rope.shape[1]
    heads, v_dim = w_uk.shape[2], w_uv.shape[3]
    hg_width = lb.shape[0]
    d_ff = w_down.shape[1]
    assert nope == LANES and v_dim == LANES and g_hg_out.shape[1] == LANES and 2 * rope == LANES
    scale = float((nope + rope) ** -0.5) * float(np.log2(np.e))
    o3 = q_lora + kv_lora + rope
    w_in_i = w_in[i]
    wd = {"dims": dict(q_lora=q_lora, kv_lora=kv_lora, nope=nope, rope=rope, heads=heads, hg_width=hg_width,
                       d_ff=d_ff)}
    wd["g_attn"], wd["g_ffn"], wd["g_ple"] = g_attn_norm[i], g_ffn_norm[i], g_ple_norm[i]
    wd["w_lat"] = _pad_cols(w_in_i[:, :o3], o3 + LANES - rope).astype(BF16)
    wd["g_q_a"], wd["g_kv_a"] = g_q_a[i], g_kv_a[i]
    wd["g_k_rope_pad"] = jnp.pad(g_k_rope[i], (0, LANES - rope)).reshape(1, LANES)
    wd["w_hgrn"] = w_in_i[:, o3:].astype(BF16)
    wd["lb"] = lb.reshape(1, hg_width)
    wq = w_uq[i].reshape(q_lora, heads, nope + rope)
    wd["w_qn"] = wq[:, :, :nope].reshape(q_lora, heads * nope).astype(BF16)
    half = rope // 2
    slab = lambda a, b: jnp.pad(jnp.concatenate([a, b], axis=-1), [(0, 0)] * (a.ndim - 1) + [(0, LANES - rope)])
    flat = lambda w: w.reshape(q_lora, heads * LANES).astype(BF16)
    wr1, wr2 = wq[:, :, nope:nope + half], wq[:, :, nope + half:]
    wd["w_qr"], wd["w_qr_swap"] = flat(slab(wr1, wr2)), flat(slab(wr2, wr1))
    wd["g_qn"] = (jnp.tile(g_q_nope[i], heads) * scale).reshape(1, heads * nope)
    gr = g_q_rope[i] * scale
    wd["g_qr"] = jnp.tile(slab(gr[:half], gr[half:]), heads).reshape(1, heads * LANES)
    wd["g_qr_swap"] = jnp.tile(slab(gr[half:], gr[:half]), heads).reshape(1, heads * LANES)
    wd["w_uk"] = w_uk[i].reshape(kv_lora, heads * nope).astype(BF16)
    wd["g_kn"] = jnp.tile(g_k_nope[i], heads).reshape(1, heads * nope)
    wd["w_uv_t"] = jnp.transpose(w_uv[i], (1, 2, 0)).reshape(heads * v_dim, kv_lora).astype(BF16)
    wd["w_absorb"] = jnp.transpose(w_uk[i] * g_k_nope[i][None, None, :], (1, 2, 0)).astype(BF16)
    wd["w_uk_t"] = jnp.transpose(w_uk[i], (1, 2, 0)).reshape(heads * nope, kv_lora).astype(BF16)
    wd["w_uv_h"] = jnp.transpose(w_uv[i], (1, 0, 2)).astype(BF16)
    wd["g_hg_out"] = g_hg_out[i]
    mla_width = heads * v_dim
    assert w_o.shape[1] == 2 * mla_width
    wd["w_o"] = w_o[i].astype(BF16)
    wd["w_up"], wd["conv_w"], wd["conv_b"] = w_up[i].astype(BF16), conv_w[i], conv_b[i].reshape(1, 2 * d_ff)
    wd["w_down"] = w_down[i].astype(BF16)
    wd["w_ple_gate"], wd["w_ple"] = w_ple_gate[i].astype(BF16), w_ple[i].astype(BF16)
    return wd


def _mixer_inputs(x, wd, tabs, *, tm):
    d = wd["dims"]
    h = _rmsnorm(x, wd["g_attn"], TILES.norm_rows)
    cq, ckv, ckv_b, kr, kr_b = _latent(h, wd["w_lat"], wd["g_q_a"], wd["g_kv_a"], wd["g_k_rope_pad"], tabs,
                                      q_lora=d["q_lora"], kv_lora=d["kv_lora"], rope=d["rope"],
                                      tm=TILES.latent_rows)
    m = x.shape[0]
    n_tab = tabs[0].shape[0] // min(tm, m)
    tab_spec = pl.BlockSpec((min(tm, m), LANES), lambda i, j: (i % n_tab, 0))

    def qn_body(acc, g):
        return jnp.concatenate(_slab_rms(acc, g, d["nope"]), axis=-1)

    def qr_body(acc, acc_swap, g, g_swap, c, s):
        outs = []
        for a in range(acc.shape[1] // LANES):
            sl = slice(a * LANES, (a + 1) * LANES)
            x, xs = acc[:, sl], acc_swap[:, sl]
            rinv = lax.rsqrt(jnp.sum(x * x, axis=-1, keepdims=True) * (1.0 / d["rope"]) + EPS)
            outs.append((x * rinv * g[:, sl]) * c + (xs * rinv * g_swap[:, sl]) * s)
        return jnp.concatenate(outs, axis=-1)

    c_tab, sa_tab, sb_tab = tabs
    qn = _mm("q_nope", qn_body, [cq], [wd["w_qn"]], tm=tm, tn=TILES.cols_low_rank, out_dtypes=[BF16],
             cols=[wd["g_qn"]])
    qr = _mm("q_rope", qr_body, [cq, cq], [wd["w_qr"], wd["w_qr_swap"]], tm=tm, tn=TILES.cols_rope, out_dtypes=[BF16],
             group_sizes=(1, 1), cols=[wd["g_qr"], wd["g_qr_swap"]],
             extras=[(c_tab, tab_spec), (sa_tab + sb_tab, tab_spec)])

    def f_body(acc, lb):
        s = jax.nn.sigmoid(acc)
        log2_f = jnp.log2(lb + (1.0 - lb) * s)
        return log2_f, (1.0 - lb) * (1.0 - s)

    ident = lambda acc: acc
    hw = d["hg_width"]
    proj = functools.partial(_mm, lhs=[h], rhs=[wd["w_hgrn"]], tm=tm, tn=TILES.cols_wide, n=hw)
    hq = proj("hgrn_q", ident, out_dtypes=[BF16], rhs_col0=0)
    lf, hk = proj("hgrn_f", f_body, out_dtypes=[F32, BF16], cols=[wd["lb"]], rhs_col0=hw)
    hv = proj("hgrn_v", ident, out_dtypes=[BF16], rhs_col0=2 * hw)
    hg = proj("hgrn_g", ident, out_dtypes=[BF16], rhs_col0=3 * hw)
    return (qn, qr, ckv, ckv_b, kr, kr_b), (hq, hk, hv, lf, hg)


def _out_proj(x, o_mla, o_hg, wd, *, tm):
    return _mm("out_proj", lambda acc, res: res + acc, [o_mla, o_hg], [wd["w_o"], wd["w_o"]], rhs_row_blocks=(0, 1),
               tm=tm, tn=TILES.cols_wide, out_dtypes=[F32], rows=[x])


def _down_ple(x, act, p, wd, *, tm):
    x, xg, ssq = _mm("ffn_down", lambda acc, res: res + acc, [act], [wd["w_down"]], tm=min(tm, TILES.down_rows),
                     tn=TILES.down_cols, out_dtypes=[F32], rows=[x], next_norm_gain=wd["g_ple"])
    width = x.shape[1]

    def ple_body(gate, emb, res, ssq_blk):
        return res + emb * jax.nn.sigmoid(gate * _row_rinv(ssq_blk, width))

    return _mm("ple", ple_body, [xg, p.astype(BF16)], [wd["w_ple_gate"], wd["w_ple"]], tm=tm, tn=TILES.cols,
               out_dtypes=[F32], group_sizes=(1, 1), rows=[x], extras=[(ssq, _ssq_spec(min(tm, x.shape[0])))])


def _prompt_layer(x, p, wd, *, batch, seq):
    d = wd["dims"]
    tm = min(TILES.rows, seq)
    assert seq % tm == 0
    tabs = _rope_tables(jnp.arange(seq), d["rope"])
    (qn, qr, ckv, ckv_b, kr, kr_b), (hq, hk, hv, lf, hg) = _mixer_inputs(x, wd, tabs, tm=tm)

    def kn_body(acc, g):
        return jnp.concatenate(_slab_rms(acc, g, d["nope"]), axis=-1)

    kn = _mm("k_nope", kn_body, [ckv_b], [wd["w_uk"]], tm=tm, tn=TILES.cols_low_rank, out_dtypes=[BF16],
             cols=[wd["g_kn"]])
    tq = min(TILES.attn, seq)
    vt = _value_t(ckv_b, wd["w_uv_t"], tk=tq, tn=TILES.cols_low_rank)
    o_mla = _attn_prompt(qn, qr, kn, kr_b, vt, batch=batch, seq=seq, heads=d["heads"], tq=tq, hb=TILES.attn_heads)

    hg_heads = d["hg_width"] // LANES
    s0 = jnp.zeros((batch, hg_heads, LANES, LANES), F32)
    o_hg, s_fin = _hgrn(hq, hk, hv, lf, hg, s0, wd["g_hg_out"], batch=batch, t_len=seq, blk=CHUNK,
                        tt=TILES.hgrn_rows, hb=TILES.hgrn_heads)

    x = _out_proj(x, o_mla, o_hg, wd, tm=tm)
    h = _rmsnorm(x, wd["g_ffn"], TILES.norm_rows)
    tm_up = min(TILES.ff_rows, seq)
    act, tail_g, tail_u = _up_conv_prompt(h, wd["w_up"], wd["conv_w"], wd["conv_b"], seq=seq, tm=tm_up,
                                          cm=TILES.ff_chunk)
    last = lambda t: t.reshape(batch, seq // tm_up, 8, -1)[:, -1, 6:, :]
    conv_state = jnp.concatenate([last(tail_g), last(tail_u)], axis=-1)
    x = _down_ple(x, act, p, wd, tm=tm)
    return x, ckv.reshape(batch, seq, -1), kr.reshape(batch, seq, -1), s_fin, conv_state


def _sample_layer(x, p, cache_ckv, cache_kr, state_hgrn, hist, wd, *, batch, t_new):
    d = wd["dims"]
    m = batch * t_new
    past = cache_ckv.shape[1]
    tabs = _rope_tables(jnp.tile(past + jnp.arange(t_new), batch), d["rope"])
    (qn, qr, ckv, ckv_b, kr, kr_b), (hq, hk, hv, lf, hg) = _mixer_inputs(x, wd, tabs, tm=m)

    qt = _per_head_mm("mla_absorb_q", qn, wd["w_absorb"])
    qr_h = jnp.transpose(qr.reshape(m, d["heads"], LANES), (1, 0, 2))
    o_lat = _attn_sample(qt, qr_h, cache_ckv, cache_kr, ckv_b, kr_b, wd["w_uk_t"], batch=batch, t_new=t_new,
                         heads=d["heads"], nope=d["nope"], rope=d["rope"], tk=TILES.sample_keys)
    o_mla = _value_up(o_lat, wd["w_uv_h"])

    o_hg, s_new = _hgrn(hq, hk, hv, lf, hg, state_hgrn, wd["g_hg_out"], batch=batch, t_len=t_new, blk=t_new,
                        tt=t_new, hb=TILES.hgrn_heads_short)

    x = _out_proj(x, o_mla, o_hg, wd, tm=m)
    h = _rmsnorm(x, wd["g_ffn"], TILES.norm_rows)
    d_ff = d["d_ff"]
    u = _mm("ffn_up_sample", lambda acc: acc, [h], [wd["w_up"]], tm=m, tn=2 * _ff_tile(d_ff), out_dtypes=[F32])
    u3 = u.reshape(batch, t_new, 2 * d_ff)
    act = _conv_sample(u3, hist, wd["conv_w"], wd["conv_b"])
    taps = wd["conv_w"].shape[0]
    conv_state = jnp.concatenate([hist, u3], axis=1)[:, -(taps - 1):]
    x = _down_ple(x, act.reshape(m, d_ff).astype(BF16), p, wd, tm=m)
    return x, ckv.reshape(batch, t_new, -1), kr.reshape(batch, t_new, -1), s_new, conv_state


def kernel(x_prompt, x_sample, cache_ckv, cache_k_rope, state_hgrn, state_ffn_conv, p_prompt, p_sample,
           g_attn_norm, w_in, g_q_a, w_uq, g_kv_a, w_uk, w_uv, g_q_nope, g_q_rope, g_k_nope, g_k_rope,
           lb_logits, g_hg_out, w_o, g_ffn_norm, w_up, conv_w, conv_b, w_down, g_ple_norm, w_ple_gate, w_ple):
    depth = w_in.shape[0]
    batch, seq, d_model = x_prompt.shape
    dec_batch, t_new, _ = x_sample.shape
    assert t_new >= conv_w.shape[1] - 1
    lb_all = jnp.cumsum(jax.nn.softmax(lb_logits.astype(F32), axis=0), axis=0)
    xp = x_prompt.reshape(batch * seq, d_model)
    xs = x_sample.reshape(dec_batch * t_new, d_model)
    outs = [[] for _ in range(8)]
    for i in range(depth):
        wd = _layer_weights(i, lb_all[i], g_attn_norm, w_in, g_q_a, w_uq, g_kv_a, w_uk, w_uv, g_q_nope, g_q_rope,
                            g_k_nope, g_k_rope, g_hg_out, w_o, g_ffn_norm, w_up, conv_w, conv_b, w_down,
                            g_ple_norm, w_ple_gate, w_ple)
        xp, *new_p = _prompt_layer(xp, p_prompt[i].reshape(batch * seq, -1), wd, batch=batch, seq=seq)
        xs, *new_s = _sample_layer(xs, p_sample[i].reshape(dec_batch * t_new, -1), cache_ckv[i], cache_k_rope[i],
                                   state_hgrn[i], state_ffn_conv[i], wd, batch=dec_batch, t_new=t_new)
        for lst, val in zip(outs, new_p + new_s):
            lst.append(val)
    return (xp.reshape(batch, seq, d_model), xs.reshape(dec_batch, t_new, d_model), *[jnp.stack(o) for o in outs])
```

```python
import functools
from typing import NamedTuple

import numpy as np
import jax
import jax.numpy as jnp
from jax import lax
from jax.experimental import pallas as pl
from jax.experimental.pallas import tpu as pltpu

F32 = jnp.float32
BF16 = jnp.bfloat16

EPS = 1e-6
CHUNK = 64
ROPE_THETA = 10000.0
LANES = 128
NEG_BIG = -1e30
VMEM_LIMIT_BYTES = 56 * 1024 * 1024


class _Tiles(NamedTuple):
    rows: int = 1024
    cols: int = 512
    cols_wide: int = 1024
    cols_low_rank: int = 2048
    cols_rope: int = 2048
    down_rows: int = 512
    down_cols: int = 512
    norm_rows: int = 512
    latent_rows: int = 512
    attn: int = 512
    attn_heads: int = 8
    hgrn_rows: int = 512
    hgrn_heads: int = 16
    hgrn_heads_short: int = 16
    ff_rows: int = 2048
    ff_chunk: int = 512
    sample_keys: int = 1024


TILES = _Tiles()


def _params(*sem):
    return pltpu.CompilerParams(dimension_semantics=sem, vmem_limit_bytes=VMEM_LIMIT_BYTES)


def _dot(a, b):
    return jnp.dot(a, b, preferred_element_type=F32)


def _dot_nt(a, b):
    return lax.dot_general(a, b, (((1,), (1,)), ((), ())), preferred_element_type=F32)


def _dot_tn(a, b):
    return lax.dot_general(a, b, (((0,), (0,)), ((), ())), preferred_element_type=F32)


def _rms(x, g):
    ms = jnp.mean(x * x, axis=-1, keepdims=True)
    return x * lax.rsqrt(ms + EPS) * g


def _slab_rms(x, g, valid):
    outs = []
    for a in range(x.shape[1] // LANES):
        blk = x[:, a * LANES:(a + 1) * LANES]
        ms = jnp.sum(blk * blk, axis=-1, keepdims=True) * (1.0 / valid)
        outs.append(blk * lax.rsqrt(ms + EPS) * g[:, a * LANES:(a + 1) * LANES])
    return outs


def _rope_slab(y, c, sa, sb):
    return y * c + pltpu.roll(y, 96, 1) * sa + pltpu.roll(y, 32, 1) * sb


def _rmsnorm_kernel(x_ref, g_ref, o_ref):
    o_ref[...] = _rms(x_ref[...], g_ref[...]).astype(o_ref.dtype)


def _rmsnorm(x, g, tm):
    m, d = x.shape
    tm = min(tm, m)
    assert m % tm == 0
    return pl.pallas_call(
        _rmsnorm_kernel,
        grid=(m // tm,),
        in_specs=[pl.BlockSpec((tm, d), lambda i: (i, 0)), pl.BlockSpec((1, d), lambda i: (0, 0))],
        out_specs=pl.BlockSpec((tm, d), lambda i: (i, 0)),
        out_shape=jax.ShapeDtypeStruct((m, d), BF16),
        compiler_params=_params("parallel"),
        name="rmsnorm",
    )(x, g.reshape(1, d))


def _lane_fold(x):
    out = x[:, :LANES]
    for a in range(1, x.shape[1] // LANES):
        out = out + x[:, a * LANES:(a + 1) * LANES]
    return out


def _row_rinv(ssq, width):
    return lax.rsqrt(jnp.sum(ssq, axis=-1, keepdims=True) * (1.0 / width) + EPS)


def _mm_kernel(*refs, body, n_lhs, group_sizes, n_in, next_norm):
    ins, outs = refs[:n_in], refs[n_in:]
    lhs, rhs, rest = ins[:n_lhs], ins[n_lhs:2 * n_lhs], ins[2 * n_lhs:]
    accs, k = [], 0
    for gs in group_sizes:
        acc = None
        for _ in range(gs):
            d = _dot(lhs[k][...], rhs[k][...])
            acc = d if acc is None else acc + d
            k += 1
        accs.append(acc)
    if next_norm:
        *rest, gain_ref = rest
        *outs, scaled_ref, ssq_ref = outs
    res = body(*accs, *[r[...] for r in rest])
    if not isinstance(res, (tuple, list)):
        res = (res,)
    for o, r in zip(outs, res):
        o[...] = r.astype(o.dtype)
    if next_norm:
        x = res[0]
        scaled_ref[...] = (x * gain_ref[...]).astype(scaled_ref.dtype)
        @pl.when(pl.program_id(1) == 0)
        def _():
            ssq_ref[...] = jnp.zeros(ssq_ref.shape, F32)

        ssq_ref[...] += _lane_fold(x * x)


def _mm(name, body, lhs, rhs, *, tm, tn, out_dtypes, group_sizes=None, cols=(), rows=(), extras=(),
        next_norm_gain=None, n=None, rhs_col0=0, rhs_row_blocks=None):
    m = lhs[0].shape[0]
    n = n or rhs[0].shape[1]
    tm, tn = min(tm, m), min(tn, n)
    assert m % tm == 0 and n % tn == 0 and rhs_col0 % tn == 0, (name, m, n, tm, tn)
    j0 = rhs_col0 // tn
    group_sizes = tuple(group_sizes or (len(lhs),))
    rhs_row_blocks = rhs_row_blocks or (0,) * len(rhs)
    tile = pl.BlockSpec((tm, tn), lambda i, j: (i, j))
    in_specs = [pl.BlockSpec((tm, a.shape[1]), lambda i, j: (i, 0)) for a in lhs]
    in_specs += [pl.BlockSpec((a.shape[1], tn), lambda i, j, rb=rb: (rb, j + j0)) for a, rb in zip(lhs, rhs_row_blocks)]
    in_specs += [pl.BlockSpec((c.shape[0], tn), lambda i, j: (0, j)) for c in cols]
    in_specs += [tile for _ in rows]
    in_specs += [spec for _, spec in extras]
    args = list(lhs) + list(rhs) + list(cols) + list(rows) + [a for a, _ in extras]
    out_specs = [tile for _ in out_dtypes]
    out_shape = [jax.ShapeDtypeStruct((m, n), dt) for dt in out_dtypes]
    if next_norm_gain is not None:
        in_specs.append(pl.BlockSpec((1, tn), lambda i, j: (0, j)))
        args.append(next_norm_gain.reshape(1, n))
        out_specs += [tile, pl.BlockSpec((tm, LANES), lambda i, j: (i, 0))]
        out_shape += [jax.ShapeDtypeStruct((m, n), BF16), jax.ShapeDtypeStruct((m, LANES), F32)]
    kern = functools.partial(_mm_kernel, body=body, n_lhs=len(lhs), group_sizes=group_sizes, n_in=len(args),
                             next_norm=next_norm_gain is not None)
    outs = pl.pallas_call(
        kern,
        grid=(m // tm, n // tn),
        in_specs=in_specs,
        out_specs=out_specs,
        out_shape=out_shape,
        compiler_params=_params("parallel", "arbitrary"),
        name=name,
    )(*args)
    return outs if len(outs) > 1 else outs[0]


def _ssq_spec(tm):
    return pl.BlockSpec((tm, LANES), lambda i, j: (i, 0))


def _latent_kernel(h_ref, w_ref, gq_ref, gkv_ref, gkr_ref, c_ref, sa_ref, sb_ref,
                   cq_ref, ckv_ref, ckvb_ref, kr_ref, krb_ref, *, q_lora, kv_lora, rope):
    z = _dot(h_ref[...], w_ref[...])
    cq_ref[...] = _rms(z[:, :q_lora], gq_ref[...]).astype(cq_ref.dtype)
    ckv = _rms(z[:, q_lora:q_lora + kv_lora], gkv_ref[...])
    ckv_ref[...] = ckv
    ckvb_ref[...] = ckv.astype(ckvb_ref.dtype)
    (kr,) = _slab_rms(z[:, q_lora + kv_lora:], gkr_ref[...], rope)
    kr = _rope_slab(kr, c_ref[...], sa_ref[...], sb_ref[...])
    kr_ref[...] = kr[:, :rope]
    krb_ref[...] = kr.astype(krb_ref.dtype)


def _latent(h, w_lat, g_q_a, g_kv_a, g_k_rope_pad, tabs, *, q_lora, kv_lora, rope, tm):
    m, d = h.shape
    tm = min(tm, m)
    n = w_lat.shape[1]
    n_tab = tabs[0].shape[0] // tm
    row = lambda w: pl.BlockSpec((tm, w), lambda i: (i, 0))
    const = lambda r, w: pl.BlockSpec((r, w), lambda i: (0, 0))
    tab = pl.BlockSpec((tm, LANES), lambda i: (i % n_tab, 0))
    kern = functools.partial(_latent_kernel, q_lora=q_lora, kv_lora=kv_lora, rope=rope)
    return pl.pallas_call(
        kern,
        grid=(m // tm,),
        in_specs=[row(d), const(d, n), const(1, q_lora), const(1, kv_lora), const(1, LANES), tab, tab, tab],
        out_specs=[row(q_lora), row(kv_lora), row(kv_lora), row(rope), row(LANES)],
        out_shape=[jax.ShapeDtypeStruct((m, q_lora), BF16), jax.ShapeDtypeStruct((m, kv_lora), F32),
                   jax.ShapeDtypeStruct((m, kv_lora), BF16), jax.ShapeDtypeStruct((m, rope), F32),
                   jax.ShapeDtypeStruct((m, LANES), BF16)],
        compiler_params=_params("parallel"),
        name="mla_latent",
    )(h, w_lat, g_q_a.reshape(1, -1), g_kv_a.reshape(1, -1), g_k_rope_pad, *tabs)


def _attn_kernel(qn_ref, qr_ref, kn_ref, kr_ref, vt_ref, o_ref, m_scr, l_scr, acc_scr, *, tq, chunk, hb):
    qi = pl.program_id(2)
    lanes = lambda h: slice(h * LANES, (h + 1) * LANES)
    qs = [jnp.concatenate([qn_ref[:, lanes(h)], qr_ref[:, lanes(h)]], axis=-1) for h in range(hb)]
    m_scr[...] = jnp.full(m_scr.shape, NEG_BIG, F32)
    l_scr[...] = jnp.zeros(l_scr.shape, F32)
    acc_scr[...] = jnp.zeros(acc_scr.shape, F32)

    def block(kb, masked):
        off = pl.multiple_of(kb * tq, tq)
        kr = kr_ref[pl.ds(off, tq), :]
        scores = []
        for h in range(hb):
            k = jnp.concatenate([kn_ref[pl.ds(off, tq), lanes(h)], kr], axis=-1)
            s = _dot_nt(k, qs[h])
            if masked:
                kc = lax.broadcasted_iota(jnp.int32, (tq, tq), 0) // chunk
                qc = lax.broadcasted_iota(jnp.int32, (tq, tq), 1) // chunk
                s = jnp.where(kc <= qc, s, NEG_BIG)
            scores.append(s)
        probs = []
        for h in range(hb):
            m_i = m_scr[h]
            m_new = jnp.maximum(m_i, jnp.max(scores[h], axis=0, keepdims=True))
            alpha = jnp.exp2(m_i - m_new)
            p = jnp.exp2(scores[h] - m_new)
            l_scr[h] = alpha * l_scr[h] + jnp.sum(p, axis=0, keepdims=True)
            m_scr[h] = m_new
            probs.append((alpha, p.astype(BF16)))
        for h in range(hb):
            alpha, p = probs[h]
            acc_scr[h] = alpha * acc_scr[h] + _dot(vt_ref[kb, lanes(h), :], p)

    def body(kb, carry):
        block(kb, False)
        return carry

    lax.fori_loop(0, qi, body, 0)
    block(qi, True)
    for h in range(hb):
        o_ref[:, lanes(h)] = (acc_scr[h] / l_scr[h]).T.astype(o_ref.dtype)


def _attn_prompt(qn, qr, kn, krb, vt, *, batch, seq, heads, tq, hb):
    nq = seq // tq
    hb = min(hb, heads)
    assert heads % hb == 0
    qspec = pl.BlockSpec((tq, hb * LANES), lambda b, h, i: (b * nq + i, h))
    once = dict(pipeline_mode=pl.Buffered(1))
    kspec = pl.BlockSpec((seq, hb * LANES), lambda b, h, i: (b, h), **once)
    kern = functools.partial(_attn_kernel, tq=tq, chunk=CHUNK, hb=hb)
    return pl.pallas_call(
        kern,
        grid=(batch, heads // hb, nq),
        in_specs=[qspec, qspec, kspec, pl.BlockSpec((seq, LANES), lambda b, h, i: (b, 0), **once),
                  pl.BlockSpec((nq, hb * LANES, tq), lambda b, h, i: (b, h, 0), **once)],
        out_specs=qspec,
        out_shape=jax.ShapeDtypeStruct(qn.shape, BF16),
        scratch_shapes=[pltpu.VMEM((hb, 1, tq), F32), pltpu.VMEM((hb, 1, tq), F32),
                        pltpu.VMEM((hb, LANES, tq), F32)],
        compiler_params=_params("parallel", "parallel", "arbitrary"),
        name="mla_attn_prompt",
    )(qn, qr, kn, krb, vt)


def _value_t_kernel(w_ref, x_ref, o_ref):
    o_ref[0] = _dot_nt(w_ref[...], x_ref[...]).astype(o_ref.dtype)


def _value_t(x, w_t, *, tk, tn):
    m, c = x.shape
    n = w_t.shape[0]
    tn = min(tn, n)
    return pl.pallas_call(
        _value_t_kernel,
        grid=(m // tk, n // tn),
        in_specs=[pl.BlockSpec((tn, c), lambda i, j: (j, 0)), pl.BlockSpec((tk, c), lambda i, j: (i, 0))],
        out_specs=pl.BlockSpec((1, tn, tk), lambda i, j: (i, j, 0)),
        out_shape=jax.ShapeDtypeStruct((m // tk, n, tk), BF16),
        compiler_params=_params("parallel", "arbitrary"),
        name="v_up_t",
    )(w_t, x)


def _absorb_kernel(x_ref, w_ref, o_ref):
    o_ref[0] = _dot(x_ref[...], w_ref[0]).astype(o_ref.dtype)


def _per_head_mm(name, x, w):
    m = x.shape[0]
    heads, _, n = w.shape
    return pl.pallas_call(
        _absorb_kernel,
        grid=(heads,),
        in_specs=[pl.BlockSpec((m, LANES), lambda h: (0, h)), pl.BlockSpec((1, LANES, n), lambda h: (h, 0, 0))],
        out_specs=pl.BlockSpec((1, m, n), lambda h: (h, 0, 0)),
        out_shape=jax.ShapeDtypeStruct((heads, m, n), BF16),
        compiler_params=_params("parallel"),
        name=name,
    )(x, w)


def _vup_kernel(x_ref, w_ref, o_ref):
    o_ref[...] = _dot(x_ref[0], w_ref[0]).astype(o_ref.dtype)


def _value_up(o_lat, w):
    heads, m, c = o_lat.shape
    return pl.pallas_call(
        _vup_kernel,
        grid=(heads,),
        in_specs=[pl.BlockSpec((1, m, c), lambda h: (h, 0, 0)), pl.BlockSpec((1, c, LANES), lambda h: (h, 0, 0))],
        out_specs=pl.BlockSpec((m, LANES), lambda h: (0, h)),
        out_shape=jax.ShapeDtypeStruct((m, heads * LANES), BF16),
        compiler_params=_params("parallel"),
        name="mla_value_up",
    )(o_lat, w)


def _attn_sample_kernel(qt_ref, qr_ref, cache_ref, kcache_ref, cnew_ref, knew_ref, wkt_ref, o_ref,
                        *, heads, t_new, past, tk, nope, rope):
    hq = heads * t_new
    c_lat = cache_ref.shape[-1]
    qt = qt_ref[...].reshape(hq, c_lat)
    qr = qr_ref[...].reshape(hq, LANES)[:, :rope]
    wkt = wkt_ref[...]

    def scores(c_b, kr_b):
        n = c_b.shape[0]
        kn_t = _dot_nt(wkt, c_b)
        ssq = jnp.sum((kn_t * kn_t).reshape(heads, nope, n), axis=1)
        rinv = lax.rsqrt(ssq * (1.0 / nope) + EPS)
        rinv = jnp.broadcast_to(rinv[:, None, :], (heads, t_new, n)).reshape(hq, n)
        return _dot_nt(qt, c_b) * rinv + _dot_nt(qr, kr_b)

    c_new = cnew_ref[...]
    s_new = scores(c_new, knew_ref[...][:, :rope])
    m_i = jnp.max(s_new, axis=-1, keepdims=True)
    p = jnp.exp2(s_new - m_i)
    l_i = jnp.sum(p, axis=-1, keepdims=True)
    acc = _dot(p.astype(BF16), c_new)
    for kt in range(past // tk):
        c_b = cache_ref[0, kt * tk:(kt + 1) * tk, :].astype(BF16)
        s = scores(c_b, kcache_ref[0, kt * tk:(kt + 1) * tk, :].astype(BF16))
        m_new = jnp.maximum(m_i, jnp.max(s, axis=-1, keepdims=True))
        alpha = jnp.exp2(m_i - m_new)
        p = jnp.exp2(s - m_new)
        l_i = alpha * l_i + jnp.sum(p, axis=-1, keepdims=True)
        acc = alpha * acc + _dot(p.astype(BF16), c_b)
        m_i = m_new
    o_ref[...] = (acc / l_i).reshape(heads, t_new, c_lat).astype(o_ref.dtype)


def _attn_sample(qt, qr_h, cache, kcache, ckv_new, kr_new, wkt, *, batch, t_new, heads, nope, rope, tk):
    past, c_lat = cache.shape[1], cache.shape[2]
    tk = min(tk, past)
    kern = functools.partial(_attn_sample_kernel, heads=heads, t_new=t_new, past=past, tk=tk, nope=nope, rope=rope)
    return pl.pallas_call(
        kern,
        grid=(batch,),
        in_specs=[pl.BlockSpec((heads, t_new, c_lat), lambda b: (0, b, 0)),
                  pl.BlockSpec((heads, t_new, LANES), lambda b: (0, b, 0)),
                  pl.BlockSpec((1, past, c_lat), lambda b: (b, 0, 0)),
                  pl.BlockSpec((1, past, rope), lambda b: (b, 0, 0)),
                  pl.BlockSpec((t_new, c_lat), lambda b: (b, 0)),
                  pl.BlockSpec((t_new, LANES), lambda b: (b, 0)),
                  pl.BlockSpec(wkt.shape, lambda b: (0, 0))],
        out_specs=pl.BlockSpec((heads, t_new, c_lat), lambda b: (0, b, 0)),
        out_shape=jax.ShapeDtypeStruct((heads, batch * t_new, c_lat), BF16),
        compiler_params=_params("parallel"),
        name="mla_attn_sample",
    )(qt, qr_h, cache, kcache, ckv_new, kr_new, wkt)


def _hgrn_tables(blk):
    t = np.arange(blk)
    levels = [blk >> (i + 1) for i in range(int(np.log2(blk)))]
    lvl = np.full((blk, blk), -1, np.int32)
    lvl[t, t] = 0
    for li, c in enumerate(levels, start=1):
        same_pair = (t[:, None] // (2 * c)) == (t[None, :] // (2 * c))
        split = (t[:, None] // c) != (t[None, :] // c)
        lvl[same_pair & split & (t[:, None] > t[None, :])] = li
    prefix = (t[None, :] <= t[:, None]).astype(np.float32)
    rows = [prefix]
    for c in (2, 1):
        mid = (t // (2 * c)) * (2 * c) + c - 1
        rows.append((t[None, :] <= mid[:, None]).astype(np.float32))
    return levels, jnp.asarray(lvl), jnp.asarray(np.concatenate(rows, axis=0), dtype=BF16)


def _hgrn_block(q_ref, k_ref, v_ref, lf_ref, g_ref, o_ref, st_scr, rows, sel, pair_masks, upper_masks, signs,
                g_out, *, blk, levels, hb):
    cols = lambda h: slice(h * LANES, (h + 1) * LANES)
    sums = []
    for h in range(hb):
        lf = lf_ref[rows, cols(h)]
        hi = lf.astype(BF16)
        lo = (lf - hi.astype(F32)).astype(BF16)
        both = _dot(sel, jnp.concatenate([hi, lo], axis=-1))
        sums.append(both[:, :LANES] + both[:, LANES:])
    intra = []
    for h in range(hb):
        b = sums[h][:blk]
        q, k = q_ref[rows, cols(h)], k_ref[rows, cols(h)]
        qf, kf = q.astype(F32), k.astype(F32)
        a = jnp.where(pair_masks[0], _dot_nt(q, k), 0.0)
        for li, c in enumerate(levels, start=1):
            if c == 2:
                ref = sums[h][blk:2 * blk]
            elif c == 1:
                ref = sums[h][2 * blk:]
            else:
                b3 = b.reshape(blk // (2 * c), 2 * c, LANES)
                ref = jnp.broadcast_to(b3[:, c - 1:c, :], b3.shape).reshape(blk, LANES)
            decay = jnp.exp2((b - ref) * signs[li - 1])
            w = (jnp.where(upper_masks[li - 1], qf, kf) * decay).astype(BF16)
            a = jnp.where(pair_masks[li], _dot_nt(w, w), a)
        intra.append(a.astype(BF16))
    for h in range(hb):
        b = sums[h][:blk]
        b_last = b[blk - 1:blk, :]
        qf, kf = q_ref[rows, cols(h)].astype(F32), k_ref[rows, cols(h)].astype(F32)
        v = v_ref[rows, cols(h)]
        s_t = st_scr[h]
        inter = _dot_nt((qf * jnp.exp2(b)).astype(BF16), s_t.astype(BF16))
        o = inter + _dot(intra[h], v)
        k_t = (kf * jnp.exp2(b_last - b)).astype(BF16)
        st_scr[h] = s_t * jnp.exp2(b_last) + _dot_tn(v, k_t)
        gf = g_ref[rows, cols(h)].astype(F32)
        o_ref[rows, cols(h)] = (_rms(o, g_out) * (gf * jax.nn.sigmoid(gf))).astype(o_ref.dtype)


def _hgrn_kernel(q_ref, k_ref, v_ref, lf_ref, g_ref, s0_ref, sel_ref, lvl_ref, gout_ref,
                 o_ref, s_out_ref, st_scr, *, blk, levels, hb, n_blk):
    tb = pl.program_id(2)

    @pl.when(tb == 0)
    def _():
        for h in range(hb):
            st_scr[h] = s0_ref[0, h].T

    sel, lvl, g_out = sel_ref[...], lvl_ref[...], gout_ref[...]
    pair_masks = [lvl == li for li in range(len(levels) + 1)]
    row = lax.broadcasted_iota(jnp.int32, (blk, LANES), 0)
    upper_masks = [(row // c) % 2 == 1 for c in levels]
    signs = [jnp.where(u, 1.0, -1.0) for u in upper_masks]

    def step(c, carry):
        rows = pl.ds(pl.multiple_of(c * blk, blk), blk)
        _hgrn_block(q_ref, k_ref, v_ref, lf_ref, g_ref, o_ref, st_scr, rows, sel, pair_masks, upper_masks, signs,
                    g_out, blk=blk, levels=levels, hb=hb)
        return carry

    lax.fori_loop(0, n_blk, step, 0)

    @pl.when(tb == pl.num_programs(2) - 1)
    def _():
        for h in range(hb):
            s_out_ref[0, h] = st_scr[h].T


def _hgrn(q, k, v, lf, gate, s0, g_out, *, batch, t_len, blk, tt, hb):
    heads = q.shape[1] // LANES
    tt, hb = min(tt, t_len), min(hb, heads)
    levels, lvl, sel = _hgrn_tables(blk)
    nt = t_len // tt
    xspec = pl.BlockSpec((tt, hb * LANES), lambda b, h, t: (b * nt + t, h))
    sspec = pl.BlockSpec((1, hb, LANES, LANES), lambda b, h, t: (b, h, 0, 0))
    const = lambda a: pl.BlockSpec(a.shape, lambda b, h, t: (0, 0))
    kern = functools.partial(_hgrn_kernel, blk=blk, levels=levels, hb=hb, n_blk=tt // blk)
    g_out = g_out.reshape(1, LANES)
    return pl.pallas_call(
        kern,
        grid=(batch, heads // hb, nt),
        in_specs=[xspec, xspec, xspec, xspec, xspec, sspec, const(sel), const(lvl), const(g_out)],
        out_specs=[xspec, sspec],
        out_shape=[jax.ShapeDtypeStruct(q.shape, BF16), jax.ShapeDtypeStruct(s0.shape, F32)],
        scratch_shapes=[pltpu.VMEM((hb, LANES, LANES), F32)],
        compiler_params=_params("parallel", "parallel", "arbitrary"),
        name="hgrn2",
    )(q, k, v, lf, gate, s0, sel, lvl, g_out)


def _silu(x):
    h = 0.5 * x
    return h + h * jnp.tanh(h)


def _ff_tile(d_ff):
    assert d_ff % LANES == 0
    return 2 * LANES if d_ff % (2 * LANES) == 0 else LANES


def _up_conv_kernel(h_ref, wg_ref, wu_ref, cwg_ref, cwu_ref, cbg_ref, cbu_ref,
                    act_ref, tailg_ref, tailu_ref, ug_scr, uu_scr, carry_scr, *, tm, cm, tiles_per_seq):
    i, j = pl.program_id(0), pl.program_id(1)
    seq_start = (i % tiles_per_seq) == 0
    halves = ((wg_ref, cwg_ref, cbg_ref, tailg_ref, ug_scr), (wu_ref, cwu_ref, cbu_ref, tailu_ref, uu_scr))
    for half, (_, _, _, _, scr) in enumerate(halves):
        @pl.when(seq_start)
        def _(scr=scr):
            scr[0:8, :] = jnp.zeros((8, scr.shape[1]), F32)

        @pl.when(jnp.logical_not(seq_start))
        def _(scr=scr, half=half):
            scr[0:8, :] = carry_scr[half, j]

    for r in range(tm // cm):
        h = h_ref[r * cm:(r + 1) * cm, :]
        conv = []
        for w_ref, cw_ref, cb_ref, _, scr in halves:
            u = _dot(h, w_ref[...])
            scr[8 + r * cm:8 + (r + 1) * cm, :] = u
            cw = cw_ref[...]
            conv.append(cb_ref[...] + cw[2:3] * u + cw[1:2] * scr[7 + r * cm:7 + (r + 1) * cm, :]
                        + cw[0:1] * scr[6 + r * cm:6 + (r + 1) * cm, :])
        act_ref[r * cm:(r + 1) * cm, :] = (_silu(conv[0]) * conv[1]).astype(act_ref.dtype)

    for half, (_, _, _, tail_ref, scr) in enumerate(halves):
        tail = scr[tm:tm + 8, :]
        carry_scr[half, j] = tail
        tail_ref[...] = tail


def _up_conv_prompt(h, w_up, conv_w, conv_b, *, seq, tm, cm):
    m, d = h.shape
    n = w_up.shape[1] // 2
    tn = _ff_tile(n)
    tm = min(tm, seq)
    cm = min(cm, tm)
    assert seq % tm == 0 and tm % cm == 0 and conv_w.shape[0] == 3
    nj = n // tn
    gate = lambda r: pl.BlockSpec((r, tn), lambda i, j: (0, j))
    up = lambda r: pl.BlockSpec((r, tn), lambda i, j: (0, j + nj))
    tail = pl.BlockSpec((8, tn), lambda i, j: (i, j))
    kern = functools.partial(_up_conv_kernel, tm=tm, cm=cm, tiles_per_seq=seq // tm)
    return pl.pallas_call(
        kern,
        grid=(m // tm, nj),
        in_specs=[pl.BlockSpec((tm, d), lambda i, j: (i, 0)), gate(d), up(d), gate(3), up(3), gate(1), up(1)],
        out_specs=[pl.BlockSpec((tm, tn), lambda i, j: (i, j)), tail, tail],
        out_shape=[jax.ShapeDtypeStruct((m, n), BF16), jax.ShapeDtypeStruct((m // tm * 8, n), F32),
                   jax.ShapeDtypeStruct((m // tm * 8, n), F32)],
        scratch_shapes=[pltpu.VMEM((tm + 8, tn), F32), pltpu.VMEM((tm + 8, tn), F32),
                        pltpu.VMEM((2, nj, 8, tn), F32)],
        compiler_params=_params("arbitrary", "arbitrary"),
        name="ffn_up_conv",
    )(h, w_up, w_up, conv_w, conv_w, conv_b, conv_b)


def _conv_sample_kernel(ug_ref, uu_ref, hg_ref, hu_ref, cwg_ref, cwu_ref, cbg_ref, cbu_ref, act_ref, *, t_len, taps):
    conv = [[None] * t_len, [None] * t_len]
    for half, (u_ref, hist_ref, cw_ref, cb_ref) in enumerate(
            ((ug_ref, hg_ref, cwg_ref, cbg_ref), (uu_ref, hu_ref, cwu_ref, cbu_ref))):
        cw, cb = cw_ref[...], cb_ref[...]
        rows = [hist_ref[:, r, :] for r in range(taps - 1)] + [u_ref[:, t, :] for t in range(t_len)]
        for t in range(t_len):
            c = cb
            for tap in range(taps):
                c = c + cw[tap:tap + 1] * rows[t + tap]
            conv[half][t] = c
    for t in range(t_len):
        act_ref[:, t, :] = _silu(conv[0][t]) * conv[1][t]


def _conv_sample(u, hist, conv_w, conv_b):
    bsz, t_len, n2 = u.shape
    n = n2 // 2
    tn = _ff_tile(n)
    nj = n // tn
    taps = conv_w.shape[0]
    gate = lambda *lead: pl.BlockSpec((*lead, tn), lambda j: (*([0] * len(lead)), j))
    up = lambda *lead: pl.BlockSpec((*lead, tn), lambda j: (*([0] * len(lead)), j + nj))
    kern = functools.partial(_conv_sample_kernel, t_len=t_len, taps=taps)
    return pl.pallas_call(
        kern,
        grid=(nj,),
        in_specs=[gate(bsz, t_len), up(bsz, t_len), gate(bsz, taps - 1), up(bsz, taps - 1),
                  gate(taps), up(taps), gate(1), up(1)],
        out_specs=gate(bsz, t_len),
        out_shape=jax.ShapeDtypeStruct((bsz, t_len, n), F32),
        compiler_params=_params("parallel"),
        name="ffn_conv_sample",
    )(u, u, hist, hist, conv_w, conv_w, conv_b, conv_b)


def _pad_cols(w, n):
    return jnp.pad(w, ((0, 0), (0, n - w.shape[1])))


def _rope_tables(pos, rope):
    half = rope // 2
    inv = 1.0 / (ROPE_THETA ** (jnp.arange(half, dtype=F32) / half))
    ang = pos.astype(F32)[:, None] * inv[None, :]
    cos, sin = jnp.cos(ang), jnp.sin(ang)
    z = jnp.zeros_like(cos)
    zz = jnp.zeros((pos.shape[0], LANES - rope), F32)
    c = jnp.concatenate([cos, cos, zz], axis=-1)
    sa = jnp.concatenate([-sin, z, zz], axis=-1)
    sb = jnp.concatenate([z, sin, zz], axis=-1)
    return c, sa, sb


def _layer_weights(i, lb, g_attn_norm, w_in, g_q_a, w_uq, g_kv_a, w_uk, w_uv, g_q_nope, g_q_rope, g_k_nope,
                   g_k_rope, g_hg_out, w_o, g_ffn_norm, w_up, conv_w, conv_b, w_down, g_ple_norm, w_ple_gate,
                   w_ple):
    q_lora, kv_lora = g_q_a.shape[1], g_kv_a.shape[1]
    nope, rope = g_q_nope.shape[1], g_q_rope.shape[1]
    heads, v_dim = w_uk.shape[2], w_uv.shape[3]
    hg_width = lb.shape[0]
    d_ff = w_down.shape[1]
    assert nope == LANES and v_dim == LANES and g_hg_out.shape[1] == LANES and 2 * rope == LANES
    scale = float((nope + rope) ** -0.5) * float(np.log2(np.e))
    o3 = q_lora + kv_lora + rope
    w_in_i = w_in[i]
    wd = {"dims": dict(q_lora=q_lora, kv_lora=kv_lora, nope=nope, rope=rope, heads=heads, hg_width=hg_width,
                       d_ff=d_ff)}
    wd["g_attn"], wd["g_ffn"], wd["g_ple"] = g_attn_norm[i], g_ffn_norm[i], g_ple_norm[i]
    wd["w_lat"] = _pad_cols(w_in_i[:, :o3], o3 + LANES - rope).astype(BF16)
    wd["g_q_a"], wd["g_kv_a"] = g_q_a[i], g_kv_a[i]
    wd["g_k_rope_pad"] = jnp.pad(g_k_rope[i], (0, LANES - rope)).reshape(1, LANES)
    wd["w_hgrn"] = w_in_i[:, o3:].astype(BF16)
    wd["lb"] = lb.reshape(1, hg_width)
    wq = w_uq[i].reshape(q_lora, heads, nope + rope)
    wd["w_qn"] = wq[:, :, :nope].reshape(q_lora, heads * nope).astype(BF16)
    half = rope // 2
    slab = lambda a, b: jnp.pad(jnp.concatenate([a, b], axis=-1), [(0, 0)] * (a.ndim - 1) + [(0, LANES - rope)])
    flat = lambda w: w.reshape(q_lora, heads * LANES).astype(BF16)
    wr1, wr2 = wq[:, :, nope:nope + half], wq[:, :, nope + half:]
    wd["w_qr"], wd["w_qr_swap"] = flat(slab(wr1, wr2)), flat(slab(wr2, wr1))
    wd["g_qn"] = (jnp.tile(g_q_nope[i], heads) * scale).reshape(1, heads * nope)
    gr = g_q_rope[i] * scale
    wd["g_qr"] = jnp.tile(slab(gr[:half], gr[half:]), heads).reshape(1, heads * LANES)
    wd["g_qr_swap"] = jnp.tile(slab(gr[half:], gr[:half]), heads).reshape(1, heads * LANES)
    wd["w_uk"] = w_uk[i].reshape(kv_lora, heads * nope).astype(BF16)
    wd["g_kn"] = jnp.tile(g_k_nope[i], heads).reshape(1, heads * nope)
    wd["w_uv_t"] = jnp.transpose(w_uv[i], (1, 2, 0)).reshape(heads * v_dim, kv_lora).astype(BF16)
    wd["w_absorb"] = jnp.transpose(w_uk[i] * g_k_nope[i][None, None, :], (1, 2, 0)).astype(BF16)
    wd["w_uk_t"] = jnp.transpose(w_uk[i], (1, 2, 0)).reshape(heads * nope, kv_lora).astype(BF16)
    wd["w_uv_h"] = jnp.transpose(w_uv[i], (1, 0, 2)).astype(BF16)
    wd["g_hg_out"] = g_hg_out[i]
    mla_width = heads * v_dim
    assert w_o.shape[1] == 2 * mla_width
    wd["w_o"] = w_o[i].astype(BF16)
    wd["w_up"], wd["conv_w"], wd["conv_b"] = w_up[i].astype(BF16), conv_w[i], conv_b[i].reshape(1, 2 * d_ff)
    wd["w_down"] = w_down[i].astype(BF16)
    wd["w_ple_gate"], wd["w_ple"] = w_ple_gate[i].astype(BF16), w_ple[i].astype(BF16)
    return wd


def _mixer_inputs(x, wd, tabs, *, tm):
    d = wd["dims"]
    h = _rmsnorm(x, wd["g_attn"], TILES.norm_rows)
    cq, ckv, ckv_b, kr, kr_b = _latent(h, wd["w_lat"], wd["g_q_a"], wd["g_kv_a"], wd["g_k_rope_pad"], tabs,
                                      q_lora=d["q_lora"], kv_lora=d["kv_lora"], rope=d["rope"],
                                      tm=TILES.latent_rows)
    m = x.shape[0]
    n_tab = tabs[0].shape[0] // min(tm, m)
    tab_spec = pl.BlockSpec((min(tm, m), LANES), lambda i, j: (i % n_tab, 0))

    def qn_body(acc, g):
        return jnp.concatenate(_slab_rms(acc, g, d["nope"]), axis=-1)

    def qr_body(acc, acc_swap, g, g_swap, c, s):
        outs = []
        for a in range(acc.shape[1] // LANES):
            sl = slice(a * LANES, (a + 1) * LANES)
            x, xs = acc[:, sl], acc_swap[:, sl]
            rinv = lax.rsqrt(jnp.sum(x * x, axis=-1, keepdims=True) * (1.0 / d["rope"]) + EPS)
            outs.append((x * rinv * g[:, sl]) * c + (xs * rinv * g_swap[:, sl]) * s)
        return jnp.concatenate(outs, axis=-1)

    c_tab, sa_tab, sb_tab = tabs
    qn = _mm("q_nope", qn_body, [cq], [wd["w_qn"]], tm=tm, tn=TILES.cols_low_rank, out_dtypes=[BF16],
             cols=[wd["g_qn"]])
    qr = _mm("q_rope", qr_body, [cq, cq], [wd["w_qr"], wd["w_qr_swap"]], tm=tm, tn=TILES.cols_rope, out_dtypes=[BF16],
             group_sizes=(1, 1), cols=[wd["g_qr"], wd["g_qr_swap"]],
             extras=[(c_tab, tab_spec), (sa_tab + sb_tab, tab_spec)])

    def f_body(acc, lb):
        s = jax.nn.sigmoid(acc)
        log2_f = jnp.log2(lb + (1.0 - lb) * s)
        return log2_f, (1.0 - lb) * (1.0 - s)

    ident = lambda acc: acc
    hw = d["hg_width"]
    proj = functools.partial(_mm, lhs=[h], rhs=[wd["w_hgrn"]], tm=tm, tn=TILES.cols_wide, n=hw)
    hq = proj("hgrn_q", ident, out_dtypes=[BF16], rhs_col0=0)
    lf, hk = proj("hgrn_f", f_body, out_dtypes=[F32, BF16], cols=[wd["lb"]], rhs_col0=hw)
    hv = proj("hgrn_v", ident, out_dtypes=[BF16], rhs_col0=2 * hw)
    hg = proj("hgrn_g", ident, out_dtypes=[BF16], rhs_col0=3 * hw)
    return (qn, qr, ckv, ckv_b, kr, kr_b), (hq, hk, hv, lf, hg)


def _out_proj(x, o_mla, o_hg, wd, *, tm):
    return _mm("out_proj", lambda acc, res: res + acc, [o_mla, o_hg], [wd["w_o"], wd["w_o"]], rhs_row_blocks=(0, 1),
               tm=tm, tn=TILES.cols_wide, out_dtypes=[F32], rows=[x])


def _down_ple(x, act, p, wd, *, tm):
    x, xg, ssq = _mm("ffn_down", lambda acc, res: res + acc, [act], [wd["w_down"]], tm=min(tm, TILES.down_rows),
                     tn=TILES.down_cols, out_dtypes=[F32], rows=[x], next_norm_gain=wd["g_ple"])
    width = x.shape[1]

    def ple_body(gate, emb, res, ssq_blk):
        return res + emb * jax.nn.sigmoid(gate * _row_rinv(ssq_blk, width))

    return _mm("ple", ple_body, [xg, p.astype(BF16)], [wd["w_ple_gate"], wd["w_ple"]], tm=tm, tn=TILES.cols,
               out_dtypes=[F32], group_sizes=(1, 1), rows=[x], extras=[(ssq, _ssq_spec(min(tm, x.shape[0])))])


def _prompt_layer(x, p, wd, *, batch, seq):
    d = wd["dims"]
    tm = min(TILES.rows, seq)
    assert seq % tm == 0
    tabs = _rope_tables(jnp.arange(seq), d["rope"])
    (qn, qr, ckv, ckv_b, kr, kr_b), (hq, hk, hv, lf, hg) = _mixer_inputs(x, wd, tabs, tm=tm)

    def kn_body(acc, g):
        return jnp.concatenate(_slab_rms(acc, g, d["nope"]), axis=-1)

    kn = _mm("k_nope", kn_body, [ckv_b], [wd["w_uk"]], tm=tm, tn=TILES.cols_low_rank, out_dtypes=[BF16],
             cols=[wd["g_kn"]])
    tq = min(TILES.attn, seq)
    vt = _value_t(ckv_b, wd["w_uv_t"], tk=tq, tn=TILES.cols_low_rank)
    o_mla = _attn_prompt(qn, qr, kn, kr_b, vt, batch=batch, seq=seq, heads=d["heads"], tq=tq, hb=TILES.attn_heads)

    hg_heads = d["hg_width"] // LANES
    s0 = jnp.zeros((batch, hg_heads, LANES, LANES), F32)
    o_hg, s_fin = _hgrn(hq, hk, hv, lf, hg, s0, wd["g_hg_out"], batch=batch, t_len=seq, blk=CHUNK,
                        tt=TILES.hgrn_rows, hb=TILES.hgrn_heads)

    x = _out_proj(x, o_mla, o_hg, wd, tm=tm)
    h = _rmsnorm(x, wd["g_ffn"], TILES.norm_rows)
    tm_up = min(TILES.ff_rows, seq)
    act, tail_g, tail_u = _up_conv_prompt(h, wd["w_up"], wd["conv_w"], wd["conv_b"], seq=seq, tm=tm_up,
                                          cm=TILES.ff_chunk)
    last = lambda t: t.reshape(batch, seq // tm_up, 8, -1)[:, -1, 6:, :]
    conv_state = jnp.concatenate([last(tail_g), last(tail_u)], axis=-1)
    x = _down_ple(x, act, p, wd, tm=tm)
    return x, ckv.reshape(batch, seq, -1), kr.reshape(batch, seq, -1), s_fin, conv_state


def _sample_layer(x, p, cache_ckv, cache_kr, state_hgrn, hist, wd, *, batch, t_new):
    d = wd["dims"]
    m = batch * t_new
    past = cache_ckv.shape[1]
    tabs = _rope_tables(jnp.tile(past + jnp.arange(t_new), batch), d["rope"])
    (qn, qr, ckv, ckv_b, kr, kr_b), (hq, hk, hv, lf, hg) = _mixer_inputs(x, wd, tabs, tm=m)

    qt = _per_head_mm("mla_absorb_q", qn, wd["w_absorb"])
    qr_h = jnp.transpose(qr.reshape(m, d["heads"], LANES), (1, 0, 2))
    o_lat = _attn_sample(qt, qr_h, cache_ckv, cache_kr, ckv_b, kr_b, wd["w_uk_t"], batch=batch, t_new=t_new,
                         heads=d["heads"], nope=d["nope"], rope=d["rope"], tk=TILES.sample_keys)
    o_mla = _value_up(o_lat, wd["w_uv_h"])

    o_hg, s_new = _hgrn(hq, hk, hv, lf, hg, state_hgrn, wd["g_hg_out"], batch=batch, t_len=t_new, blk=t_new,
                        tt=t_new, hb=TILES.hgrn_heads_short)

    x = _out_proj(x, o_mla, o_hg, wd, tm=m)
    h = _rmsnorm(x, wd["g_ffn"], TILES.norm_rows)
    d_ff = d["d_ff"]
    u = _mm("ffn_up_sample", lambda acc: acc, [h], [wd["w_up"]], tm=m, tn=2 * _ff_tile(d_ff), out_dtypes=[F32])
    u3 = u.reshape(batch, t_new, 2 * d_ff)
    act = _conv_sample(u3, hist, wd["conv_w"], wd["conv_b"])
    taps = wd["conv_w"].shape[0]
    conv_state = jnp.concatenate([hist, u3], axis=1)[:, -(taps - 1):]
    x = _down_ple(x, act.reshape(m, d_ff).astype(BF16), p, wd, tm=m)
    return x, ckv.reshape(batch, t_new, -1), kr.reshape(batch, t_new, -1), s_new, conv_state


def kernel(x_prompt, x_sample, cache_ckv, cache_k_rope, state_hgrn, state_ffn_conv, p_prompt, p_sample,
           g_attn_norm, w_in, g_q_a, w_uq, g_kv_a, w_uk, w_uv, g_q_nope, g_q_rope, g_k_nope, g_k_rope,
           lb_logits, g_hg_out, w_o, g_ffn_norm, w_up, conv_w, conv_b, w_down, g_ple_norm, w_ple_gate, w_ple):
    depth = w_in.shape[0]
    batch, seq, d_model = x_prompt.shape
    dec_batch, t_new, _ = x_sample.shape
    assert t_new >= conv_w.shape[1] - 1
    lb_all = jnp.cumsum(jax.nn.softmax(lb_logits.astype(F32), axis=0), axis=0)
    xp = x_prompt.reshape(batch * seq, d_model)
    xs = x_sample.reshape(dec_batch * t_new, d_model)
    outs = [[] for _ in range(8)]
    for i in range(depth):
        wd = _layer_weights(i, lb_all[i], g_attn_norm, w_in, g_q_a, w_uq, g_kv_a, w_uk, w_uv, g_q_nope, g_q_rope,
                            g_k_nope, g_k_rope, g_hg_out, w_o, g_ffn_norm, w_up, conv_w, conv_b, w_down,
                            g_ple_norm, w_ple_gate, w_ple)
        xp, *new_p = _prompt_layer(xp, p_prompt[i].reshape(batch * seq, -1), wd, batch=batch, seq=seq)
        xs, *new_s = _sample_layer(xs, p_sample[i].reshape(dec_batch * t_new, -1), cache_ckv[i], cache_k_rope[i],
                                   state_hgrn[i], state_ffn_conv[i], wd, batch=dec_batch, t_new=t_new)
        for lst, val in zip(outs, new_p + new_s):
            lst.append(val)
    return (xp.reshape(batch, seq, d_model), xs.reshape(dec_batch, t_new, d_model), *[jnp.stack(o) for o in outs])
```

```python
import functools
from typing import NamedTuple

import numpy as np
import jax
import jax.numpy as jnp
from jax import lax
from jax.experimental import pallas as pl
from jax.experimental.pallas import tpu as pltpu

F32 = jnp.float32
BF16 = jnp.bfloat16

EPS = 1e-6
CHUNK = 64
ROPE_THETA = 10000.0
LANES = 128
NEG_BIG = -1e30
VMEM_LIMIT_BYTES = 56 * 1024 * 1024


class _Tiles(NamedTuple):
    rows: int = 1024
    cols: int = 512
    cols_wide: int = 1024
    cols_low_rank: int = 2048
    cols_rope: int = 2048
    down_rows: int = 512
    down_cols: int = 512
    norm_rows: int = 512
    latent_rows: int = 512
    attn: int = 512
    attn_heads: int = 8
    hgrn_rows: int = 512
    hgrn_heads: int = 16
    hgrn_heads_short: int = 16
    ff_rows: int = 2048
    ff_chunk: int = 512
    sample_keys: int = 1024


TILES = _Tiles()


def _params(*sem):
    return pltpu.CompilerParams(dimension_semantics=sem, vmem_limit_bytes=VMEM_LIMIT_BYTES)


def _dot(a, b):
    return jnp.dot(a, b, preferred_element_type=F32)


def _dot_nt(a, b):
    return lax.dot_general(a, b, (((1,), (1,)), ((), ())), preferred_element_type=F32)


def _dot_tn(a, b):
    return lax.dot_general(a, b, (((0,), (0,)), ((), ())), preferred_element_type=F32)


def _rms(x, g):
    ms = jnp.mean(x * x, axis=-1, keepdims=True)
    return x * lax.rsqrt(ms + EPS) * g


def _slab_rms(x, g, valid):
    outs = []
    for a in range(x.shape[1] // LANES):
        blk = x[:, a * LANES:(a + 1) * LANES]
        ms = jnp.sum(blk * blk, axis=-1, keepdims=True) * (1.0 / valid)
        outs.append(blk * lax.rsqrt(ms + EPS) * g[:, a * LANES:(a + 1) * LANES])
    return outs


def _rope_slab(y, c, sa, sb):
    return y * c + pltpu.roll(y, 96, 1) * sa + pltpu.roll(y, 32, 1) * sb


def _rmsnorm_kernel(x_ref, g_ref, o_ref):
    o_ref[...] = _rms(x_ref[...], g_ref[...]).astype(o_ref.dtype)


def _rmsnorm(x, g, tm):
    m, d = x.shape
    tm = min(tm, m)
    assert m % tm == 0
    return pl.pallas_call(
        _rmsnorm_kernel,
        grid=(m // tm,),
        in_specs=[pl.BlockSpec((tm, d), lambda i: (i, 0)), pl.BlockSpec((1, d), lambda i: (0, 0))],
        out_specs=pl.BlockSpec((tm, d), lambda i: (i, 0)),
        out_shape=jax.ShapeDtypeStruct((m, d), BF16),
        compiler_params=_params("parallel"),
        name="rmsnorm",
    )(x, g.reshape(1, d))


def _lane_fold(x):
    out = x[:, :LANES]
    for a in range(1, x.shape[1] // LANES):
        out = out + x[:, a * LANES:(a + 1) * LANES]
    return out


def _row_rinv(ssq, width):
    return lax.rsqrt(jnp.sum(ssq, axis=-1, keepdims=True) * (1.0 / width) + EPS)


def _mm_kernel(*refs, body, n_lhs, group_sizes, n_in, next_norm):
    ins, outs = refs[:n_in], refs[n_in:]
    n_rhs = sum(group_sizes)
    lhs, rhs, rest = ins[:n_lhs], ins[n_lhs:n_lhs + n_rhs], ins[n_lhs + n_rhs:]
    accs, k = [], 0
    for gs in group_sizes:
        acc = None
        for _ in range(gs):
            d = _dot(lhs[min(k, n_lhs - 1)][...], rhs[k][...])

            acc = d if acc is None else acc + d
            k += 1
        accs.append(acc)
    if next_norm:
        *rest, gain_ref = rest
        *outs, scaled_ref, ssq_ref = outs
    res = body(*accs, *[r[...] for r in rest])
    if not isinstance(res, (tuple, list)):
        res = (res,)
    for o, r in zip(outs, res):
        o[...] = r.astype(o.dtype)
    if next_norm:
        x = res[0]
        scaled_ref[...] = (x * gain_ref[...]).astype(scaled_ref.dtype)
        @pl.when(pl.program_id(1) == 0)
        def _():
            ssq_ref[...] = jnp.zeros(ssq_ref.shape, F32)

        ssq_ref[...] += _lane_fold(x * x)


def _mm(name, body, lhs, rhs, *, tm, tn, out_dtypes, group_sizes=None, cols=(), rows=(), extras=(),
        next_norm_gain=None, n=None, rhs_col0=0, rhs_row_blocks=None):
    m = lhs[0].shape[0]
    n = n or rhs[0].shape[1]
    tm, tn = min(tm, m), min(tn, n)
    col0 = rhs_col0 if isinstance(rhs_col0, tuple) else (rhs_col0,) * len(rhs)
    assert m % tm == 0 and n % tn == 0 and all(c % tn == 0 for c in col0), (name, m, n, tm, tn)
    assert len(lhs) in (1, len(rhs))
    group_sizes = tuple(group_sizes or (len(rhs),))
    rhs_row_blocks = rhs_row_blocks or (0,) * len(rhs)
    k_dims = [lhs[min(k, len(lhs) - 1)].shape[1] for k in range(len(rhs))]
    tile = pl.BlockSpec((tm, tn), lambda i, j: (i, j))
    in_specs = [pl.BlockSpec((tm, a.shape[1]), lambda i, j: (i, 0)) for a in lhs]
    in_specs += [pl.BlockSpec((kd, tn), lambda i, j, rb=rb, j0=c // tn: (rb, j + j0))
                 for kd, rb, c in zip(k_dims, rhs_row_blocks, col0)]
    in_specs += [pl.BlockSpec((c.shape[0], tn), lambda i, j: (0, j)) for c in cols]
    in_specs += [tile for _ in rows]
    in_specs += [spec for _, spec in extras]
    args = list(lhs) + list(rhs) + list(cols) + list(rows) + [a for a, _ in extras]
    out_specs = [tile for _ in out_dtypes]
    out_shape = [jax.ShapeDtypeStruct((m, n), dt) for dt in out_dtypes]
    if next_norm_gain is not None:
        in_specs.append(pl.BlockSpec((1, tn), lambda i, j: (0, j)))
        args.append(next_norm_gain.reshape(1, n))
        out_specs += [tile, pl.BlockSpec((tm, LANES), lambda i, j: (i, 0))]
        out_shape += [jax.ShapeDtypeStruct((m, n), BF16), jax.ShapeDtypeStruct((m, LANES), F32)]
    kern = functools.partial(_mm_kernel, body=body, n_lhs=len(lhs), group_sizes=group_sizes, n_in=len(args),
                             next_norm=next_norm_gain is not None)
    outs = pl.pallas_call(
        kern,
        grid=(m // tm, n // tn),
        in_specs=in_specs,
        out_specs=out_specs,
        out_shape=out_shape,
        compiler_params=_params("parallel", "arbitrary"),
        name=name,
    )(*args)
    return outs if len(outs) > 1 else outs[0]


def _ssq_spec(tm):
    return pl.BlockSpec((tm, LANES), lambda i, j: (i, 0))


def _latent_kernel(h_ref, w_ref, gq_ref, gkv_ref, gkr_ref, c_ref, sa_ref, sb_ref,
                   cq_ref, ckv_ref, ckvb_ref, kr_ref, krb_ref, *, q_lora, kv_lora, rope):
    z = _dot(h_ref[...], w_ref[...])
    cq_ref[...] = _rms(z[:, :q_lora], gq_ref[...]).astype(cq_ref.dtype)
    ckv = _rms(z[:, q_lora:q_lora + kv_lora], gkv_ref[...])
    ckv_ref[...] = ckv
    ckvb_ref[...] = ckv.astype(ckvb_ref.dtype)
    (kr,) = _slab_rms(z[:, q_lora + kv_lora:], gkr_ref[...], rope)
    kr = _rope_slab(kr, c_ref[...], sa_ref[...], sb_ref[...])
    kr_ref[...] = kr[:, :rope]
    krb_ref[...] = kr.astype(krb_ref.dtype)


def _latent(h, w_lat, g_q_a, g_kv_a, g_k_rope_pad, tabs, *, q_lora, kv_lora, rope, tm):
    m, d = h.shape
    tm = min(tm, m)
    n = w_lat.shape[1]
    n_tab = tabs[0].shape[0] // tm
    row = lambda w: pl.BlockSpec((tm, w), lambda i: (i, 0))
    const = lambda r, w: pl.BlockSpec((r, w), lambda i: (0, 0))
    tab = pl.BlockSpec((tm, LANES), lambda i: (i % n_tab, 0))
    kern = functools.partial(_latent_kernel, q_lora=q_lora, kv_lora=kv_lora, rope=rope)
    return pl.pallas_call(
        kern,
        grid=(m // tm,),
        in_specs=[row(d), const(d, n), const(1, q_lora), const(1, kv_lora), const(1, LANES), tab, tab, tab],
        out_specs=[row(q_lora), row(kv_lora), row(kv_lora), row(rope), row(LANES)],
        out_shape=[jax.ShapeDtypeStruct((m, q_lora), BF16), jax.ShapeDtypeStruct((m, kv_lora), F32),
                   jax.ShapeDtypeStruct((m, kv_lora), BF16), jax.ShapeDtypeStruct((m, rope), F32),
                   jax.ShapeDtypeStruct((m, LANES), BF16)],
        compiler_params=_params("parallel"),
        name="mla_latent",
    )(h, w_lat, g_q_a.reshape(1, -1), g_kv_a.reshape(1, -1), g_k_rope_pad, *tabs)


def _attn_kernel(qn_ref, qr_ref, kn_ref, kr_ref, vt_ref, o_ref, m_scr, l_scr, acc_scr, *, tq, chunk, hb):
    qi = pl.program_id(2)
    lanes = lambda h: slice(h * LANES, (h + 1) * LANES)
    qs = [jnp.concatenate([qn_ref[:, lanes(h)], qr_ref[:, lanes(h)]], axis=-1) for h in range(hb)]
    m_scr[...] = jnp.full(m_scr.shape, NEG_BIG, F32)
    l_scr[...] = jnp.zeros(l_scr.shape, F32)
    acc_scr[...] = jnp.zeros(acc_scr.shape, F32)

    def block(kb, masked):
        off = pl.multiple_of(kb * tq, tq)
        kr = kr_ref[pl.ds(off, tq), :]
        scores = []
        for h in range(hb):
            k = jnp.concatenate([kn_ref[pl.ds(off, tq), lanes(h)], kr], axis=-1)
            s = _dot_nt(k, qs[h])
            if masked:
                kc = lax.broadcasted_iota(jnp.int32, (tq, tq), 0) // chunk
                qc = lax.broadcasted_iota(jnp.int32, (tq, tq), 1) // chunk
                s = jnp.where(kc <= qc, s, NEG_BIG)
            scores.append(s)
        probs = []
        for h in range(hb):
            m_i = m_scr[h]
            m_new = jnp.maximum(m_i, jnp.max(scores[h], axis=0, keepdims=True))
            alpha = jnp.exp2(m_i - m_new)
            p = jnp.exp2(scores[h] - m_new)
            l_scr[h] = alpha * l_scr[h] + jnp.sum(p, axis=0, keepdims=True)
            m_scr[h] = m_new
            probs.append((alpha, p.astype(BF16)))
        for h in range(hb):
            alpha, p = probs[h]
            acc_scr[h] = alpha * acc_scr[h] + _dot(vt_ref[kb, lanes(h), :], p)

    def body(kb, carry):
        block(kb, False)
        return carry

    lax.fori_loop(0, qi, body, 0)
    block(qi, True)
    for h in range(hb):
        o_ref[:, lanes(h)] = (acc_scr[h] / l_scr[h]).T.astype(o_ref.dtype)


def _attn_prompt(qn, qr, kn, krb, vt, *, batch, seq, heads, tq, hb):
    nq = seq // tq
    hb = min(hb, heads)
    assert heads % hb == 0
    qspec = pl.BlockSpec((tq, hb * LANES), lambda b, h, i: (b * nq + i, h))
    once = dict(pipeline_mode=pl.Buffered(1))
    kspec = pl.BlockSpec((seq, hb * LANES), lambda b, h, i: (b, h), **once)
    kern = functools.partial(_attn_kernel, tq=tq, chunk=CHUNK, hb=hb)
    return pl.pallas_call(
        kern,
        grid=(batch, heads // hb, nq),
        in_specs=[qspec, qspec, kspec, pl.BlockSpec((seq, LANES), lambda b, h, i: (b, 0), **once),
                  pl.BlockSpec((nq, hb * LANES, tq), lambda b, h, i: (b, h, 0), **once)],
        out_specs=qspec,
        out_shape=jax.ShapeDtypeStruct(qn.shape, BF16),
        scratch_shapes=[pltpu.VMEM((hb, 1, tq), F32), pltpu.VMEM((hb, 1, tq), F32),
                        pltpu.VMEM((hb, LANES, tq), F32)],
        compiler_params=_params("parallel", "parallel", "arbitrary"),
        name="mla_attn_prompt",
    )(qn, qr, kn, krb, vt)


def _value_t_kernel(w_ref, x_ref, o_ref):
    o_ref[0] = _dot_nt(w_ref[...], x_ref[...]).astype(o_ref.dtype)


def _value_t(x, w_t, *, tk, tn):
    m, c = x.shape
    n = w_t.shape[0]
    tn = min(tn, n)
    return pl.pallas_call(
        _value_t_kernel,
        grid=(m // tk, n // tn),
        in_specs=[pl.BlockSpec((tn, c), lambda i, j: (j, 0)), pl.BlockSpec((tk, c), lambda i, j: (i, 0))],
        out_specs=pl.BlockSpec((1, tn, tk), lambda i, j: (i, j, 0)),
        out_shape=jax.ShapeDtypeStruct((m // tk, n, tk), BF16),
        compiler_params=_params("parallel", "arbitrary"),
        name="v_up_t",
    )(w_t, x)


def _absorb_kernel(x_ref, w_ref, o_ref):
    o_ref[0] = _dot(x_ref[...], w_ref[0]).astype(o_ref.dtype)


def _per_head_mm(name, x, w):
    m = x.shape[0]
    heads, _, n = w.shape
    return pl.pallas_call(
        _absorb_kernel,
        grid=(heads,),
        in_specs=[pl.BlockSpec((m, LANES), lambda h: (0, h)), pl.BlockSpec((1, LANES, n), lambda h: (h, 0, 0))],
        out_specs=pl.BlockSpec((1, m, n), lambda h: (h, 0, 0)),
        out_shape=jax.ShapeDtypeStruct((heads, m, n), BF16),
        compiler_params=_params("parallel"),
        name=name,
    )(x, w)


def _vup_kernel(x_ref, w_ref, o_ref):
    o_ref[...] = _dot(x_ref[0], w_ref[0]).astype(o_ref.dtype)


def _value_up(o_lat, w):
    heads, m, c = o_lat.shape
    return pl.pallas_call(
        _vup_kernel,
        grid=(heads,),
        in_specs=[pl.BlockSpec((1, m, c), lambda h: (h, 0, 0)), pl.BlockSpec((1, c, LANES), lambda h: (h, 0, 0))],
        out_specs=pl.BlockSpec((m, LANES), lambda h: (0, h)),
        out_shape=jax.ShapeDtypeStruct((m, heads * LANES), BF16),
        compiler_params=_params("parallel"),
        name="mla_value_up",
    )(o_lat, w)


def _attn_sample_kernel(qt_ref, qr_ref, cache_ref, kcache_ref, cnew_ref, knew_ref, wkt_ref, o_ref,
                        *, heads, t_new, past, tk, nope, rope):
    hq = heads * t_new
    c_lat = cache_ref.shape[-1]
    qt = qt_ref[...].reshape(hq, c_lat)
    qr = qr_ref[...].reshape(hq, LANES)[:, :rope]
    wkt = wkt_ref[...]

    def scores(c_b, kr_b):
        n = c_b.shape[0]
        kn_t = _dot_nt(wkt, c_b)
        ssq = jnp.sum((kn_t * kn_t).reshape(heads, nope, n), axis=1)
        rinv = lax.rsqrt(ssq * (1.0 / nope) + EPS)
        rinv = jnp.broadcast_to(rinv[:, None, :], (heads, t_new, n)).reshape(hq, n)
        return _dot_nt(qt, c_b) * rinv + _dot_nt(qr, kr_b)

    c_new = cnew_ref[...]
    s_new = scores(c_new, knew_ref[...][:, :rope])
    m_i = jnp.max(s_new, axis=-1, keepdims=True)
    p = jnp.exp2(s_new - m_i)
    l_i = jnp.sum(p, axis=-1, keepdims=True)
    acc = _dot(p.astype(BF16), c_new)
    for kt in range(past // tk):
        c_b = cache_ref[0, kt * tk:(kt + 1) * tk, :].astype(BF16)
        s = scores(c_b, kcache_ref[0, kt * tk:(kt + 1) * tk, :].astype(BF16))
        m_new = jnp.maximum(m_i, jnp.max(s, axis=-1, keepdims=True))
        alpha = jnp.exp2(m_i - m_new)
        p = jnp.exp2(s - m_new)
        l_i = alpha * l_i + jnp.sum(p, axis=-1, keepdims=True)
        acc = alpha * acc + _dot(p.astype(BF16), c_b)
        m_i = m_new
    o_ref[...] = (acc / l_i).reshape(heads, t_new, c_lat).astype(o_ref.dtype)


def _attn_sample(qt, qr_h, cache, kcache, ckv_new, kr_new, wkt, *, batch, t_new, heads, nope, rope, tk):
    past, c_lat = cache.shape[1], cache.shape[2]
    tk = min(tk, past)
    kern = functools.partial(_attn_sample_kernel, heads=heads, t_new=t_new, past=past, tk=tk, nope=nope, rope=rope)
    return pl.pallas_call(
        kern,
        grid=(batch,),
        in_specs=[pl.BlockSpec((heads, t_new, c_lat), lambda b: (0, b, 0)),
                  pl.BlockSpec((heads, t_new, LANES), lambda b: (0, b, 0)),
                  pl.BlockSpec((1, past, c_lat), lambda b: (b, 0, 0)),
                  pl.BlockSpec((1, past, rope), lambda b: (b, 0, 0)),
                  pl.BlockSpec((t_new, c_lat), lambda b: (b, 0)),
                  pl.BlockSpec((t_new, LANES), lambda b: (b, 0)),
                  pl.BlockSpec(wkt.shape, lambda b: (0, 0))],
        out_specs=pl.BlockSpec((heads, t_new, c_lat), lambda b: (0, b, 0)),
        out_shape=jax.ShapeDtypeStruct((heads, batch * t_new, c_lat), BF16),
        compiler_params=_params("parallel"),
        name="mla_attn_sample",
    )(qt, qr_h, cache, kcache, ckv_new, kr_new, wkt)


def _hgrn_tables(blk):
    t = np.arange(blk)
    levels = [blk >> (i + 1) for i in range(int(np.log2(blk)))]
    lvl = np.full((blk, blk), -1, np.int32)
    lvl[t, t] = 0
    for li, c in enumerate(levels, start=1):
        same_pair = (t[:, None] // (2 * c)) == (t[None, :] // (2 * c))
        split = (t[:, None] // c) != (t[None, :] // c)
        lvl[same_pair & split & (t[:, None] > t[None, :])] = li
    prefix = (t[None, :] <= t[:, None]).astype(np.float32)
    rows = [prefix]
    for c in (2, 1):
        mid = (t // (2 * c)) * (2 * c) + c - 1
        rows.append((t[None, :] <= mid[:, None]).astype(np.float32))
    return levels, jnp.asarray(lvl), jnp.asarray(np.concatenate(rows, axis=0), dtype=BF16)


def _hgrn_block(q_ref, k_ref, v_ref, lf_ref, g_ref, o_ref, st_scr, rows, sel, pair_masks, upper_masks, signs,
                g_out, *, blk, levels, hb):
    cols = lambda h: slice(h * LANES, (h + 1) * LANES)
    sums = []
    for h in range(hb):
        lf = lf_ref[rows, cols(h)]
        hi = lf.astype(BF16)
        lo = (lf - hi.astype(F32)).astype(BF16)
        both = _dot(sel, jnp.concatenate([hi, lo], axis=-1))
        sums.append(both[:, :LANES] + both[:, LANES:])
    intra = []
    for h in range(hb):
        b = sums[h][:blk]
        q, k = q_ref[rows, cols(h)], k_ref[rows, cols(h)]
        qf, kf = q.astype(F32), k.astype(F32)
        a = jnp.where(pair_masks[0], _dot_nt(q, k), 0.0)
        for li, c in enumerate(levels, start=1):
            if c == 2:
                ref = sums[h][blk:2 * blk]
            elif c == 1:
                ref = sums[h][2 * blk:]
            else:
                b3 = b.reshape(blk // (2 * c), 2 * c, LANES)
                ref = jnp.broadcast_to(b3[:, c - 1:c, :], b3.shape).reshape(blk, LANES)
            decay = jnp.exp2((b - ref) * signs[li - 1])
            w = (jnp.where(upper_masks[li - 1], qf, kf) * decay).astype(BF16)
            a = jnp.where(pair_masks[li], _dot_nt(w, w), a)
        intra.append(a.astype(BF16))
    for h in range(hb):
        b = sums[h][:blk]
        b_last = b[blk - 1:blk, :]
        qf, kf = q_ref[rows, cols(h)].astype(F32), k_ref[rows, cols(h)].astype(F32)
        v = v_ref[rows, cols(h)]
        s_t = st_scr[h]
        inter = _dot_nt((qf * jnp.exp2(b)).astype(BF16), s_t.astype(BF16))
        o = inter + _dot(intra[h], v)
        k_t = (kf * jnp.exp2(b_last - b)).astype(BF16)
        st_scr[h] = s_t * jnp.exp2(b_last) + _dot_tn(v, k_t)
        gf = g_ref[rows, cols(h)].astype(F32)
        o_ref[rows, cols(h)] = (_rms(o, g_out) * (gf * jax.nn.sigmoid(gf))).astype(o_ref.dtype)


def _hgrn_kernel(q_ref, k_ref, v_ref, lf_ref, g_ref, s0_ref, sel_ref, lvl_ref, gout_ref,
                 o_ref, s_out_ref, st_scr, *, blk, levels, hb, n_blk):
    tb = pl.program_id(2)

    @pl.when(tb == 0)
    def _():
        for h in range(hb):
            st_scr[h] = s0_ref[0, h].T

    sel, lvl, g_out = sel_ref[...], lvl_ref[...], gout_ref[...]
    pair_masks = [lvl == li for li in range(len(levels) + 1)]
    row = lax.broadcasted_iota(jnp.int32, (blk, LANES), 0)
    upper_masks = [(row // c) % 2 == 1 for c in levels]
    signs = [jnp.where(u, 1.0, -1.0) for u in upper_masks]

    def step(c, carry):
        rows = pl.ds(pl.multiple_of(c * blk, blk), blk)
        _hgrn_block(q_ref, k_ref, v_ref, lf_ref, g_ref, o_ref, st_scr, rows, sel, pair_masks, upper_masks, signs,
                    g_out, blk=blk, levels=levels, hb=hb)
        return carry

    lax.fori_loop(0, n_blk, step, 0)

    @pl.when(tb == pl.num_programs(2) - 1)
    def _():
        for h in range(hb):
            s_out_ref[0, h] = st_scr[h].T


def _hgrn(q, k, v, lf, gate, s0, g_out, *, batch, t_len, blk, tt, hb):
    heads = q.shape[1] // LANES
    tt, hb = min(tt, t_len), min(hb, heads)
    levels, lvl, sel = _hgrn_tables(blk)
    nt = t_len // tt
    xspec = pl.BlockSpec((tt, hb * LANES), lambda b, h, t: (b * nt + t, h))
    sspec = pl.BlockSpec((1, hb, LANES, LANES), lambda b, h, t: (b, h, 0, 0))
    const = lambda a: pl.BlockSpec(a.shape, lambda b, h, t: (0, 0))
    kern = functools.partial(_hgrn_kernel, blk=blk, levels=levels, hb=hb, n_blk=tt // blk)
    g_out = g_out.reshape(1, LANES)
    return pl.pallas_call(
        kern,
        grid=(batch, heads // hb, nt),
        in_specs=[xspec, xspec, xspec, xspec, xspec, sspec, const(sel), const(lvl), const(g_out)],
        out_specs=[xspec, sspec],
        out_shape=[jax.ShapeDtypeStruct(q.shape, BF16), jax.ShapeDtypeStruct(s0.shape, F32)],
        scratch_shapes=[pltpu.VMEM((hb, LANES, LANES), F32)],
        compiler_params=_params("parallel", "parallel", "arbitrary"),
        name="hgrn2",
    )(q, k, v, lf, gate, s0, sel, lvl, g_out)


def _silu(x):
    h = 0.5 * x
    return h + h * jnp.tanh(h)


def _ff_tile(d_ff):
    assert d_ff % LANES == 0
    return 2 * LANES if d_ff % (2 * LANES) == 0 else LANES


def _up_conv_kernel(h_ref, wg_ref, wu_ref, cwg_ref, cwu_ref, cbg_ref, cbu_ref,
                    act_ref, tailg_ref, tailu_ref, ug_scr, uu_scr, carry_scr, *, tm, cm, tiles_per_seq):
    i, j = pl.program_id(0), pl.program_id(1)
    seq_start = (i % tiles_per_seq) == 0
    halves = ((wg_ref, cwg_ref, cbg_ref, tailg_ref, ug_scr), (wu_ref, cwu_ref, cbu_ref, tailu_ref, uu_scr))
    for half, (_, _, _, _, scr) in enumerate(halves):
        @pl.when(seq_start)
        def _(scr=scr):
            scr[0:8, :] = jnp.zeros((8, scr.shape[1]), F32)

        @pl.when(jnp.logical_not(seq_start))
        def _(scr=scr, half=half):
            scr[0:8, :] = carry_scr[half, j]

    for r in range(tm // cm):
        h = h_ref[r * cm:(r + 1) * cm, :]
        conv = []
        for w_ref, cw_ref, cb_ref, _, scr in halves:
            u = _dot(h, w_ref[...])
            scr[8 + r * cm:8 + (r + 1) * cm, :] = u
            cw = cw_ref[...]
            conv.append(cb_ref[...] + cw[2:3] * u + cw[1:2] * scr[7 + r * cm:7 + (r + 1) * cm, :]
                        + cw[0:1] * scr[6 + r * cm:6 + (r + 1) * cm, :])
        act_ref[r * cm:(r + 1) * cm, :] = (_silu(conv[0]) * conv[1]).astype(act_ref.dtype)

    for half, (_, _, _, tail_ref, scr) in enumerate(halves):
        tail = scr[tm:tm + 8, :]
        carry_scr[half, j] = tail
        tail_ref[...] = tail


def _up_conv_prompt(h, w_up, conv_w, conv_b, *, seq, tm, cm):
    m, d = h.shape
    n = w_up.shape[1] // 2
    tn = _ff_tile(n)
    tm = min(tm, seq)
    cm = min(cm, tm)
    assert seq % tm == 0 and tm % cm == 0 and conv_w.shape[0] == 3
    nj = n // tn
    gate = lambda r: pl.BlockSpec((r, tn), lambda i, j: (0, j))
    up = lambda r: pl.BlockSpec((r, tn), lambda i, j: (0, j + nj))
    tail = pl.BlockSpec((8, tn), lambda i, j: (i, j))
    kern = functools.partial(_up_conv_kernel, tm=tm, cm=cm, tiles_per_seq=seq // tm)
    return pl.pallas_call(
        kern,
        grid=(m // tm, nj),
        in_specs=[pl.BlockSpec((tm, d), lambda i, j: (i, 0)), gate(d), up(d), gate(3), up(3), gate(1), up(1)],
        out_specs=[pl.BlockSpec((tm, tn), lambda i, j: (i, j)), tail, tail],
        out_shape=[jax.ShapeDtypeStruct((m, n), BF16), jax.ShapeDtypeStruct((m // tm * 8, n), F32),
                   jax.ShapeDtypeStruct((m // tm * 8, n), F32)],
        scratch_shapes=[pltpu.VMEM((tm + 8, tn), F32), pltpu.VMEM((tm + 8, tn), F32),
                        pltpu.VMEM((2, nj, 8, tn), F32)],
        compiler_params=_params("arbitrary", "arbitrary"),
        name="ffn_up_conv",
    )(h, w_up, w_up, conv_w, conv_w, conv_b, conv_b)


def _conv_sample_kernel(ug_ref, uu_ref, hg_ref, hu_ref, cwg_ref, cwu_ref, cbg_ref, cbu_ref, act_ref, *, t_len, taps):
    conv = [[None] * t_len, [None] * t_len]
    for half, (u_ref, hist_ref, cw_ref, cb_ref) in enumerate(
            ((ug_ref, hg_ref, cwg_ref, cbg_ref), (uu_ref, hu_ref, cwu_ref, cbu_ref))):
        cw, cb = cw_ref[...], cb_ref[...]
        rows = [hist_ref[:, r, :] for r in range(taps - 1)] + [u_ref[:, t, :] for t in range(t_len)]
        for t in range(t_len):
            c = cb
            for tap in range(taps):
                c = c + cw[tap:tap + 1] * rows[t + tap]
            conv[half][t] = c
    for t in range(t_len):
        act_ref[:, t, :] = _silu(conv[0][t]) * conv[1][t]


def _conv_sample(u, hist, conv_w, conv_b):
    bsz, t_len, n2 = u.shape
    n = n2 // 2
    tn = _ff_tile(n)
    nj = n // tn
    taps = conv_w.shape[0]
    gate = lambda *lead: pl.BlockSpec((*lead, tn), lambda j: (*([0] * len(lead)), j))
    up = lambda *lead: pl.BlockSpec((*lead, tn), lambda j: (*([0] * len(lead)), j + nj))
    kern = functools.partial(_conv_sample_kernel, t_len=t_len, taps=taps)
    return pl.pallas_call(
        kern,
        grid=(nj,),
        in_specs=[gate(bsz, t_len), up(bsz, t_len), gate(bsz, taps - 1), up(bsz, taps - 1),
                  gate(taps), up(taps), gate(1), up(1)],
        out_specs=gate(bsz, t_len),
        out_shape=jax.ShapeDtypeStruct((bsz, t_len, n), F32),
        compiler_params=_params("parallel"),
        name="ffn_conv_sample",
    )(u, u, hist, hist, conv_w, conv_w, conv_b, conv_b)


def _pad_cols(w, n):
    return jnp.pad(w, ((0, 0), (0, n - w.shape[1])))


def _rope_tables(pos, rope):
    half = rope // 2
    inv = 1.0 / (ROPE_THETA ** (jnp.arange(half, dtype=F32) / half))
    ang = pos.astype(F32)[:, None] * inv[None, :]
    cos, sin = jnp.cos(ang), jnp.sin(ang)
    z = jnp.zeros_like(cos)
    zz = jnp.zeros((pos.shape[0], LANES - rope), F32)
    c = jnp.concatenate([cos, cos, zz], axis=-1)
    sa = jnp.concatenate([-sin, z, zz], axis=-1)
    sb = jnp.concatenate([z, sin, zz], axis=-1)
    return c, sa, sb


def _layer_weights(i, lb, g_attn_norm, w_in, g_q_a, w_uq, g_kv_a, w_uk, w_uv, g_q_nope, g_q_rope, g_k_nope,
                   g_k_rope, g_hg_out, w_o, g_ffn_norm, w_up, conv_w, conv_b, w_down, g_ple_norm, w_ple_gate,
                   w_ple):
    q_lora, kv_lora = g_q_a.shape[1], g_kv_a.shape[1]
    nope, rope = g_q_nope.shape[1], g_q_rope.shape[1]
    heads, v_dim = w_uk.shape[2], w_uv.shape[3]
    hg_width = lb.shape[0]
    d_ff = w_down.shape[1]
    assert nope == LANES and v_dim == LANES and g_hg_out.shape[1] == LANES and 2 * rope == LANES
    scale = float((nope + rope) ** -0.5) * float(np.log2(np.e))
    o3 = q_lora + kv_lora + rope
    w_in_i = w_in[i]
    wd = {"dims": dict(q_lora=q_lora, kv_lora=kv_lora, nope=nope, rope=rope, heads=heads, hg_width=hg_width,
                       d_ff=d_ff)}
    wd["g_attn"], wd["g_ffn"], wd["g_ple"] = g_attn_norm[i], g_ffn_norm[i], g_ple_norm[i]
    wd["w_lat"] = _pad_cols(w_in_i[:, :o3], o3 + LANES - rope).astype(BF16)
    wd["g_q_a"], wd["g_kv_a"] = g_q_a[i], g_kv_a[i]
    wd["g_k_rope_pad"] = jnp.pad(g_k_rope[i], (0, LANES - rope)).reshape(1, LANES)
    wd["w_hgrn"] = w_in_i[:, o3:].astype(BF16)
    wd["lb"] = lb.reshape(1, hg_width)
    wq = w_uq[i].reshape(q_lora, heads, nope + rope)
    wd["w_qn"] = wq[:, :, :nope].reshape(q_lora, heads * nope).astype(BF16)
    half = rope // 2
    slab = lambda a, b: jnp.pad(jnp.concatenate([a, b], axis=-1), [(0, 0)] * (a.ndim - 1) + [(0, LANES - rope)])
    flat = lambda w: w.reshape(q_lora, heads * LANES).astype(BF16)
    wr1, wr2 = wq[:, :, nope:nope + half], wq[:, :, nope + half:]
    wd["w_qr"], wd["w_qr_swap"] = flat(slab(wr1, wr2)), flat(slab(wr2, wr1))
    wd["g_qn"] = (jnp.tile(g_q_nope[i], heads) * scale).reshape(1, heads * nope)
    gr = g_q_rope[i] * scale
    wd["g_qr"] = jnp.tile(slab(gr[:half], gr[half:]), heads).reshape(1, heads * LANES)
    wd["g_qr_swap"] = jnp.tile(slab(gr[half:], gr[:half]), heads).reshape(1, heads * LANES)
    wd["w_uk"] = w_uk[i].reshape(kv_lora, heads * nope).astype(BF16)
    wd["g_kn"] = jnp.tile(g_k_nope[i], heads).reshape(1, heads * nope)
    wd["w_uv_t"] = jnp.transpose(w_uv[i], (1, 2, 0)).reshape(heads * v_dim, kv_lora).astype(BF16)
    wd["w_absorb"] = jnp.transpose(w_uk[i] * g_k_nope[i][None, None, :], (1, 2, 0)).astype(BF16)
    wd["w_uk_t"] = jnp.transpose(w_uk[i], (1, 2, 0)).reshape(heads * nope, kv_lora).astype(BF16)
    wd["w_uv_h"] = jnp.transpose(w_uv[i], (1, 0, 2)).astype(BF16)
    wd["g_hg_out"] = g_hg_out[i]
    mla_width = heads * v_dim
    assert w_o.shape[1] == 2 * mla_width
    wd["w_o"] = w_o[i].astype(BF16)
    wd["w_up"], wd["conv_w"], wd["conv_b"] = w_up[i].astype(BF16), conv_w[i], conv_b[i].reshape(1, 2 * d_ff)
    wd["w_down"] = w_down[i].astype(BF16)
    wd["w_ple_gate"], wd["w_ple"] = w_ple_gate[i].astype(BF16), w_ple[i].astype(BF16)
    return wd


def _mixer_inputs(x, wd, tabs, *, tm):
    d = wd["dims"]
    h = _rmsnorm(x, wd["g_attn"], TILES.norm_rows)
    cq, ckv, ckv_b, kr, kr_b = _latent(h, wd["w_lat"], wd["g_q_a"], wd["g_kv_a"], wd["g_k_rope_pad"], tabs,
                                      q_lora=d["q_lora"], kv_lora=d["kv_lora"], rope=d["rope"],
                                      tm=TILES.latent_rows)
    m = x.shape[0]
    n_tab = tabs[0].shape[0] // min(tm, m)
    tab_spec = pl.BlockSpec((min(tm, m), LANES), lambda i, j: (i % n_tab, 0))

    def qn_body(acc, g):
        return jnp.concatenate(_slab_rms(acc, g, d["nope"]), axis=-1)

    def qr_body(acc, acc_swap, g, g_swap, c, s):
        outs = []
        for a in range(acc.shape[1] // LANES):
            sl = slice(a * LANES, (a + 1) * LANES)
            x, xs = acc[:, sl], acc_swap[:, sl]
            rinv = lax.rsqrt(jnp.sum(x * x, axis=-1, keepdims=True) * (1.0 / d["rope"]) + EPS)
            outs.append((x * rinv * g[:, sl]) * c + (xs * rinv * g_swap[:, sl]) * s)
        return jnp.concatenate(outs, axis=-1)

    c_tab, sa_tab, sb_tab = tabs
    qn = _mm("q_nope", qn_body, [cq], [wd["w_qn"]], tm=tm, tn=TILES.cols_low_rank, out_dtypes=[BF16],
             cols=[wd["g_qn"]])
    qr = _mm("q_rope", qr_body, [cq, cq], [wd["w_qr"], wd["w_qr_swap"]], tm=tm, tn=TILES.cols_rope, out_dtypes=[BF16],
             group_sizes=(1, 1), cols=[wd["g_qr"], wd["g_qr_swap"]],
             extras=[(c_tab, tab_spec), (sa_tab + sb_tab, tab_spec)])

    def f_body(acc, lb):
        s = jax.nn.sigmoid(acc)
        log2_f = jnp.log2(lb + (1.0 - lb) * s)
        return log2_f, (1.0 - lb) * (1.0 - s)

    hw = d["hg_width"]
    proj = functools.partial(_mm, lhs=[h], rhs=[wd["w_hgrn"]], tm=tm, tn=TILES.cols_wide, n=hw)
    lf, hk = proj("hgrn_f", f_body, out_dtypes=[F32, BF16], cols=[wd["lb"]], rhs_col0=hw)
    hq, hv, hg = _mm("hgrn_qvg", lambda a, b, c: (a, b, c), [h], [wd["w_hgrn"]] * 3, tm=tm, tn=TILES.cols, n=hw,
                     group_sizes=(1, 1, 1), rhs_col0=(0, 2 * hw, 3 * hw), out_dtypes=[BF16] * 3)
    return (qn, qr, ckv, ckv_b, kr, kr_b), (hq, hk, hv, lf, hg)


def _out_proj(x, o_mla, o_hg, wd, *, tm):
    return _mm("out_proj", lambda acc, res: res + acc, [o_mla, o_hg], [wd["w_o"], wd["w_o"]], rhs_row_blocks=(0, 1),
               tm=tm, tn=TILES.cols_wide, out_dtypes=[F32], rows=[x])


def _down_ple(x, act, p, wd, *, tm):
    x, xg, ssq = _mm("ffn_down", lambda acc, res: res + acc, [act], [wd["w_down"]], tm=min(tm, TILES.down_rows),
                     tn=TILES.down_cols, out_dtypes=[F32], rows=[x], next_norm_gain=wd["g_ple"])
    width = x.shape[1]

    def ple_body(gate, emb, res, ssq_blk):
        return res + emb * jax.nn.sigmoid(gate * _row_rinv(ssq_blk, width))

    return _mm("ple", ple_body, [xg, p.astype(BF16)], [wd["w_ple_gate"], wd["w_ple"]], tm=tm, tn=TILES.cols,
               out_dtypes=[F32], group_sizes=(1, 1), rows=[x], extras=[(ssq, _ssq_spec(min(tm, x.shape[0])))])


def _prompt_layer(x, p, wd, *, batch, seq):
    d = wd["dims"]
    tm = min(TILES.rows, seq)
    assert seq % tm == 0
    tabs = _rope_tables(jnp.arange(seq), d["rope"])
    (qn, qr, ckv, ckv_b, kr, kr_b), (hq, hk, hv, lf, hg) = _mixer_inputs(x, wd, tabs, tm=tm)

    def kn_body(acc, g):
        return jnp.concatenate(_slab_rms(acc, g, d["nope"]), axis=-1)

    kn = _mm("k_nope", kn_body, [ckv_b], [wd["w_uk"]], tm=tm, tn=TILES.cols_low_rank, out_dtypes=[BF16],
             cols=[wd["g_kn"]])
    tq = min(TILES.attn, seq)
    vt = _value_t(ckv_b, wd["w_uv_t"], tk=tq, tn=TILES.cols_low_rank)
    o_mla = _attn_prompt(qn, qr, kn, kr_b, vt, batch=batch, seq=seq, heads=d["heads"], tq=tq, hb=TILES.attn_heads)

    hg_heads = d["hg_width"] // LANES
    s0 = jnp.zeros((batch, hg_heads, LANES, LANES), F32)
    o_hg, s_fin = _hgrn(hq, hk, hv, lf, hg, s0, wd["g_hg_out"], batch=batch, t_len=seq, blk=CHUNK,
                        tt=TILES.hgrn_rows, hb=TILES.hgrn_heads)

    x = _out_proj(x, o_mla, o_hg, wd, tm=tm)
    h = _rmsnorm(x, wd["g_ffn"], TILES.norm_rows)
    tm_up = min(TILES.ff_rows, seq)
    act, tail_g, tail_u = _up_conv_prompt(h, wd["w_up"], wd["conv_w"], wd["conv_b"], seq=seq, tm=tm_up,
                                          cm=TILES.ff_chunk)
    last = lambda t: t.reshape(batch, seq // tm_up, 8, -1)[:, -1, 6:, :]
    conv_state = jnp.concatenate([last(tail_g), last(tail_u)], axis=-1)
    x = _down_ple(x, act, p, wd, tm=tm)
    return x, ckv.reshape(batch, seq, -1), kr.reshape(batch, seq, -1), s_fin, conv_state


def _sample_layer(x, p, cache_ckv, cache_kr, state_hgrn, hist, wd, *, batch, t_new):
    d = wd["dims"]
    m = batch * t_new
    past = cache_ckv.shape[1]
    tabs = _rope_tables(jnp.tile(past + jnp.arange(t_new), batch), d["rope"])
    (qn, qr, ckv, ckv_b, kr, kr_b), (hq, hk, hv, lf, hg) = _mixer_inputs(x, wd, tabs, tm=m)

    qt = _per_head_mm("mla_absorb_q", qn, wd["w_absorb"])
    qr_h = jnp.transpose(qr.reshape(m, d["heads"], LANES), (1, 0, 2))
    o_lat = _attn_sample(qt, qr_h, cache_ckv, cache_kr, ckv_b, kr_b, wd["w_uk_t"], batch=batch, t_new=t_new,
                         heads=d["heads"], nope=d["nope"], rope=d["rope"], tk=TILES.sample_keys)
    o_mla = _value_up(o_lat, wd["w_uv_h"])

    o_hg, s_new = _hgrn(hq, hk, hv, lf, hg, state_hgrn, wd["g_hg_out"], batch=batch, t_len=t_new, blk=t_new,
                        tt=t_new, hb=TILES.hgrn_heads_short)

    x = _out_proj(x, o_mla, o_hg, wd, tm=m)
    h = _rmsnorm(x, wd["g_ffn"], TILES.norm_rows)
    d_ff = d["d_ff"]
    u = _mm("ffn_up_sample", lambda acc: acc, [h], [wd["w_up"]], tm=m, tn=2 * _ff_tile(d_ff), out_dtypes=[F32])
    u3 = u.reshape(batch, t_new, 2 * d_ff)
    act = _conv_sample(u3, hist, wd["conv_w"], wd["conv_b"])
    taps = wd["conv_w"].shape[0]
    conv_state = jnp.concatenate([hist, u3], axis=1)[:, -(taps - 1):]
    x = _down_ple(x, act.reshape(m, d_ff).astype(BF16), p, wd, tm=m)
    return x, ckv.reshape(batch, t_new, -1), kr.reshape(batch, t_new, -1), s_new, conv_state


def kernel(x_prompt, x_sample, cache_ckv, cache_k_rope, state_hgrn, state_ffn_conv, p_prompt, p_sample,
           g_attn_norm, w_in, g_q_a, w_uq, g_kv_a, w_uk, w_uv, g_q_nope, g_q_rope, g_k_nope, g_k_rope,
           lb_logits, g_hg_out, w_o, g_ffn_norm, w_up, conv_w, conv_b, w_down, g_ple_norm, w_ple_gate, w_ple):
    depth = w_in.shape[0]
    batch, seq, d_model = x_prompt.shape
    dec_batch, t_new, _ = x_sample.shape
    assert t_new >= conv_w.shape[1] - 1
    lb_all = jnp.cumsum(jax.nn.softmax(lb_logits.astype(F32), axis=0), axis=0)
    xp = x_prompt.reshape(batch * seq, d_model)
    xs = x_sample.reshape(dec_batch * t_new, d_model)
    outs = [[] for _ in range(8)]
    for i in range(depth):
        wd = _layer_weights(i, lb_all[i], g_attn_norm, w_in, g_q_a, w_uq, g_kv_a, w_uk, w_uv, g_q_nope, g_q_rope,
                            g_k_nope, g_k_rope, g_hg_out, w_o, g_ffn_norm, w_up, conv_w, conv_b, w_down,
                            g_ple_norm, w_ple_gate, w_ple)
        xp, *new_p = _prompt_layer(xp, p_prompt[i].reshape(batch * seq, -1), wd, batch=batch, seq=seq)
        xs, *new_s = _sample_layer(xs, p_sample[i].reshape(dec_batch * t_new, -1), cache_ckv[i], cache_k_rope[i],
                                   state_hgrn[i], state_ffn_conv[i], wd, batch=dec_batch, t_new=t_new)
        for lst, val in zip(outs, new_p + new_s):
            lst.append(val)
    return (xp.reshape(batch, seq, d_model), xs.reshape(dec_batch, t_new, d_model), *[jnp.stack(o) for o in outs])
```
